```python
import jax
import jax.numpy as jnp
from jax import lax
import numpy as np

D_MODEL = 1024
BATCH = 16
SEQ = 2048
DEPTH = 2
DEC_BATCH = 32
DEC_SEQ = 16
PAST_LEN = 4096

CHUNK = 64
Q_BLOCK = 128
N_EVEN = (DEPTH + 1) // 2
N_ODD = DEPTH // 2
EPS = 1e-6
ROPE_THETA = 10000.0

H_A = 4
DK_A = 64
DV_A = 128
GATE_RANK = 16
GATE_TAU = 16.0
H_B = 4
DK_B = 64
DV_B = 128
IN_AB = 2 * H_A * DK_A + 2 * H_A * DV_A + GATE_RANK + 2 * H_B * DK_B + 2 * H_B * DV_B
MIX_AB = H_A * DV_A + H_B * DV_B
H_C = 8
Q_LORA = 384
KV_LORA = 512
NOPE = 128
ROPE = 64
V_C = 128
IN_C = Q_LORA + KV_LORA + ROPE
D_FF = 2816
CONV_W = 3

kernel_name = "hybrid_gla_retnet_mla_convffn_stream_step"


def split_cols(x, sizes):
    out, start = [], 0
    for s in sizes:
        out.append(x[..., start:start + s])
        start += s
    return out


def rms_norm(x, g):
    xf = x.astype(jnp.float32)
    y = xf * lax.rsqrt(jnp.mean(xf * xf, axis=-1, keepdims=True) + EPS)
    return (y * g.astype(jnp.float32)).astype(x.dtype)


def head_rms_norm(o, g):
    return o * lax.rsqrt(jnp.mean(o * o, axis=-1, keepdims=True) + EPS) * g.astype(jnp.float32)


def head_group_norm(o, g):
    return head_rms_norm(o - jnp.mean(o, axis=-1, keepdims=True), g)


def rope(x, pos):
    half = x.shape[-1] // 2
    freqs = ROPE_THETA ** (-jnp.arange(half, dtype=jnp.float32) / half)
    ang = pos.astype(jnp.float32)[:, None] * freqs[None, :]
    ang = ang.reshape((1, ang.shape[0]) + (1,) * (x.ndim - 3) + (half,))
    cos, sin = jnp.cos(ang), jnp.sin(ang)
    x1 = x[..., :half].astype(jnp.float32)
    x2 = x[..., half:].astype(jnp.float32)
    return jnp.concatenate([x1 * cos - x2 * sin, x1 * sin + x2 * cos], axis=-1).astype(x.dtype)


def decay_linear_attention(q, k, v, log_a, s0):
    B, T, H, K = q.shape
    V = v.shape[-1]
    c = min(CHUNK, T)
    n = T // c
    f32 = jnp.float32
    q = q.astype(f32).reshape(B, n, c, H, K)
    k = k.astype(f32).reshape(B, n, c, H, K)
    v = v.astype(f32).reshape(B, n, c, H, V)
    b = jnp.cumsum(log_a.astype(f32).reshape(B, n, c, H, K), axis=2)
    b_ref = b[:, :, (c - 1) // 2][:, :, None]
    b_last = b[:, :, -1][:, :, None]
    q_rel = q * jnp.exp(b - b_ref)
    k_rel = k * jnp.exp(b_ref - b)
    causal = jnp.tril(jnp.ones((c, c), dtype=bool))
    scores = jnp.where(causal, jnp.einsum('bnthk,bnshk->bnhts', q_rel, k_rel), 0.0)
    o = jnp.einsum('bnhts,bnshv->bnthv', scores, v)
    chunk_kv = jnp.einsum('bnshk,bnshv->bnhkv', k * jnp.exp(b_last - b), v)
    chunk_decay = jnp.exp(b_last[:, :, 0])

    def step(s, inp):
        d, kv_c = inp
        return d[..., None] * s + kv_c, s

    s_final, s_start = lax.scan(step, s0.astype(f32),
                                (jnp.moveaxis(chunk_decay, 1, 0), jnp.moveaxis(chunk_kv, 1, 0)))
    s_start = jnp.moveaxis(s_start, 0, 1)
    o = o + jnp.einsum('bnthk,bnhkv->bnthv', q * jnp.exp(b), s_start)
    return o.reshape(B, T, H, V), s_final


def even_mixer(h, pos, s_gla, s_ret, w_in, w_gate_up, b_gate, g_gla, g_ret, w_out):
    B, T, _ = h.shape
    q_a, k_a, v_a, r_a, lo_a, q_b, k_b, v_b, g_b = split_cols(
        h @ w_in, [H_A * DK_A, H_A * DK_A, H_A * DV_A, H_A * DV_A, GATE_RANK,
                   H_B * DK_B, H_B * DK_B, H_B * DV_B, H_B * DV_B])
    log_a = jax.nn.log_sigmoid((lo_a @ w_gate_up + b_gate).astype(jnp.float32)) / GATE_TAU
    o_a, s_gla_new = decay_linear_attention(
        q_a.reshape(B, T, H_A, DK_A) * DK_A ** -0.5, k_a.reshape(B, T, H_A, DK_A),
        v_a.reshape(B, T, H_A, DV_A), log_a.reshape(B, T, H_A, DK_A), s_gla)
    o_a = head_rms_norm(o_a, g_gla.reshape(H_A, DV_A)) * \
        jax.nn.silu(r_a.astype(jnp.float32)).reshape(B, T, H_A, DV_A)
    qr = rope(q_b.reshape(B, T, H_B, DK_B), pos)
    kr = rope(k_b.reshape(B, T, H_B, DK_B), pos) * DK_B ** -0.5
    log_gamma = jnp.log1p(-jnp.exp2(-5.0 - jnp.arange(H_B, dtype=jnp.float32)))
    log_g = jnp.broadcast_to(log_gamma[:, None], (B, T, H_B, DK_B))
    o_b, s_ret_new = decay_linear_attention(qr, kr, v_b.reshape(B, T, H_B, DV_B), log_g, s_ret)
    o_b = head_group_norm(o_b, g_ret.reshape(H_B, DV_B)) * \
        jax.nn.silu(g_b.astype(jnp.float32)).reshape(B, T, H_B, DV_B)
    mix = jnp.concatenate([o_a.reshape(B, T, -1), o_b.reshape(B, T, -1)], axis=-1).astype(h.dtype)
    return mix @ w_out, s_gla_new, s_ret_new


def mla_attention(q_nope, q_rope, ckv, krope, q_pos, k_pos, w_uk, w_uv):
    B, T, H, _ = q_nope.shape
    qb = min(Q_BLOCK, T)
    nb = T // qb
    k_chunk = k_pos // CHUNK
    scale = (NOPE + ROPE) ** -0.5

    def block(args):
        qn, qr, qp = args
        q_lat = jnp.einsum('bqhn,lhn->bqhl', qn, w_uk)
        s = jnp.einsum('bqhl,bsl->bhqs', q_lat, ckv) + jnp.einsum('bqhr,bsr->bhqs', qr, krope)
        s = s.astype(jnp.float32) * scale
        mask = k_chunk[None, :] <= (qp // CHUNK)[:, None]
        p = jax.nn.softmax(jnp.where(mask, s, -jnp.inf), axis=-1).astype(ckv.dtype)
        o_lat = jnp.einsum('bhqs,bsl->bqhl', p, ckv)
        return jnp.einsum('bqhl,lhv->bqhv', o_lat, w_uv)

    def to_blocks(a):
        return jnp.moveaxis(a.reshape((B, nb, qb) + a.shape[2:]), 1, 0)

    o = lax.map(block, (to_blocks(q_nope), to_blocks(q_rope), q_pos.reshape(nb, qb)))
    return jnp.moveaxis(o, 0, 1).reshape(B, T, H, V_C)


def odd_mixer(h, pos, ckv_past, kr_past, w_in, g_q, g_kv, w_uq, w_uk, w_uv, w_out):
    B, T, _ = h.shape
    cq, ckv, kr = split_cols(h @ w_in, [Q_LORA, KV_LORA, ROPE])
    cq = rms_norm(cq, g_q)
    ckv = rms_norm(ckv, g_kv)
    kr = rope(kr, pos)
    q = (cq @ w_uq).reshape(B, T, H_C, NOPE + ROPE)
    q_nope = q[..., :NOPE]
    q_rope = rope(q[..., NOPE:], pos)
    if ckv_past is None:
        ckv_all, kr_all, k_pos = ckv, kr, pos
    else:
        ckv_all = jnp.concatenate([ckv_past.astype(ckv.dtype), ckv], axis=1)
        kr_all = jnp.concatenate([kr_past.astype(kr.dtype), kr], axis=1)
        k_pos = jnp.concatenate([jnp.arange(ckv_past.shape[1], dtype=jnp.int32), pos])
    o = mla_attention(q_nope, q_rope, ckv_all, kr_all, pos, k_pos, w_uk, w_uv)
    return o.reshape(B, T, H_C * V_C) @ w_out, ckv, kr


def conv_ffn(h, w_in, w_dw, b_dw, w_out, conv_prev):
    T = h.shape[1]
    a, u = split_cols(h @ w_in, [D_FF, D_FF])
    a_full = jnp.concatenate([conv_prev.astype(a.dtype), a], axis=1)
    c = b_dw + a_full[:, 0:T] * w_dw[0]
    for j in range(1, CONV_W):
        c = c + a_full[:, j:j + T] * w_dw[j]
    act = jax.nn.gelu(c.astype(jnp.float32), approximate=False) * u.astype(jnp.float32)
    return act.astype(h.dtype) @ w_out, a_full[:, T:]


def run_trunk(x, pos0, gla0, ret0, ckv_past, kr_past, conv0, w):
    B, T, _ = x.shape
    pos = pos0 + jnp.arange(T, dtype=jnp.int32)
    gla_new, ret_new, ckv_new, kr_new, conv_new = [], [], [], [], []
    for layer in range(DEPTH):
        i = layer // 2
        h = rms_norm(x, w['norm_mix'][layer])
        if layer % 2 == 0:
            s_gla = jnp.zeros((B, H_A, DK_A, DV_A), jnp.float32) if gla0 is None else gla0[i]
            s_ret = jnp.zeros((B, H_B, DK_B, DV_B), jnp.float32) if ret0 is None else ret0[i]
            out, s_gla, s_ret = even_mixer(h, pos, s_gla, s_ret, w['w_in_ab'][i], w['w_gate_up'][i],
                                           w['b_gate'][i], w['g_gla'][i], w['g_ret'][i], w['w_out_ab'][i])
            gla_new.append(s_gla.astype(x.dtype))
            ret_new.append(s_ret.astype(x.dtype))
        else:
            past_c = None if ckv_past is None else ckv_past[i]
            past_r = None if kr_past is None else kr_past[i]
            out, ckv, kr = odd_mixer(h, pos, past_c, past_r, w['w_in_c'][i], w['g_q'][i], w['g_kv'][i],
                                     w['w_uq'][i], w['w_uk'][i], w['w_uv'][i], w['w_out_c'][i])
            ckv_new.append(ckv)
            kr_new.append(kr)
        x = x + out.astype(x.dtype)
        h = rms_norm(x, w['norm_ffn'][layer])
        prev = jnp.zeros((B, CONV_W - 1, D_FF), x.dtype) if conv0 is None else conv0[layer]
        out, conv_rows = conv_ffn(h, w['w_ffn_in'][layer], w['w_dwconv'][layer], w['b_dwconv'][layer],
                                  w['w_ffn_out'][layer], prev)
        x = x + out.astype(x.dtype)
        conv_new.append(conv_rows)
    y = rms_norm(x, w['norm_final'])
    return (y, jnp.stack(gla_new), jnp.stack(ret_new), jnp.stack(ckv_new),
            jnp.stack(kr_new), jnp.stack(conv_new))


def setup_inputs(seed: int = 0) -> dict:
    key = jax.random.key(seed)
    ks = iter(jax.random.split(key, 40))

    def nrm(shape, scale):
        return jax.random.normal(next(ks), shape, jnp.float32) * scale

    def gain(shape):
        return 1.0 + 0.1 * jax.random.normal(next(ks), shape, jnp.float32)

    return {
        'x_prompt': nrm((BATCH, SEQ, D_MODEL), 1.0),
        'x_sample': nrm((DEC_BATCH, DEC_SEQ, D_MODEL), 1.0),
        'state_gla': nrm((N_EVEN, DEC_BATCH, H_A, DK_A, DV_A), 0.5),
        'state_ret': nrm((N_EVEN, DEC_BATCH, H_B, DK_B, DV_B), 1.0),
        'cache_ckv': nrm((N_ODD, DEC_BATCH, PAST_LEN, KV_LORA), 1.0),
        'cache_krope': nrm((N_ODD, DEC_BATCH, PAST_LEN, ROPE), 1.0),
        'state_conv': nrm((DEPTH, DEC_BATCH, CONV_W - 1, D_FF), 1.0),
        'norm_mix': gain((DEPTH, D_MODEL)),
        'norm_ffn': gain((DEPTH, D_MODEL)),
        'norm_final': gain((D_MODEL,)),
        'w_in_ab': nrm((N_EVEN, D_MODEL, IN_AB), D_MODEL ** -0.5),
        'w_gate_up': nrm((N_EVEN, GATE_RANK, H_A * DK_A), GATE_RANK ** -0.5),
        'b_gate': nrm((N_EVEN, H_A * DK_A), 0.1),
        'g_gla': gain((N_EVEN, H_A * DV_A)),
        'g_ret': gain((N_EVEN, H_B * DV_B)),
        'w_out_ab': nrm((N_EVEN, MIX_AB, D_MODEL), MIX_AB ** -0.5),
        'w_in_c': nrm((N_ODD, D_MODEL, IN_C), D_MODEL ** -0.5),
        'g_q': gain((N_ODD, Q_LORA)),
        'g_kv': gain((N_ODD, KV_LORA)),
        'w_uq': nrm((N_ODD, Q_LORA, H_C * (NOPE + ROPE)), Q_LORA ** -0.5),
        'w_uk': nrm((N_ODD, KV_LORA, H_C, NOPE), KV_LORA ** -0.5),
        'w_uv': nrm((N_ODD, KV_LORA, H_C, V_C), KV_LORA ** -0.5),
        'w_out_c': nrm((N_ODD, H_C * V_C, D_MODEL), (H_C * V_C) ** -0.5),
        'w_ffn_in': nrm((DEPTH, D_MODEL, 2 * D_FF), D_MODEL ** -0.5),
        'w_dwconv': nrm((DEPTH, CONV_W, D_FF), CONV_W ** -0.5),
        'b_dwconv': nrm((DEPTH, D_FF), 0.02),
        'w_ffn_out': nrm((DEPTH, D_FF, D_MODEL), D_FF ** -0.5),
    }


def reference(x_prompt, x_sample, state_gla, state_ret, cache_ckv, cache_krope, state_conv,
              norm_mix, norm_ffn, norm_final, w_in_ab, w_gate_up, b_gate, g_gla, g_ret, w_out_ab,
              w_in_c, g_q, g_kv, w_uq, w_uk, w_uv, w_out_c, w_ffn_in, w_dwconv, b_dwconv, w_ffn_out):
    w = {'norm_mix': norm_mix, 'norm_ffn': norm_ffn, 'norm_final': norm_final,
         'w_in_ab': w_in_ab, 'w_gate_up': w_gate_up, 'b_gate': b_gate, 'g_gla': g_gla,
         'g_ret': g_ret, 'w_out_ab': w_out_ab, 'w_in_c': w_in_c, 'g_q': g_q, 'g_kv': g_kv,
         'w_uq': w_uq, 'w_uk': w_uk, 'w_uv': w_uv, 'w_out_c': w_out_c, 'w_ffn_in': w_ffn_in,
         'w_dwconv': w_dwconv, 'b_dwconv': b_dwconv, 'w_ffn_out': w_ffn_out}
    y_prompt, gla_p, ret_p, ckv_p, kr_p, conv_p = run_trunk(
        x_prompt, 0, None, None, None, None, None, w)
    y_sample, gla_s, ret_s, ckv_s, kr_s, conv_s = run_trunk(
        x_sample, PAST_LEN, state_gla, state_ret, cache_ckv, cache_krope, state_conv, w)
    return (y_prompt, y_sample, gla_p, gla_s, ret_p, ret_s, ckv_p, ckv_s, kr_p, kr_s, conv_p, conv_s)
```

```python
import functools

import numpy as np
import jax
import jax.numpy as jnp
from jax import lax
from jax.experimental import pallas as pl
from jax.experimental.pallas import tpu as pltpu

F32 = jnp.float32
BF16 = jnp.bfloat16

D_MODEL = 1024
CHUNK = 64
EPS = 1e-6
ROPE_THETA = 10000.0
H_A, DK_A, DV_A = 4, 64, 128
GATE_RANK = 16
GATE_TAU = 16.0
H_B, DK_B, DV_B = 4, 64, 128
H_C = 8
Q_LORA, KV_LORA, NOPE, ROPE, V_C = 384, 512, 128, 64, 128
D_FF = 2816
CONV_W = 3

LANES = 128
SUBLANES = 8
VMEM_LIMIT = 56 * 1024 * 1024
AB_COLS = 3200
HEAD_W = 2 * LANES
TM_PROJ = 512
TM_FFN = 1024
TF_FFN = 256
ROWS_LINATTN = 256
TQ_FLASH = 512
TK_LATENT = 1024


def _params(n_axes):
    return pltpu.CompilerParams(dimension_semantics=("arbitrary",) * n_axes,
                                vmem_limit_bytes=VMEM_LIMIT)


def _rms(x, g):
    return x * lax.rsqrt(jnp.mean(x * x, axis=-1, keepdims=True) + EPS) * g


def _rope(x, cos, sin_signed):
    w = x.shape[1]
    half = ROPE // 2
    lane = lax.broadcasted_iota(jnp.int32, x.shape, 1)
    first = (lane % ROPE) < half
    swapped = jnp.where(first, pltpu.roll(x, w - half, 1), pltpu.roll(x, half, 1))
    return x * cos + swapped * sin_signed


def _nt_dot(a, b):
    return lax.dot_general(a, b, (((1,), (1,)), ((), ())), preferred_element_type=F32)


def _norm_matmul_kernel(x_ref, g_ref, w_ref, o_ref, *, col_chunk):
    h = _rms(x_ref[...], g_ref[...]).astype(BF16)
    n = w_ref.shape[1]
    for c0 in range(0, n, col_chunk):
        o_ref[:, c0:c0 + col_chunk] = jnp.dot(
            h, w_ref[:, c0:c0 + col_chunk], preferred_element_type=F32).astype(o_ref.dtype)


def _norm_matmul(x, g, w, tm, col_chunk):
    m, d = x.shape
    n = w.shape[1]
    return pl.pallas_call(
        functools.partial(_norm_matmul_kernel, col_chunk=col_chunk),
        grid=(m // tm,),
        in_specs=[pl.BlockSpec((tm, d), lambda i: (i, 0)),
                  pl.BlockSpec((1, d), lambda i: (0, 0)),
                  pl.BlockSpec((d, n), lambda i: (0, 0))],
        out_specs=pl.BlockSpec((tm, n), lambda i: (i, 0)),
        out_shape=jax.ShapeDtypeStruct((m, n), BF16),
        compiler_params=_params(1),
        name="norm_inproj_ab",
    )(x, g, w)


def _decay_attention_tile(q, k, v, bcum, s_ref, n_chunks, mid, last):
    c = CHUNK
    hk, hv = 4 * DK_A, 4 * DV_A
    r4 = lax.broadcasted_iota(jnp.int32, (4 * c, hk), 0)
    c4 = lax.broadcasted_iota(jnp.int32, (4 * c, hk), 1)
    head_mask_k = (r4 // c) == (c4 // DK_A)
    r5 = lax.broadcasted_iota(jnp.int32, (4 * c, hv), 0)
    c5 = lax.broadcasted_iota(jnp.int32, (4 * c, hv), 1)
    head_mask_v = (r5 // c) == (c5 // DV_A)
    rs = lax.broadcasted_iota(jnp.int32, (hk, hv), 0)
    cs = lax.broadcasted_iota(jnp.int32, (hk, hv), 1)
    head_mask_s = (rs // DK_A) == (cs // DV_A)
    rt = lax.broadcasted_iota(jnp.int32, (c, 4 * c), 0)
    ct = lax.broadcasted_iota(jnp.int32, (c, 4 * c), 1)
    causal = (ct % c) <= rt
    pad_k = jnp.zeros((LANES - c, hk), F32)
    pad_v = jnp.zeros((LANES - c, hv), BF16)

    outs = []
    for n in range(n_chunks):
        sl = slice(n * c, (n + 1) * c)
        qn, kn, vn, bn = q[sl], k[sl], v[sl], bcum[sl]
        b_ref = bn[mid:mid + 1]
        b_last = bn[last:last + 1]
        q_rel = (qn * jnp.exp(bn - b_ref)).astype(BF16)
        k_rel = kn * jnp.exp(b_ref - bn)
        k_dec = kn * jnp.exp(b_last - bn)
        q_dec = (qn * jnp.exp(bn)).astype(BF16)
        k_bd = jnp.where(head_mask_k, jnp.concatenate([k_rel] * 4, axis=0), 0.0).astype(BF16)
        scores = _nt_dot(q_rel, k_bd)
        scores = jnp.where(causal, scores, 0.0).astype(BF16)
        v_bd = jnp.where(head_mask_v, jnp.concatenate([vn] * 4, axis=0), 0.0).astype(BF16)
        o = jnp.dot(scores, v_bd, preferred_element_type=F32)
        s_old = s_ref[...]
        o = o + jnp.dot(q_dec, s_old.astype(BF16), preferred_element_type=F32)
        k_dec_t = jnp.concatenate([k_dec, pad_k], axis=0).T.astype(BF16)
        v_pad = jnp.concatenate([vn.astype(BF16), pad_v], axis=0)
        kv = jnp.dot(k_dec_t, v_pad, preferred_element_type=F32)
        decay_t = jnp.broadcast_to(jnp.exp(b_last), (LANES, hk)).T
        decay_full = jnp.concatenate([decay_t] * (hv // LANES), axis=1)
        s_ref[...] = s_old * decay_full + jnp.where(head_mask_s, kv, 0.0)
        outs.append(o)
    return outs[0] if n_chunks == 1 else jnp.concatenate(outs, axis=0)


def _linattn_kernel(qa_ref, ka_ref, va_ref, ra_ref, qb_ref, kb_ref, vb_ref, gb_ref, lo_ref,
                    wgu_ref, bg_ref, gg_ref, gr_ref, cos_ref, sin_ref, lgam_ref, *rest,
                    rows, has_state):
    if has_state:
        s0a_ref, s0b_ref, mix_ref, sa_out, sb_out, sa, sb = rest
    else:
        mix_ref, sa_out, sb_out, sa, sb = rest
    t = pl.program_id(1)
    c = CHUNK
    padded = max(rows, c)
    n_chunks = padded // c
    valid = min(rows, c)
    mid, last = (valid - 1) // 2, valid - 1

    @pl.when(t == 0)
    def _init():
        sa[...] = jnp.zeros_like(sa)
        sb[...] = jnp.zeros_like(sb)
        if has_state:
            for h in range(4):
                sa[h * DK_A:(h + 1) * DK_A, h * DV_A:(h + 1) * DV_A] = s0a_ref[0, h]
                sb[h * DK_B:(h + 1) * DK_B, h * DV_B:(h + 1) * DV_B] = s0b_ref[0, h]

    def pad_rows(x):
        if padded == rows:
            return x
        return jnp.concatenate([x, jnp.zeros((padded - rows, x.shape[1]), x.dtype)], axis=0)

    row = lax.broadcasted_iota(jnp.int32, (padded, padded), 0)
    col = lax.broadcasted_iota(jnp.int32, (padded, padded), 1)
    tri = jnp.where(((row // c) == (col // c)) & (col <= row), 1.0, 0.0).astype(BF16)

    gate = jnp.dot(lo_ref[...], wgu_ref[...], preferred_element_type=F32) + bg_ref[...]
    log_a = (jnp.minimum(gate, 0.0) - jnp.log1p(jnp.exp(-jnp.abs(gate)))) / GATE_TAU
    log_a = pad_rows(log_a)
    hi = log_a.astype(BF16)
    lo = (log_a - hi.astype(F32)).astype(BF16)
    bcum_a = (jnp.dot(tri, hi, preferred_element_type=F32)
              + jnp.dot(tri, lo, preferred_element_type=F32))
    qa = pad_rows(qa_ref[...].astype(F32) * DK_A ** -0.5)
    ka = pad_rows(ka_ref[...].astype(F32))
    va = pad_rows(va_ref[...].astype(F32))
    o_a = _decay_attention_tile(qa, ka, va, bcum_a, sa, n_chunks, mid, last)[:rows]

    cos, sin = cos_ref[...], sin_ref[...]
    qb = pad_rows(_rope(qb_ref[...].astype(F32), cos, sin))
    kb = pad_rows(_rope(kb_ref[...].astype(F32), cos, sin) * DK_B ** -0.5)
    vb = pad_rows(vb_ref[...].astype(F32))
    pos = lax.broadcasted_iota(jnp.int32, (padded, 4 * DK_B), 0) % c
    steps = jnp.where(pos < valid, pos + 1, valid).astype(F32)
    bcum_b = steps * lgam_ref[...]
    o_b = _decay_attention_tile(qb, kb, vb, bcum_b, sb, n_chunks, mid, last)[:rows]

    gg, gr = gg_ref[...], gr_ref[...]
    for h in range(4):
        sl = slice(h * DV_A, (h + 1) * DV_A)
        oh = o_a[:, sl]
        oh = oh * lax.rsqrt(jnp.mean(oh * oh, axis=-1, keepdims=True) + EPS) * gg[:, sl]
        r = ra_ref[:, sl].astype(F32)
        mix_ref[:, sl] = (oh * (r * jax.nn.sigmoid(r))).astype(mix_ref.dtype)
        ob = o_b[:, sl]
        ob = ob - jnp.mean(ob, axis=-1, keepdims=True)
        ob = ob * lax.rsqrt(jnp.mean(ob * ob, axis=-1, keepdims=True) + EPS) * gr[:, sl]
        gb = gb_ref[:, sl].astype(F32)
        mix_ref[:, 4 * DV_A + h * DV_B:4 * DV_A + (h + 1) * DV_B] = (
            ob * (gb * jax.nn.sigmoid(gb))).astype(mix_ref.dtype)

    @pl.when(t == pl.num_programs(1) - 1)
    def _emit_state():
        for h in range(4):
            sa_out[0, h] = sa[h * DK_A:(h + 1) * DK_A, h * DV_A:(h + 1) * DV_A]
            sb_out[0, h] = sb[h * DK_B:(h + 1) * DK_B, h * DV_B:(h + 1) * DV_B]


def _linattn(slab, batch, seq, rows, w_gu, b_gate, g_gla, g_ret, cos, sin, lgam, s0a, s0b):
    nt = seq // rows
    has_state = s0a is not None

    def col(width, idx):
        return pl.BlockSpec((rows, width), lambda b, t, idx=idx: (b * nt + t, idx))

    def const(shape):
        return pl.BlockSpec(shape, lambda b, t: (0,) * len(shape))

    hk = 4 * DK_A
    hv = 4 * DV_A
    in_specs = [col(hk, 0), col(hk, 1), col(hv, 1), col(hv, 2),
                col(hk, 6), col(hk, 7), col(hv, 4), col(hv, 5),
                col(LANES, 24),
                const((LANES, hk)), const((1, hk)), const((1, hv)), const((1, hv)),
                pl.BlockSpec((rows, hk), lambda b, t: (t, 0)),
                pl.BlockSpec((rows, hk), lambda b, t: (t, 0)),
                const((1, hk))]
    args = [slab] * 9 + [w_gu, b_gate, g_gla, g_ret, cos, sin, lgam]
    state_spec = pl.BlockSpec((1, 4, DK_A, DV_A), lambda b, t: (b, 0, 0, 0))
    if has_state:
        in_specs += [state_spec, state_spec]
        args += [s0a, s0b]
    state_shape = jax.ShapeDtypeStruct((batch, 4, DK_A, DV_A), F32)
    return pl.pallas_call(
        functools.partial(_linattn_kernel, rows=rows, has_state=has_state),
        grid=(batch, nt),
        in_specs=in_specs,
        out_specs=[pl.BlockSpec((rows, 2 * hv), lambda b, t: (b * nt + t, 0)), state_spec, state_spec],
        out_shape=[jax.ShapeDtypeStruct((batch * seq, 2 * hv), BF16), state_shape, state_shape],
        scratch_shapes=[pltpu.VMEM((hk, hv), F32), pltpu.VMEM((hk, hv), F32)],
        compiler_params=_params(2),
        name="decay_linear_attention",
    )(*args)


def _ffn_kernel(x_ref, mix_ref, wo_ref, g_ref, wa_ref, wu_ref, wdw_ref, bdw_ref, wout_ref, st_ref,
                gf_ref, y_ref, cv_ref, h_s, acc_s, e_s, carry_s,
                *, tm, prev_rows, shift, tiles_per_seq, final_norm):
    i = pl.program_id(0)
    j = pl.program_id(1)
    p = prev_rows

    @pl.when(j == 0)
    def _start():
        xm = x_ref[...] + jnp.dot(mix_ref[...], wo_ref[...], preferred_element_type=F32)
        acc_s[...] = xm
        h_s[...] = _rms(xm, g_ref[...]).astype(BF16)

    h = h_s[...]
    a = jnp.dot(h, wa_ref[...], preferred_element_type=F32)
    u = jnp.dot(h, wu_ref[...], preferred_element_type=F32)

    at_start = (i % tiles_per_seq) == 0

    @pl.when(at_start)
    def _from_state():
        e_s[0:p] = st_ref[0]

    @pl.when(jnp.logical_not(at_start))
    def _from_carry():
        e_s[0:p] = carry_s[j]

    e_s[p:p + tm] = a
    wdw = wdw_ref[...]
    conv = bdw_ref[...] + e_s[p - 2 * shift:p - 2 * shift + tm] * wdw[0:1]
    conv = conv + e_s[p - shift:p - shift + tm] * wdw[1:2]
    conv = conv + a * wdw[2:3]
    act = 0.5 * conv * (1.0 + lax.erf(conv * np.float32(np.sqrt(0.5)))) * u
    acc_s[...] += jnp.dot(act.astype(BF16), wout_ref[...], preferred_element_type=F32)
    carry_s[j] = e_s[tm:tm + p]
    cv_ref[0] = e_s[p + tm - 2 * shift:p + tm]

    @pl.when(j == pl.num_programs(1) - 1)
    def _finish():
        out = acc_s[...]
        if final_norm:
            out = _rms(out, gf_ref[...])
        y_ref[...] = out


def _ffn(x, mix, w_o, g_ffn, w_in, w_dw, b_dw, w_out, conv_state, g_final, *, tm, tf, shift,
         tiles_per_seq, final_norm):
    m, d = x.shape
    n_f = D_FF // tf
    n_seq, prev_rows, _ = conv_state.shape
    kern = functools.partial(_ffn_kernel, tm=tm, prev_rows=prev_rows, shift=shift,
                             tiles_per_seq=tiles_per_seq, final_norm=final_norm)
    return pl.pallas_call(
        kern,
        grid=(m // tm, n_f),
        in_specs=[pl.BlockSpec((tm, d), lambda i, j: (i, 0)),
                  pl.BlockSpec((tm, d), lambda i, j: (i, 0)),
                  pl.BlockSpec((d, d), lambda i, j: (0, 0)),
                  pl.BlockSpec((1, d), lambda i, j: (0, 0)),
                  pl.BlockSpec((d, tf), lambda i, j: (0, j)),
                  pl.BlockSpec((d, tf), lambda i, j: (0, n_f + j)),
                  pl.BlockSpec((CONV_W, tf), lambda i, j: (0, j)),
                  pl.BlockSpec((1, tf), lambda i, j: (0, j)),
                  pl.BlockSpec((tf, d), lambda i, j: (j, 0)),
                  pl.BlockSpec((1, prev_rows, tf), lambda i, j: (i // tiles_per_seq, 0, j)),
                  pl.BlockSpec((1, d), lambda i, j: (0, 0))],
        out_specs=[pl.BlockSpec((tm, d), lambda i, j: (i, 0)),
                   pl.BlockSpec((1, 2 * shift, tf), lambda i, j: (i, 0, j))],
        out_shape=[jax.ShapeDtypeStruct((m, d), F32),
                   jax.ShapeDtypeStruct((m // tm, 2 * shift, D_FF), F32)],
        scratch_shapes=[pltpu.VMEM((tm, d), BF16), pltpu.VMEM((tm, d), F32),
                        pltpu.VMEM((prev_rows + tm, tf), F32),
                        pltpu.VMEM((n_f, prev_rows, tf), F32)],
        compiler_params=_params(2),
        name="outproj_convffn",
    )(x, mix, w_o, g_ffn, w_in, w_in, w_dw, b_dw, w_out, conv_state, g_final)


def _mla_latents(x_ref, g_ref, win_ref, gq_ref, gkv_ref, cos_ref, sin_ref, ckv_ref, kr_ref):
    h = _rms(x_ref[...], g_ref[...]).astype(BF16)
    p = jnp.dot(h, win_ref[...], preferred_element_type=F32)
    cq_n = _rms(p[:, 0:Q_LORA], gq_ref[...]).astype(BF16)
    ckv_n = _rms(p[:, Q_LORA:Q_LORA + KV_LORA], gkv_ref[...])
    ckv_ref[...] = ckv_n
    kr = _rope(p[:, Q_LORA + KV_LORA:], cos_ref[...], sin_ref[...])
    kr_ref[...] = kr[:, 0:ROPE]
    return cq_n, ckv_n, kr


def _mla_proj_prompt_kernel(x_ref, g_ref, win_ref, gq_ref, gkv_ref, wuq_ref, wukv_ref, cos_ref,
                            sin_ref, q_ref, k_ref, v_ref, ckv_ref, kr_ref, *, scale):
    cq_n, ckv_n, kr = _mla_latents(x_ref, g_ref, win_ref, gq_ref, gkv_ref, cos_ref, sin_ref,
                                   ckv_ref, kr_ref)
    cos, sin = cos_ref[...], sin_ref[...]
    kr_b = kr.astype(BF16)
    ckv_b = ckv_n.astype(BF16)
    for hh in range(H_C):
        qh = jnp.dot(cq_n, wuq_ref[:, hh * HEAD_W:(hh + 1) * HEAD_W],
                     preferred_element_type=F32) * scale
        q_ref[hh, :, 0:NOPE] = qh[:, 0:NOPE].astype(BF16)
        q_ref[hh, :, NOPE:HEAD_W] = _rope(qh[:, NOPE:HEAD_W], cos, sin).astype(BF16)
        kvh = jnp.dot(ckv_b, wukv_ref[:, hh * HEAD_W:(hh + 1) * HEAD_W],
                      preferred_element_type=F32)
        k_ref[hh, :, 0:NOPE] = kvh[:, 0:NOPE].astype(BF16)
        k_ref[hh, :, NOPE:HEAD_W] = kr_b
        v_ref[hh] = kvh[:, NOPE:HEAD_W].astype(BF16)


def _mla_proj_prompt(x, g, w_in, g_q, g_kv, w_uq, w_ukv, cos, sin, seq, tm, scale):
    m, d = x.shape
    nt = seq // tm
    const2 = lambda shape: pl.BlockSpec(shape, lambda i: (0, 0))
    return pl.pallas_call(
        functools.partial(_mla_proj_prompt_kernel, scale=scale),
        grid=(m // tm,),
        in_specs=[pl.BlockSpec((tm, d), lambda i: (i, 0)), const2((1, d)), const2(w_in.shape),
                  const2((1, Q_LORA)), const2((1, KV_LORA)), const2(w_uq.shape), const2(w_ukv.shape),
                  pl.BlockSpec((tm, LANES), lambda i: (i % nt, 0)),
                  pl.BlockSpec((tm, LANES), lambda i: (i % nt, 0))],
        out_specs=[pl.BlockSpec((H_C, tm, HEAD_W), lambda i: (0, i, 0)),
                   pl.BlockSpec((H_C, tm, HEAD_W), lambda i: (0, i, 0)),
                   pl.BlockSpec((H_C, tm, V_C), lambda i: (0, i, 0)),
                   pl.BlockSpec((tm, KV_LORA), lambda i: (i, 0)),
                   pl.BlockSpec((tm, ROPE), lambda i: (i, 0))],
        out_shape=[jax.ShapeDtypeStruct((H_C, m, HEAD_W), BF16),
                   jax.ShapeDtypeStruct((H_C, m, HEAD_W), BF16),
                   jax.ShapeDtypeStruct((H_C, m, V_C), BF16),
                   jax.ShapeDtypeStruct((m, KV_LORA), F32),
                   jax.ShapeDtypeStruct((m, ROPE), F32)],
        compiler_params=_params(1),
        name="mla_proj_prompt",
    )(x, g, w_in, g_q, g_kv, w_uq, w_ukv, cos, sin)


def _mla_proj_sample_kernel(x_ref, g_ref, win_ref, gq_ref, gkv_ref, wuq_ref, wukt_ref, cos_ref,
                            sin_ref, qlat_ref, qr_ref, ckv_ref, kr_ref, *, scale):
    cq_n, _, _ = _mla_latents(x_ref, g_ref, win_ref, gq_ref, gkv_ref, cos_ref, sin_ref,
                              ckv_ref, kr_ref)
    cos, sin = cos_ref[...], sin_ref[...]
    for hh in range(H_C):
        qh = jnp.dot(cq_n, wuq_ref[:, hh * HEAD_W:(hh + 1) * HEAD_W],
                     preferred_element_type=F32)
        q_lat = jnp.dot(qh[:, 0:NOPE].astype(BF16), wukt_ref[hh], preferred_element_type=F32)
        qlat_ref[hh] = (q_lat * scale).astype(BF16)
        qr_ref[hh] = (_rope(qh[:, NOPE:HEAD_W], cos, sin) * scale)[:, 0:ROPE].astype(BF16)


def _mla_proj_sample(x, g, w_in, g_q, g_kv, w_uq, w_ukt, cos, sin, scale):
    m, d = x.shape
    const = lambda shape: pl.BlockSpec(shape, lambda i: (0,) * len(shape))
    return pl.pallas_call(
        functools.partial(_mla_proj_sample_kernel, scale=scale),
        grid=(1,),
        in_specs=[const((m, d)), const((1, d)), const(w_in.shape), const((1, Q_LORA)),
                  const((1, KV_LORA)), const(w_uq.shape), const(w_ukt.shape),
                  const((m, LANES)), const((m, LANES))],
        out_specs=[const((H_C, m, KV_LORA)), const((H_C, m, ROPE)), const((m, KV_LORA)),
                   const((m, ROPE))],
        out_shape=[jax.ShapeDtypeStruct((H_C, m, KV_LORA), BF16),
                   jax.ShapeDtypeStruct((H_C, m, ROPE), BF16),
                   jax.ShapeDtypeStruct((m, KV_LORA), F32),
                   jax.ShapeDtypeStruct((m, ROPE), F32)],
        compiler_params=_params(1),
        name="mla_proj_sample",
    )(x, g, w_in, g_q, g_kv, w_uq, w_ukt, cos, sin)


def _flash_kernel(q_ref, k_ref, v_ref, o_ref, m_s, l_s, acc_s, *, tq):
    i = pl.program_id(2)
    q = q_ref[0]
    m_s[...] = jnp.full_like(m_s, -jnp.inf)
    l_s[...] = jnp.zeros_like(l_s)
    acc_s[...] = jnp.zeros_like(acc_s)

    def block(j, masked):
        start = pl.multiple_of(j * tq, tq)
        kj = k_ref[0, pl.ds(start, tq), :]
        vj = v_ref[0, pl.ds(start, tq), :]
        s = _nt_dot(q, kj)
        if masked:
            row = lax.broadcasted_iota(jnp.int32, (tq, tq), 0)
            col = lax.broadcasted_iota(jnp.int32, (tq, tq), 1)
            s = jnp.where((col // CHUNK) <= (row // CHUNK), s, -jnp.inf)
        m_prev = m_s[...]
        m_new = jnp.maximum(m_prev, jnp.max(s, axis=-1, keepdims=True))
        p = jnp.exp(s - m_new)
        alpha = jnp.exp(m_prev - m_new)
        l_s[...] = alpha * l_s[...] + jnp.sum(p, axis=-1, keepdims=True)
        acc_s[...] = alpha * acc_s[...] + jnp.dot(p.astype(BF16), vj, preferred_element_type=F32)
        m_s[...] = m_new

    def body(j, carry):
        block(j, False)
        return carry

    lax.fori_loop(0, i, body, 0)
    block(i, True)
    o_ref[...] = (acc_s[...] / l_s[...]).astype(o_ref.dtype)


def _flash_prompt(q, k, v, batch, seq, tq):
    nq = seq // tq
    m = batch * seq
    return pl.pallas_call(
        functools.partial(_flash_kernel, tq=tq),
        grid=(batch, H_C, nq),
        in_specs=[pl.BlockSpec((1, tq, HEAD_W), lambda b, h, i: (h, b * nq + i, 0)),
                  pl.BlockSpec((1, seq, HEAD_W), lambda b, h, i: (h, b, 0)),
                  pl.BlockSpec((1, seq, V_C), lambda b, h, i: (h, b, 0))],
        out_specs=pl.BlockSpec((tq, V_C), lambda b, h, i: (b * nq + i, h)),
        out_shape=jax.ShapeDtypeStruct((m, H_C * V_C), BF16),
        scratch_shapes=[pltpu.VMEM((tq, 1), F32), pltpu.VMEM((tq, 1), F32),
                        pltpu.VMEM((tq, V_C), F32)],
        compiler_params=_params(3),
        name="flash_prompt",
    )(q, k, v)


def _latent_attn_kernel(qlat_ref, qr_ref, cckv_ref, ckr_ref, nckv_ref, nkr_ref, wuv_ref, o_ref,
                        m_s, l_s, acc_s, *, steps, new_rows):
    j = pl.program_id(1)
    rows = H_C * steps
    q_lat = qlat_ref[...].reshape(rows, KV_LORA)
    q_r = qr_ref[...].reshape(rows, ROPE)

    @pl.when(j == 0)
    def _init():
        m_s[...] = jnp.full_like(m_s, -jnp.inf)
        l_s[...] = jnp.zeros_like(l_s)
        acc_s[...] = jnp.zeros_like(acc_s)

    def update(ckv, kr, valid):
        ckv_b = ckv.astype(BF16)
        s = _nt_dot(q_lat, ckv_b) + _nt_dot(q_r, kr.astype(BF16))
        if valid is not None:
            col = lax.broadcasted_iota(jnp.int32, s.shape, 1)
            s = jnp.where(col < valid, s, -jnp.inf)
        m_prev = m_s[...]
        m_new = jnp.maximum(m_prev, jnp.max(s, axis=-1, keepdims=True))
        p = jnp.exp(s - m_new)
        alpha = jnp.exp(m_prev - m_new)
        l_s[...] = alpha * l_s[...] + jnp.sum(p, axis=-1, keepdims=True)
        acc_s[...] = alpha * acc_s[...] + jnp.dot(p.astype(BF16), ckv_b,
                                                  preferred_element_type=F32)
        m_s[...] = m_new

    update(cckv_ref[0], ckr_ref[0], None)

    @pl.when(j == pl.num_programs(1) - 1)
    def _finish():
        update(nckv_ref[0], nkr_ref[0], new_rows)
        o_lat = (acc_s[...] / l_s[...]).astype(BF16)
        for hh in range(H_C):
            o_ref[:, hh * V_C:(hh + 1) * V_C] = jnp.dot(
                o_lat[hh * steps:(hh + 1) * steps], wuv_ref[hh],
                preferred_element_type=F32).astype(o_ref.dtype)


def _latent_attn_sample(q_lat, q_r, cache_ckv, cache_kr, new_ckv, new_kr, w_uv, steps, tk):
    batch, past, _ = cache_ckv.shape
    pad_rows = new_ckv.shape[1]
    return pl.pallas_call(
        functools.partial(_latent_attn_kernel, steps=steps, new_rows=steps),
        grid=(batch, past // tk),
        in_specs=[pl.BlockSpec((H_C, steps, KV_LORA), lambda b, j: (0, b, 0)),
                  pl.BlockSpec((H_C, steps, ROPE), lambda b, j: (0, b, 0)),
                  pl.BlockSpec((1, tk, KV_LORA), lambda b, j: (b, j, 0)),
                  pl.BlockSpec((1, tk, ROPE), lambda b, j: (b, j, 0)),
                  pl.BlockSpec((1, pad_rows, KV_LORA), lambda b, j: (b, 0, 0)),
                  pl.BlockSpec((1, pad_rows, ROPE), lambda b, j: (b, 0, 0)),
                  pl.BlockSpec((H_C, KV_LORA, V_C), lambda b, j: (0, 0, 0))],
        out_specs=pl.BlockSpec((steps, H_C * V_C), lambda b, j: (b, 0)),
        out_shape=jax.ShapeDtypeStruct((batch * steps, H_C * V_C), BF16),
        scratch_shapes=[pltpu.VMEM((H_C * steps, 1), F32), pltpu.VMEM((H_C * steps, 1), F32),
                        pltpu.VMEM((H_C * steps, KV_LORA), F32)],
        compiler_params=_params(2),
        name="latent_attn_sample",
    )(q_lat, q_r, cache_ckv, cache_kr, new_ckv, new_kr, w_uv)


def _rope_tables(pos, width):
    half = ROPE // 2
    freqs = ROPE_THETA ** (-jnp.arange(half, dtype=F32) / half)
    ang = pos.astype(F32)[:, None] * freqs[None, :]
    cos, sin = jnp.cos(ang), jnp.sin(ang)
    cos_h = jnp.concatenate([cos, cos], axis=-1)
    sin_h = jnp.concatenate([-sin, sin], axis=-1)
    reps = width // ROPE
    return jnp.tile(cos_h, (1, reps)), jnp.tile(sin_h, (1, reps))


def _prep_even(w_in_ab, w_gate_up):
    sizes = [H_A * DK_A, H_A * DK_A, H_A * DV_A, H_A * DV_A, GATE_RANK,
             H_B * DK_B, H_B * DK_B, H_B * DV_B, H_B * DV_B]
    offs = np.concatenate([[0], np.cumsum(sizes)])
    parts = [w_in_ab[:, offs[n]:offs[n + 1]] for n in range(9)]
    lo = jnp.pad(parts[4], ((0, 0), (0, LANES - GATE_RANK)))
    w = jnp.concatenate(parts[0:4] + parts[5:9] + [lo], axis=1).astype(BF16)
    w_gu = jnp.pad(w_gate_up, ((0, LANES - GATE_RANK), (0, 0))).astype(BF16)
    return w, w_gu


def _prep_odd(w_in_c, w_uq, w_uk, w_uv):
    d = w_in_c.shape[0]
    w_in = jnp.pad(w_in_c, ((0, 0), (0, LANES - ROPE))).astype(BF16)
    uq = w_uq.reshape(Q_LORA, H_C, NOPE + ROPE)
    uq = jnp.pad(uq, ((0, 0), (0, 0), (0, HEAD_W - NOPE - ROPE)))
    w_uq_p = uq.reshape(Q_LORA, H_C * HEAD_W).astype(BF16)
    w_ukv = jnp.concatenate([w_uk, w_uv], axis=2).reshape(KV_LORA, H_C * HEAD_W).astype(BF16)
    w_ukt = jnp.transpose(w_uk, (1, 2, 0)).astype(BF16)
    w_uvh = jnp.transpose(w_uv, (1, 0, 2)).astype(BF16)
    del d
    return w_in, w_uq_p, w_ukv, w_ukt, w_uvh


def _log_gamma_row():
    lg = np.log1p(-np.exp2(-5.0 - np.arange(H_B, dtype=np.float32))).astype(np.float32)
    return jnp.asarray(np.repeat(lg, DK_B)[None, :])


def _to_time_major(a, batch, steps):
    return a.reshape(batch, steps, -1).transpose(1, 0, 2).reshape(batch * steps, -1)


def _to_batch_major(a, batch, steps):
    return a.reshape(steps, batch, -1).transpose(1, 0, 2).reshape(batch * steps, -1)


def _divisor_tile(n, pref):
    t = min(n, pref)
    while n % t:
        t //= 2
    return t


def _trunk(x, pos0, gla0, ret0, ckv_past, kr_past, conv0, w, time_major_ffn):
    batch, seq, d = x.shape
    m = batch * seq
    pos = pos0 + jnp.arange(seq, dtype=jnp.int32)
    row = lambda v: v.reshape(1, -1)
    scale = float((NOPE + ROPE) ** -0.5)
    tf = TF_FFN
    tm_proj = _divisor_tile(m, TM_PROJ)

    if time_major_ffn:
        ffn_kw = dict(tm=m, tf=tf, shift=batch, tiles_per_seq=1)
        to_ffn = lambda a: _to_time_major(a, batch, seq)
        from_ffn = lambda a: _to_batch_major(a, batch, seq)
        prep_state = lambda s: s.transpose(1, 0, 2).reshape(1, 2 * batch, D_FF)
        post_state = lambda s: s.reshape(2, batch, D_FF).transpose(1, 0, 2)
    else:
        tm_ffn = _divisor_tile(seq, TM_FFN)
        ffn_kw = dict(tm=tm_ffn, tf=tf, shift=1, tiles_per_seq=seq // tm_ffn)
        to_ffn = from_ffn = lambda a: a
        prep_state = lambda s: jnp.pad(s, ((0, 0), (SUBLANES - 2, 0), (0, 0)))
        post_state = lambda s: s

    xf = x.reshape(m, d)
    conv_new = []
    gla_new = ret_new = ckv_new = kr_new = None
    for layer in range(2):
        g_mix = row(w['norm_mix'][layer])
        if layer == 0:
            w_ab, w_gu = _prep_even(w['w_in_ab'][0], w['w_gate_up'][0])
            slab = _norm_matmul(xf, g_mix, w_ab, tm_proj, 640)
            cos, sin = _rope_tables(pos, 4 * DK_B)
            rows = _divisor_tile(seq, ROWS_LINATTN)
            mix, gla_new, ret_new = _linattn(
                slab, batch, seq, rows, w_gu, row(w['b_gate'][0]), row(w['g_gla'][0]),
                row(w['g_ret'][0]), cos, sin, _log_gamma_row(),
                None if gla0 is None else gla0[0], None if ret0 is None else ret0[0])
            w_o = w['w_out_ab'][0].astype(BF16)
        else:
            w_in, w_uq_p, w_ukv, w_ukt, w_uvh = _prep_odd(w['w_in_c'][0], w['w_uq'][0], w['w_uk'][0],
                                                          w['w_uv'][0])
            cos, sin = _rope_tables(pos, LANES)
            g_q, g_kv = row(w['g_q'][0]), row(w['g_kv'][0])
            if ckv_past is None:
                q, k, v, ckv_new, kr_new = _mla_proj_prompt(
                    xf, g_mix, w_in, g_q, g_kv, w_uq_p, w_ukv, cos, sin, seq, tm_proj, scale)
                mix = _flash_prompt(q, k, v, batch, seq, _divisor_tile(seq, TQ_FLASH))
            else:
                assert ckv_past.shape[2] % CHUNK == 0 and seq <= CHUNK
                cos_m, sin_m = jnp.tile(cos, (batch, 1)), jnp.tile(sin, (batch, 1))
                q_lat, q_r, ckv_new, kr_new = _mla_proj_sample(
                    xf, g_mix, w_in, g_q, g_kv, w_uq_p, w_ukt, cos_m, sin_m, scale)
                pad = ((0, 0), (0, LANES - seq), (0, 0))
                new_ckv = jnp.pad(ckv_new.reshape(batch, seq, KV_LORA), pad)
                new_kr = jnp.pad(kr_new.reshape(batch, seq, ROPE), pad)
                mix = _latent_attn_sample(q_lat, q_r, ckv_past[0], kr_past[0], new_ckv, new_kr,
                                          w_uvh, seq, _divisor_tile(ckv_past.shape[2], TK_LATENT))
            w_o = w['w_out_c'][0].astype(BF16)

        if conv0 is None:
            state = jnp.zeros((batch, SUBLANES, D_FF), F32) if not time_major_ffn else \
                jnp.zeros((1, 2 * batch, D_FF), F32)
        else:
            state = prep_state(conv0[layer])
        y, conv_rows = _ffn(to_ffn(xf), to_ffn(mix), w_o, row(w['norm_ffn'][layer]),
                            w['w_ffn_in'][layer].astype(BF16), w['w_dwconv'][layer],
                            row(w['b_dwconv'][layer]), w['w_ffn_out'][layer].astype(BF16), state,
                            row(w['norm_final']), final_norm=(layer == 1), **ffn_kw)
        xf = from_ffn(y)
        tps = ffn_kw['tiles_per_seq']
        conv_new.append(post_state(conv_rows[tps - 1::tps]))

    return (xf.reshape(batch, seq, d), gla_new[None], ret_new[None],
            ckv_new.reshape(1, batch, seq, KV_LORA), kr_new.reshape(1, batch, seq, ROPE),
            jnp.stack(conv_new))


def kernel(x_prompt, x_sample, state_gla, state_ret, cache_ckv, cache_krope, state_conv, norm_mix, norm_ffn, norm_final, w_in_ab, w_gate_up, b_gate, g_gla, g_ret, w_out_ab, w_in_c, g_q, g_kv, w_uq, w_uk, w_uv, w_out_c, w_ffn_in, w_dwconv, b_dwconv, w_ffn_out):
    w = {'norm_mix': norm_mix, 'norm_ffn': norm_ffn, 'norm_final': norm_final,
         'w_in_ab': w_in_ab, 'w_gate_up': w_gate_up, 'b_gate': b_gate, 'g_gla': g_gla,
         'g_ret': g_ret, 'w_out_ab': w_out_ab, 'w_in_c': w_in_c, 'g_q': g_q, 'g_kv': g_kv,
         'w_uq': w_uq, 'w_uk': w_uk, 'w_uv': w_uv, 'w_out_c': w_out_c, 'w_ffn_in': w_ffn_in,
         'w_dwconv': w_dwconv, 'b_dwconv': b_dwconv, 'w_ffn_out': w_ffn_out}
    past_len = cache_ckv.shape[2]
    y_p, gla_p, ret_p, ckv_p, kr_p, conv_p = _trunk(
        x_prompt, 0, None, None, None, None, None, w, time_major_ffn=False)
    y_s, gla_s, ret_s, ckv_s, kr_s, conv_s = _trunk(
        x_sample, past_len, state_gla, state_ret, cache_ckv, cache_krope, state_conv, w,
        time_major_ffn=True)
    return (y_p, y_s, gla_p, gla_s, ret_p, ret_s, ckv_p, ckv_s, kr_p, kr_s, conv_p, conv_s)
```

```python
import functools

import numpy as np
import jax
import jax.numpy as jnp
from jax import lax
from jax.experimental import pallas as pl
from jax.experimental.pallas import tpu as pltpu

F32 = jnp.float32
BF16 = jnp.bfloat16

D_MODEL = 1024
CHUNK = 64
EPS = 1e-6
ROPE_THETA = 10000.0
H_A, DK_A, DV_A = 4, 64, 128
GATE_RANK = 16
GATE_TAU = 16.0
H_B, DK_B, DV_B = 4, 64, 128
H_C = 8
Q_LORA, KV_LORA, NOPE, ROPE, V_C = 384, 512, 128, 64, 128
D_FF = 2816
CONV_W = 3

LANES = 128
SUBLANES = 8
VMEM_LIMIT = 56 * 1024 * 1024
AB_COLS = 3200
HEAD_W = 2 * LANES
TM_PROJ = 512
TM_FFN = 1024
TF_FFN = 256
SUB_FFN = 256
ROWS_LINATTN = 256
TQ_FLASH = 256
TK_LATENT = 1024


def _params(n_axes):
    return pltpu.CompilerParams(dimension_semantics=("arbitrary",) * n_axes,
                                vmem_limit_bytes=VMEM_LIMIT)


def _rms(x, g):
    return x * lax.rsqrt(jnp.mean(x * x, axis=-1, keepdims=True) + EPS) * g


def _rope(x, cos, sin_signed):
    w = x.shape[1]
    half = ROPE // 2
    lane = lax.broadcasted_iota(jnp.int32, x.shape, 1)
    first = (lane % ROPE) < half
    swapped = jnp.where(first, pltpu.roll(x, w - half, 1), pltpu.roll(x, half, 1))
    return x * cos + swapped * sin_signed


def _nt_dot(a, b):
    return lax.dot_general(a, b, (((1,), (1,)), ((), ())), preferred_element_type=F32)


def _norm_matmul_kernel(x_ref, g_ref, w_ref, o_ref, *, col_chunk):
    h = _rms(x_ref[...], g_ref[...]).astype(BF16)
    n = w_ref.shape[1]
    for c0 in range(0, n, col_chunk):
        o_ref[:, c0:c0 + col_chunk] = jnp.dot(
            h, w_ref[:, c0:c0 + col_chunk], preferred_element_type=F32).astype(o_ref.dtype)


def _norm_matmul(x, g, w, tm, col_chunk):
    m, d = x.shape
    n = w.shape[1]
    return pl.pallas_call(
        functools.partial(_norm_matmul_kernel, col_chunk=col_chunk),
        grid=(m // tm,),
        in_specs=[pl.BlockSpec((tm, d), lambda i: (i, 0)),
                  pl.BlockSpec((1, d), lambda i: (0, 0)),
                  pl.BlockSpec((d, n), lambda i: (0, 0))],
        out_specs=pl.BlockSpec((tm, n), lambda i: (i, 0)),
        out_shape=jax.ShapeDtypeStruct((m, n), BF16),
        compiler_params=_params(1),
        name="norm_inproj_ab",
    )(x, g, w)


def _decay_attention_tile(q, k, v, bcum, s_ref, n_chunks, mid, last):
    c = CHUNK
    hk, hv = 4 * DK_A, 4 * DV_A
    r4 = lax.broadcasted_iota(jnp.int32, (4 * c, hk), 0)
    c4 = lax.broadcasted_iota(jnp.int32, (4 * c, hk), 1)
    head_mask_k = (r4 // c) == (c4 // DK_A)
    r5 = lax.broadcasted_iota(jnp.int32, (4 * c, hv), 0)
    c5 = lax.broadcasted_iota(jnp.int32, (4 * c, hv), 1)
    head_mask_v = (r5 // c) == (c5 // DV_A)
    rs = lax.broadcasted_iota(jnp.int32, (hk, hv), 0)
    cs = lax.broadcasted_iota(jnp.int32, (hk, hv), 1)
    head_mask_s = (rs // DK_A) == (cs // DV_A)
    rt = lax.broadcasted_iota(jnp.int32, (c, 4 * c), 0)
    ct = lax.broadcasted_iota(jnp.int32, (c, 4 * c), 1)
    causal = (ct % c) <= rt
    pad_k = jnp.zeros((LANES - c, hk), F32)
    pad_v = jnp.zeros((LANES - c, hv), BF16)

    outs = []
    for n in range(n_chunks):
        sl = slice(n * c, (n + 1) * c)
        qn, kn, vn, bn = q[sl], k[sl], v[sl], bcum[sl]
        b_ref = bn[mid:mid + 1]
        b_last = bn[last:last + 1]
        q_rel = (qn * jnp.exp(bn - b_ref)).astype(BF16)
        k_rel = kn * jnp.exp(b_ref - bn)
        k_dec = kn * jnp.exp(b_last - bn)
        q_dec = (qn * jnp.exp(bn)).astype(BF16)
        k_bd = jnp.where(head_mask_k, jnp.concatenate([k_rel] * 4, axis=0), 0.0).astype(BF16)
        scores = _nt_dot(q_rel, k_bd)
        scores = jnp.where(causal, scores, 0.0).astype(BF16)
        v_bd = jnp.where(head_mask_v, jnp.concatenate([vn] * 4, axis=0), 0.0).astype(BF16)
        o = jnp.dot(scores, v_bd, preferred_element_type=F32)
        s_old = s_ref[...]
        o = o + jnp.dot(q_dec, s_old.astype(BF16), preferred_element_type=F32)
        k_dec_t = jnp.concatenate([k_dec, pad_k], axis=0).T.astype(BF16)
        v_pad = jnp.concatenate([vn.astype(BF16), pad_v], axis=0)
        kv = jnp.dot(k_dec_t, v_pad, preferred_element_type=F32)
        decay_t = jnp.broadcast_to(jnp.exp(b_last), (LANES, hk)).T
        decay_full = jnp.concatenate([decay_t] * (hv // LANES), axis=1)
        s_ref[...] = s_old * decay_full + jnp.where(head_mask_s, kv, 0.0)
        outs.append(o)
    return outs[0] if n_chunks == 1 else jnp.concatenate(outs, axis=0)


def _linattn_kernel(qa_ref, ka_ref, va_ref, ra_ref, qb_ref, kb_ref, vb_ref, gb_ref, lo_ref,
                    wgu_ref, bg_ref, gg_ref, gr_ref, cos_ref, sin_ref, lgam_ref, *rest,
                    rows, has_state):
    if has_state:
        s0a_ref, s0b_ref, mix_ref, sa_out, sb_out, sa, sb = rest
    else:
        mix_ref, sa_out, sb_out, sa, sb = rest
    t = pl.program_id(1)
    c = CHUNK
    padded = max(rows, c)
    n_chunks = padded // c
    valid = min(rows, c)
    mid, last = (valid - 1) // 2, valid - 1

    @pl.when(t == 0)
    def _init():
        sa[...] = jnp.zeros_like(sa)
        sb[...] = jnp.zeros_like(sb)
        if has_state:
            for h in range(4):
                sa[h * DK_A:(h + 1) * DK_A, h * DV_A:(h + 1) * DV_A] = s0a_ref[0, h]
                sb[h * DK_B:(h + 1) * DK_B, h * DV_B:(h + 1) * DV_B] = s0b_ref[0, h]

    def pad_rows(x):
        if padded == rows:
            return x
        return jnp.concatenate([x, jnp.zeros((padded - rows, x.shape[1]), x.dtype)], axis=0)

    row = lax.broadcasted_iota(jnp.int32, (padded, padded), 0)
    col = lax.broadcasted_iota(jnp.int32, (padded, padded), 1)
    tri = jnp.where(((row // c) == (col // c)) & (col <= row), 1.0, 0.0).astype(BF16)

    gate = jnp.dot(lo_ref[...], wgu_ref[...], preferred_element_type=F32) + bg_ref[...]
    log_a = (jnp.minimum(gate, 0.0) - jnp.log1p(jnp.exp(-jnp.abs(gate)))) / GATE_TAU
    log_a = pad_rows(log_a)
    hi = log_a.astype(BF16)
    lo = (log_a - hi.astype(F32)).astype(BF16)
    bcum_a = (jnp.dot(tri, hi, preferred_element_type=F32)
              + jnp.dot(tri, lo, preferred_element_type=F32))
    qa = pad_rows(qa_ref[...].astype(F32) * DK_A ** -0.5)
    ka = pad_rows(ka_ref[...].astype(F32))
    va = pad_rows(va_ref[...].astype(F32))
    o_a = _decay_attention_tile(qa, ka, va, bcum_a, sa, n_chunks, mid, last)[:rows]

    cos, sin = cos_ref[...], sin_ref[...]
    qb = pad_rows(_rope(qb_ref[...].astype(F32), cos, sin))
    kb = pad_rows(_rope(kb_ref[...].astype(F32), cos, sin) * DK_B ** -0.5)
    vb = pad_rows(vb_ref[...].astype(F32))
    pos = lax.broadcasted_iota(jnp.int32, (padded, 4 * DK_B), 0) % c
    steps = jnp.where(pos < valid, pos + 1, valid).astype(F32)
    bcum_b = steps * lgam_ref[...]
    o_b = _decay_attention_tile(qb, kb, vb, bcum_b, sb, n_chunks, mid, last)[:rows]

    gg, gr = gg_ref[...], gr_ref[...]
    for h in range(4):
        sl = slice(h * DV_A, (h + 1) * DV_A)
        oh = o_a[:, sl]
        oh = oh * lax.rsqrt(jnp.mean(oh * oh, axis=-1, keepdims=True) + EPS) * gg[:, sl]
        r = ra_ref[:, sl].astype(F32)
        mix_ref[:, sl] = (oh * (r * jax.nn.sigmoid(r))).astype(mix_ref.dtype)
        ob = o_b[:, sl]
        ob = ob - jnp.mean(ob, axis=-1, keepdims=True)
        ob = ob * lax.rsqrt(jnp.mean(ob * ob, axis=-1, keepdims=True) + EPS) * gr[:, sl]
        gb = gb_ref[:, sl].astype(F32)
        mix_ref[:, 4 * DV_A + h * DV_B:4 * DV_A + (h + 1) * DV_B] = (
            ob * (gb * jax.nn.sigmoid(gb))).astype(mix_ref.dtype)

    @pl.when(t == pl.num_programs(1) - 1)
    def _emit_state():
        for h in range(4):
            sa_out[0, h] = sa[h * DK_A:(h + 1) * DK_A, h * DV_A:(h + 1) * DV_A]
            sb_out[0, h] = sb[h * DK_B:(h + 1) * DK_B, h * DV_B:(h + 1) * DV_B]


def _linattn(slab, batch, seq, rows, w_gu, b_gate, g_gla, g_ret, cos, sin, lgam, s0a, s0b):
    nt = seq // rows
    has_state = s0a is not None

    def col(width, idx):
        return pl.BlockSpec((rows, width), lambda b, t, idx=idx: (b * nt + t, idx))

    def const(shape):
        return pl.BlockSpec(shape, lambda b, t: (0,) * len(shape))

    hk = 4 * DK_A
    hv = 4 * DV_A
    in_specs = [col(hk, 0), col(hk, 1), col(hv, 1), col(hv, 2),
                col(hk, 6), col(hk, 7), col(hv, 4), col(hv, 5),
                col(LANES, 24),
                const((LANES, hk)), const((1, hk)), const((1, hv)), const((1, hv)),
                pl.BlockSpec((rows, hk), lambda b, t: (t, 0)),
                pl.BlockSpec((rows, hk), lambda b, t: (t, 0)),
                const((1, hk))]
    args = [slab] * 9 + [w_gu, b_gate, g_gla, g_ret, cos, sin, lgam]
    state_spec = pl.BlockSpec((1, 4, DK_A, DV_A), lambda b, t: (b, 0, 0, 0))
    if has_state:
        in_specs += [state_spec, state_spec]
        args += [s0a, s0b]
    state_shape = jax.ShapeDtypeStruct((batch, 4, DK_A, DV_A), F32)
    return pl.pallas_call(
        functools.partial(_linattn_kernel, rows=rows, has_state=has_state),
        grid=(batch, nt),
        in_specs=in_specs,
        out_specs=[pl.BlockSpec((rows, 2 * hv), lambda b, t: (b * nt + t, 0)), state_spec, state_spec],
        out_shape=[jax.ShapeDtypeStruct((batch * seq, 2 * hv), BF16), state_shape, state_shape],
        scratch_shapes=[pltpu.VMEM((hk, hv), F32), pltpu.VMEM((hk, hv), F32)],
        compiler_params=_params(2),
        name="decay_linear_attention",
    )(*args)


def _ffn_kernel(x_ref, mix_ref, wo_ref, g_ref, wau_ref, wdw_ref, bdw_ref, wout_ref, st_ref,
                gf_ref, y_ref, cv_ref, h_s, acc_s, e_s, carry_s,
                *, tm, tf, sub, prev_rows, shift, tiles_per_seq, final_norm):
    i = pl.program_id(0)
    j = pl.program_id(1)
    p = prev_rows
    subs = [slice(r, r + sub) for r in range(0, tm, sub)]

    @pl.when(j == 0)
    def _start():
        for rs in subs:
            xm = x_ref[rs] + jnp.dot(mix_ref[rs], wo_ref[...], preferred_element_type=F32)
            acc_s[rs] = xm
            h_s[rs] = _rms(xm, g_ref[...]).astype(BF16)

    at_start = (i % tiles_per_seq) == 0

    @pl.when(at_start)
    def _from_state():
        e_s[0:p] = st_ref[0]

    @pl.when(jnp.logical_not(at_start))
    def _from_carry():
        e_s[0:p] = carry_s[j]

    wdw = wdw_ref[...]
    bdw = bdw_ref[...]
    for rs in subs:
        au = jnp.dot(h_s[rs], wau_ref[...], preferred_element_type=F32)
        a, u = au[:, :tf], au[:, tf:]
        r0 = rs.start
        e_s[p + r0:p + r0 + sub] = a
        conv = bdw + e_s[p - 2 * shift + r0:p - 2 * shift + r0 + sub] * wdw[0:1]
        conv = conv + e_s[p - shift + r0:p - shift + r0 + sub] * wdw[1:2]
        conv = conv + a * wdw[2:3]
        act = 0.5 * conv * (1.0 + lax.erf(conv * np.float32(np.sqrt(0.5)))) * u
        acc_s[rs] += jnp.dot(act.astype(BF16), wout_ref[...], preferred_element_type=F32)
    carry_s[j] = e_s[tm:tm + p]
    cv_ref[0] = e_s[p + tm - 2 * shift:p + tm]

    @pl.when(j == pl.num_programs(1) - 1)
    def _finish():
        for rs in subs:
            out = acc_s[rs]
            if final_norm:
                out = _rms(out, gf_ref[...])
            y_ref[rs] = out


def _ffn(x, mix, w_o, g_ffn, w_in, w_dw, b_dw, w_out, conv_state, g_final, *, tm, tf, shift,
         tiles_per_seq, final_norm):
    m, d = x.shape
    n_f = D_FF // tf
    n_seq, prev_rows, _ = conv_state.shape
    kern = functools.partial(_ffn_kernel, tm=tm, tf=tf, sub=min(tm, SUB_FFN), prev_rows=prev_rows,
                             shift=shift, tiles_per_seq=tiles_per_seq, final_norm=final_norm)
    return pl.pallas_call(
        kern,
        grid=(m // tm, n_f),
        in_specs=[pl.BlockSpec((tm, d), lambda i, j: (i, 0)),
                  pl.BlockSpec((tm, d), lambda i, j: (i, 0)),
                  pl.BlockSpec((d, d), lambda i, j: (0, 0)),
                  pl.BlockSpec((1, d), lambda i, j: (0, 0)),
                  pl.BlockSpec((d, 2 * tf), lambda i, j: (0, j)),
                  pl.BlockSpec((CONV_W, tf), lambda i, j: (0, j)),
                  pl.BlockSpec((1, tf), lambda i, j: (0, j)),
                  pl.BlockSpec((tf, d), lambda i, j: (j, 0)),
                  pl.BlockSpec((1, prev_rows, tf), lambda i, j: (i // tiles_per_seq, 0, j)),
                  pl.BlockSpec((1, d), lambda i, j: (0, 0))],
        out_specs=[pl.BlockSpec((tm, d), lambda i, j: (i, 0)),
                   pl.BlockSpec((1, 2 * shift, tf), lambda i, j: (i, 0, j))],
        out_shape=[jax.ShapeDtypeStruct((m, d), F32),
                   jax.ShapeDtypeStruct((m // tm, 2 * shift, D_FF), F32)],
        scratch_shapes=[pltpu.VMEM((tm, d), BF16), pltpu.VMEM((tm, d), F32),
                        pltpu.VMEM((prev_rows + tm, tf), F32),
                        pltpu.VMEM((n_f, prev_rows, tf), F32)],
        compiler_params=_params(2),
        name="outproj_convffn",
    )(x, mix, w_o, g_ffn, w_in, w_dw, b_dw, w_out, conv_state, g_final)


def _mla_latents(x_ref, g_ref, win_ref, gq_ref, gkv_ref, cos_ref, sin_ref, ckv_ref, kr_ref):
    h = _rms(x_ref[...], g_ref[...]).astype(BF16)
    p = jnp.dot(h, win_ref[...], preferred_element_type=F32)
    cq_n = _rms(p[:, 0:Q_LORA], gq_ref[...]).astype(BF16)
    ckv_n = _rms(p[:, Q_LORA:Q_LORA + KV_LORA], gkv_ref[...])
    ckv_ref[...] = ckv_n
    kr = _rope(p[:, Q_LORA + KV_LORA:], cos_ref[...], sin_ref[...])
    kr_ref[...] = kr[:, 0:ROPE]
    return cq_n, ckv_n, kr


def _mla_proj_prompt_kernel(x_ref, g_ref, win_ref, gq_ref, gkv_ref, wuq_ref, wukv_ref, cos_ref,
                            sin_ref, q_ref, k_ref, v_ref, ckv_ref, kr_ref, *, scale):
    cq_n, ckv_n, kr = _mla_latents(x_ref, g_ref, win_ref, gq_ref, gkv_ref, cos_ref, sin_ref,
                                   ckv_ref, kr_ref)
    cos, sin = cos_ref[...], sin_ref[...]
    kr_b = kr.astype(BF16)
    ckv_b = ckv_n.astype(BF16)
    for hh in range(H_C):
        qh = jnp.dot(cq_n, wuq_ref[:, hh * HEAD_W:(hh + 1) * HEAD_W],
                     preferred_element_type=F32) * scale
        q_ref[hh, :, 0:NOPE] = qh[:, 0:NOPE].astype(BF16)
        q_ref[hh, :, NOPE:HEAD_W] = _rope(qh[:, NOPE:HEAD_W], cos, sin).astype(BF16)
        kvh = jnp.dot(ckv_b, wukv_ref[:, hh * HEAD_W:(hh + 1) * HEAD_W],
                      preferred_element_type=F32)
        k_ref[hh, :, 0:NOPE] = kvh[:, 0:NOPE].astype(BF16)
        k_ref[hh, :, NOPE:HEAD_W] = kr_b
        v_ref[hh] = kvh[:, NOPE:HEAD_W].astype(BF16)


def _mla_proj_prompt(x, g, w_in, g_q, g_kv, w_uq, w_ukv, cos, sin, seq, tm, scale):
    m, d = x.shape
    nt = seq // tm
    const2 = lambda shape: pl.BlockSpec(shape, lambda i: (0, 0))
    return pl.pallas_call(
        functools.partial(_mla_proj_prompt_kernel, scale=scale),
        grid=(m // tm,),
        in_specs=[pl.BlockSpec((tm, d), lambda i: (i, 0)), const2((1, d)), const2(w_in.shape),
                  const2((1, Q_LORA)), const2((1, KV_LORA)), const2(w_uq.shape), const2(w_ukv.shape),
                  pl.BlockSpec((tm, LANES), lambda i: (i % nt, 0)),
                  pl.BlockSpec((tm, LANES), lambda i: (i % nt, 0))],
        out_specs=[pl.BlockSpec((H_C, tm, HEAD_W), lambda i: (0, i, 0)),
                   pl.BlockSpec((H_C, tm, HEAD_W), lambda i: (0, i, 0)),
                   pl.BlockSpec((H_C, tm, V_C), lambda i: (0, i, 0)),
                   pl.BlockSpec((tm, KV_LORA), lambda i: (i, 0)),
                   pl.BlockSpec((tm, ROPE), lambda i: (i, 0))],
        out_shape=[jax.ShapeDtypeStruct((H_C, m, HEAD_W), BF16),
                   jax.ShapeDtypeStruct((H_C, m, HEAD_W), BF16),
                   jax.ShapeDtypeStruct((H_C, m, V_C), BF16),
                   jax.ShapeDtypeStruct((m, KV_LORA), F32),
                   jax.ShapeDtypeStruct((m, ROPE), F32)],
        compiler_params=_params(1),
        name="mla_proj_prompt",
    )(x, g, w_in, g_q, g_kv, w_uq, w_ukv, cos, sin)


def _mla_proj_sample_kernel(x_ref, g_ref, win_ref, gq_ref, gkv_ref, wuq_ref, wukt_ref, cos_ref,
                            sin_ref, qlat_ref, qr_ref, ckv_ref, kr_ref, *, scale):
    cq_n, _, _ = _mla_latents(x_ref, g_ref, win_ref, gq_ref, gkv_ref, cos_ref, sin_ref,
                              ckv_ref, kr_ref)
    cos, sin = cos_ref[...], sin_ref[...]
    for hh in range(H_C):
        qh = jnp.dot(cq_n, wuq_ref[:, hh * HEAD_W:(hh + 1) * HEAD_W],
                     preferred_element_type=F32)
        q_lat = jnp.dot(qh[:, 0:NOPE].astype(BF16), wukt_ref[hh], preferred_element_type=F32)
        qlat_ref[hh] = (q_lat * scale).astype(BF16)
        qr_ref[hh] = (_rope(qh[:, NOPE:HEAD_W], cos, sin) * scale)[:, 0:ROPE].astype(BF16)


def _mla_proj_sample(x, g, w_in, g_q, g_kv, w_uq, w_ukt, cos, sin, scale):
    m, d = x.shape
    const = lambda shape: pl.BlockSpec(shape, lambda i: (0,) * len(shape))
    return pl.pallas_call(
        functools.partial(_mla_proj_sample_kernel, scale=scale),
        grid=(1,),
        in_specs=[const((m, d)), const((1, d)), const(w_in.shape), const((1, Q_LORA)),
                  const((1, KV_LORA)), const(w_uq.shape), const(w_ukt.shape),
                  const((m, LANES)), const((m, LANES))],
        out_specs=[const((H_C, m, KV_LORA)), const((H_C, m, ROPE)), const((m, KV_LORA)),
                   const((m, ROPE))],
        out_shape=[jax.ShapeDtypeStruct((H_C, m, KV_LORA), BF16),
                   jax.ShapeDtypeStruct((H_C, m, ROPE), BF16),
                   jax.ShapeDtypeStruct((m, KV_LORA), F32),
                   jax.ShapeDtypeStruct((m, ROPE), F32)],
        compiler_params=_params(1),
        name="mla_proj_sample",
    )(x, g, w_in, g_q, g_kv, w_uq, w_ukt, cos, sin)


def _flash_kernel(q_ref, k_ref, v_ref, o_ref, m_s, acc_s, *, tile, n_tiles):
    t = tile
    row = lax.broadcasted_iota(jnp.int32, (t, t), 0)
    col = lax.broadcasted_iota(jnp.int32, (t, t), 1)
    visible = (col // CHUNK) <= (row // CHUNK)
    ones = jnp.ones((t, V_C), BF16)
    for j in range(n_tiles):
        ks = slice(j * t, (j + 1) * t)
        kj = k_ref[0, ks, :]
        vj = jnp.concatenate([v_ref[0, ks, :], ones], axis=1)
        for i in range(j, n_tiles):
            qs = slice(i * t, (i + 1) * t)
            s = _nt_dot(q_ref[0, qs, :], kj)
            if i == j:
                s = jnp.where(visible, s, -jnp.inf)
            m_new = jnp.broadcast_to(jnp.max(s, axis=-1, keepdims=True), (t, LANES))
            if j > 0:
                m_prev = m_s[qs]
                m_new = jnp.maximum(m_prev, m_new)
            p = jnp.exp2(s - jnp.concatenate([m_new] * (t // LANES), axis=1))
            pv = jnp.dot(p.astype(BF16), vj, preferred_element_type=F32)
            if j > 0:
                alpha = jnp.exp2(m_prev - m_new)
                pv = pv + jnp.concatenate([alpha, alpha], axis=1) * acc_s[qs]
            if i == j:
                o_ref[qs, :] = (pv[:, :V_C] / pv[:, V_C:]).astype(o_ref.dtype)
            else:
                m_s[qs] = m_new
                acc_s[qs] = pv


def _flash_prompt(q, k, v, batch, seq, tile):
    m = batch * seq
    return pl.pallas_call(
        functools.partial(_flash_kernel, tile=tile, n_tiles=seq // tile),
        grid=(batch, H_C),
        in_specs=[pl.BlockSpec((1, seq, HEAD_W), lambda b, h: (h, b, 0)),
                  pl.BlockSpec((1, seq, HEAD_W), lambda b, h: (h, b, 0)),
                  pl.BlockSpec((1, seq, V_C), lambda b, h: (h, b, 0))],
        out_specs=pl.BlockSpec((seq, V_C), lambda b, h: (b, h)),
        out_shape=jax.ShapeDtypeStruct((m, H_C * V_C), BF16),
        scratch_shapes=[pltpu.VMEM((seq, LANES), F32), pltpu.VMEM((seq, 2 * V_C), F32)],
        compiler_params=_params(2),
        name="flash_prompt",
    )(q, k, v)


def _latent_attn_kernel(qlat_ref, qr_ref, cckv_ref, ckr_ref, nckv_ref, nkr_ref, wuv_ref, o_ref,
                        m_s, l_s, acc_s, *, steps, new_rows):
    j = pl.program_id(1)
    rows = H_C * steps
    q_lat = qlat_ref[...].reshape(rows, KV_LORA)
    q_r = qr_ref[...].reshape(rows, ROPE)

    @pl.when(j == 0)
    def _init():
        m_s[...] = jnp.full_like(m_s, -jnp.inf)
        l_s[...] = jnp.zeros_like(l_s)
        acc_s[...] = jnp.zeros_like(acc_s)

    def update(ckv, kr, valid):
        ckv_b = ckv.astype(BF16)
        s = _nt_dot(q_lat, ckv_b) + _nt_dot(q_r, kr.astype(BF16))
        if valid is not None:
            col = lax.broadcasted_iota(jnp.int32, s.shape, 1)
            s = jnp.where(col < valid, s, -jnp.inf)
        m_prev = m_s[...]
        m_new = jnp.maximum(m_prev, jnp.max(s, axis=-1, keepdims=True))
        p = jnp.exp(s - m_new)
        alpha = jnp.exp(m_prev - m_new)
        l_s[...] = alpha * l_s[...] + jnp.sum(p, axis=-1, keepdims=True)
        acc_s[...] = alpha * acc_s[...] + jnp.dot(p.astype(BF16), ckv_b,
                                                  preferred_element_type=F32)
        m_s[...] = m_new

    update(cckv_ref[0], ckr_ref[0], None)

    @pl.when(j == pl.num_programs(1) - 1)
    def _finish():
        update(nckv_ref[0], nkr_ref[0], new_rows)
        o_lat = (acc_s[...] / l_s[...]).astype(BF16)
        for hh in range(H_C):
            o_ref[:, hh * V_C:(hh + 1) * V_C] = jnp.dot(
                o_lat[hh * steps:(hh + 1) * steps], wuv_ref[hh],
                preferred_element_type=F32).astype(o_ref.dtype)


def _latent_attn_sample(q_lat, q_r, cache_ckv, cache_kr, new_ckv, new_kr, w_uv, steps, tk):
    batch, past, _ = cache_ckv.shape
    pad_rows = new_ckv.shape[1]
    return pl.pallas_call(
        functools.partial(_latent_attn_kernel, steps=steps, new_rows=steps),
        grid=(batch, past // tk),
        in_specs=[pl.BlockSpec((H_C, steps, KV_LORA), lambda b, j: (0, b, 0)),
                  pl.BlockSpec((H_C, steps, ROPE), lambda b, j: (0, b, 0)),
                  pl.BlockSpec((1, tk, KV_LORA), lambda b, j: (b, j, 0)),
                  pl.BlockSpec((1, tk, ROPE), lambda b, j: (b, j, 0)),
                  pl.BlockSpec((1, pad_rows, KV_LORA), lambda b, j: (b, 0, 0)),
                  pl.BlockSpec((1, pad_rows, ROPE), lambda b, j: (b, 0, 0)),
                  pl.BlockSpec((H_C, KV_LORA, V_C), lambda b, j: (0, 0, 0))],
        out_specs=pl.BlockSpec((steps, H_C * V_C), lambda b, j: (b, 0)),
        out_shape=jax.ShapeDtypeStruct((batch * steps, H_C * V_C), BF16),
        scratch_shapes=[pltpu.VMEM((H_C * steps, 1), F32), pltpu.VMEM((H_C * steps, 1), F32),
                        pltpu.VMEM((H_C * steps, KV_LORA), F32)],
        compiler_params=_params(2),
        name="latent_attn_sample",
    )(q_lat, q_r, cache_ckv, cache_kr, new_ckv, new_kr, w_uv)


def _rope_tables(pos, width):
    half = ROPE // 2
    freqs = ROPE_THETA ** (-jnp.arange(half, dtype=F32) / half)
    ang = pos.astype(F32)[:, None] * freqs[None, :]
    cos, sin = jnp.cos(ang), jnp.sin(ang)
    cos_h = jnp.concatenate([cos, cos], axis=-1)
    sin_h = jnp.concatenate([-sin, sin], axis=-1)
    reps = width // ROPE
    return jnp.tile(cos_h, (1, reps)), jnp.tile(sin_h, (1, reps))


def _prep_even(w_in_ab, w_gate_up):
    sizes = [H_A * DK_A, H_A * DK_A, H_A * DV_A, H_A * DV_A, GATE_RANK,
             H_B * DK_B, H_B * DK_B, H_B * DV_B, H_B * DV_B]
    offs = np.concatenate([[0], np.cumsum(sizes)])
    parts = [w_in_ab[:, offs[n]:offs[n + 1]] for n in range(9)]
    lo = jnp.pad(parts[4], ((0, 0), (0, LANES - GATE_RANK)))
    w = jnp.concatenate(parts[0:4] + parts[5:9] + [lo], axis=1).astype(BF16)
    w_gu = jnp.pad(w_gate_up, ((0, LANES - GATE_RANK), (0, 0))).astype(BF16)
    return w, w_gu


def _prep_odd(w_in_c, w_uq, w_uk, w_uv):
    d = w_in_c.shape[0]
    w_in = jnp.pad(w_in_c, ((0, 0), (0, LANES - ROPE))).astype(BF16)
    uq = w_uq.reshape(Q_LORA, H_C, NOPE + ROPE)
    uq = jnp.pad(uq, ((0, 0), (0, 0), (0, HEAD_W - NOPE - ROPE)))
    w_uq_p = uq.reshape(Q_LORA, H_C * HEAD_W).astype(BF16)
    w_ukv = jnp.concatenate([w_uk, w_uv], axis=2).reshape(KV_LORA, H_C * HEAD_W).astype(BF16)
    w_ukt = jnp.transpose(w_uk, (1, 2, 0)).astype(BF16)
    w_uvh = jnp.transpose(w_uv, (1, 0, 2)).astype(BF16)
    del d
    return w_in, w_uq_p, w_ukv, w_ukt, w_uvh


def _log_gamma_row():
    lg = np.log1p(-np.exp2(-5.0 - np.arange(H_B, dtype=np.float32))).astype(np.float32)
    return jnp.asarray(np.repeat(lg, DK_B)[None, :])


def _to_time_major(a, batch, steps):
    return a.reshape(batch, steps, -1).transpose(1, 0, 2).reshape(batch * steps, -1)


def _to_batch_major(a, batch, steps):
    return a.reshape(steps, batch, -1).transpose(1, 0, 2).reshape(batch * steps, -1)


def _divisor_tile(n, pref):
    t = min(n, pref)
    while n % t:
        t //= 2
    return t


def _trunk(x, pos0, gla0, ret0, ckv_past, kr_past, conv0, w, time_major_ffn):
    batch, seq, d = x.shape
    m = batch * seq
    pos = pos0 + jnp.arange(seq, dtype=jnp.int32)
    row = lambda v: v.reshape(1, -1)
    scale = float((NOPE + ROPE) ** -0.5)
    tf = TF_FFN
    tm_proj = _divisor_tile(m, TM_PROJ)

    if time_major_ffn:
        ffn_kw = dict(tm=m, tf=tf, shift=batch, tiles_per_seq=1)
        to_ffn = lambda a: _to_time_major(a, batch, seq)
        from_ffn = lambda a: _to_batch_major(a, batch, seq)
        prep_state = lambda s: s.transpose(1, 0, 2).reshape(1, 2 * batch, D_FF)
        post_state = lambda s: s.reshape(2, batch, D_FF).transpose(1, 0, 2)
    else:
        tm_ffn = _divisor_tile(seq, TM_FFN)
        ffn_kw = dict(tm=tm_ffn, tf=tf, shift=1, tiles_per_seq=seq // tm_ffn)
        to_ffn = from_ffn = lambda a: a
        prep_state = lambda s: jnp.pad(s, ((0, 0), (SUBLANES - 2, 0), (0, 0)))
        post_state = lambda s: s

    xf = x.reshape(m, d)
    conv_new = []
    gla_new = ret_new = ckv_new = kr_new = None
    for layer in range(2):
        g_mix = row(w['norm_mix'][layer])
        if layer == 0:
            w_ab, w_gu = _prep_even(w['w_in_ab'][0], w['w_gate_up'][0])
            slab = _norm_matmul(xf, g_mix, w_ab, tm_proj, 640)
            cos, sin = _rope_tables(pos, 4 * DK_B)
            rows = _divisor_tile(seq, ROWS_LINATTN)
            mix, gla_new, ret_new = _linattn(
                slab, batch, seq, rows, w_gu, row(w['b_gate'][0]), row(w['g_gla'][0]),
                row(w['g_ret'][0]), cos, sin, _log_gamma_row(),
                None if gla0 is None else gla0[0], None if ret0 is None else ret0[0])
            w_o = w['w_out_ab'][0].astype(BF16)
        else:
            w_in, w_uq_p, w_ukv, w_ukt, w_uvh = _prep_odd(w['w_in_c'][0], w['w_uq'][0], w['w_uk'][0],
                                                          w['w_uv'][0])
            cos, sin = _rope_tables(pos, LANES)
            g_q, g_kv = row(w['g_q'][0]), row(w['g_kv'][0])
            if ckv_past is None:
                q, k, v, ckv_new, kr_new = _mla_proj_prompt(
                    xf, g_mix, w_in, g_q, g_kv, w_uq_p, w_ukv, cos, sin, seq, tm_proj,
                    scale * float(np.log2(np.e)))
                mix = _flash_prompt(q, k, v, batch, seq, _divisor_tile(seq, TQ_FLASH))
            else:
                assert ckv_past.shape[2] % CHUNK == 0 and seq <= CHUNK
                cos_m, sin_m = jnp.tile(cos, (batch, 1)), jnp.tile(sin, (batch, 1))
                q_lat, q_r, ckv_new, kr_new = _mla_proj_sample(
                    xf, g_mix, w_in, g_q, g_kv, w_uq_p, w_ukt, cos_m, sin_m, scale)
                pad = ((0, 0), (0, LANES - seq), (0, 0))
                new_ckv = jnp.pad(ckv_new.reshape(batch, seq, KV_LORA), pad)
                new_kr = jnp.pad(kr_new.reshape(batch, seq, ROPE), pad)
                mix = _latent_attn_sample(q_lat, q_r, ckv_past[0], kr_past[0], new_ckv, new_kr,
                                          w_uvh, seq, _divisor_tile(ckv_past.shape[2], TK_LATENT))
            w_o = w['w_out_c'][0].astype(BF16)

        if conv0 is None:
            state = jnp.zeros((batch, SUBLANES, D_FF), F32) if not time_major_ffn else \
                jnp.zeros((1, 2 * batch, D_FF), F32)
        else:
            state = prep_state(conv0[layer])
        w_au = w['w_ffn_in'][layer].astype(BF16).reshape(d, 2, D_FF // tf, tf)
        w_au = w_au.transpose(0, 2, 1, 3).reshape(d, 2 * D_FF)
        y, conv_rows = _ffn(to_ffn(xf), to_ffn(mix), w_o, row(w['norm_ffn'][layer]),
                            w_au, w['w_dwconv'][layer],
                            row(w['b_dwconv'][layer]), w['w_ffn_out'][layer].astype(BF16), state,
                            row(w['norm_final']), final_norm=(layer == 1), **ffn_kw)
        xf = from_ffn(y)
        tps = ffn_kw['tiles_per_seq']
        conv_new.append(post_state(conv_rows[tps - 1::tps]))

    return (xf.reshape(batch, seq, d), gla_new[None], ret_new[None],
            ckv_new.reshape(1, batch, seq, KV_LORA), kr_new.reshape(1, batch, seq, ROPE),
            jnp.stack(conv_new))


def kernel(x_prompt, x_sample, state_gla, state_ret, cache_ckv, cache_krope, state_conv, norm_mix, norm_ffn, norm_final, w_in_ab, w_gate_up, b_gate, g_gla, g_ret, w_out_ab, w_in_c, g_q, g_kv, w_uq, w_uk, w_uv, w_out_c, w_ffn_in, w_dwconv, b_dwconv, w_ffn_out):
    w = {'norm_mix': norm_mix, 'norm_ffn': norm_ffn, 'norm_final': norm_final,
         'w_in_ab': w_in_ab, 'w_gate_up': w_gate_up, 'b_gate': b_gate, 'g_gla': g_gla,
         'g_ret': g_ret, 'w_out_ab': w_out_ab, 'w_in_c': w_in_c, 'g_q': g_q, 'g_kv': g_kv,
         'w_uq': w_uq, 'w_uk': w_uk, 'w_uv': w_uv, 'w_out_c': w_out_c, 'w_ffn_in': w_ffn_in,
         'w_dwconv': w_dwconv, 'b_dwconv': b_dwconv, 'w_ffn_out': w_ffn_out}
    past_len = cache_ckv.shape[2]
    y_p, gla_p, ret_p, ckv_p, kr_p, conv_p = _trunk(
        x_prompt, 0, None, None, None, None, None, w, time_major_ffn=False)
    y_s, gla_s, ret_s, ckv_s, kr_s, conv_s = _trunk(
        x_sample, past_len, state_gla, state_ret, cache_ckv, cache_krope, state_conv, w,
        time_major_ffn=True)
    return (y_p, y_s, gla_p, gla_s, ret_p, ret_s, ckv_p, ckv_s, kr_p, kr_s, conv_p, conv_s)
```

```python
import functools

import numpy as np
import jax
import jax.numpy as jnp
from jax import lax
from jax.experimental import pallas as pl
from jax.experimental.pallas import tpu as pltpu

F32 = jnp.float32
BF16 = jnp.bfloat16

D_MODEL = 1024
CHUNK = 64
EPS = 1e-6
ROPE_THETA = 10000.0
H_A, DK_A, DV_A = 4, 64, 128
GATE_RANK = 16
GATE_TAU = 16.0
H_B, DK_B, DV_B = 4, 64, 128
H_C = 8
Q_LORA, KV_LORA, NOPE, ROPE, V_C = 384, 512, 128, 64, 128
D_FF = 2816
CONV_W = 3

LANES = 128
SUBLANES = 8
VMEM_LIMIT = 56 * 1024 * 1024
AB_COLS = 3200
HEAD_W = 2 * LANES
TM_PROJ = 512
TM_FFN = 1024
TF_FFN = D_FF // 2
SUB_FFN = 256
ROWS_LINATTN = 256
TQ_FLASH = 256
TK_LATENT = 1024


def _params(n_axes):
    return pltpu.CompilerParams(dimension_semantics=("arbitrary",) * n_axes,
                                vmem_limit_bytes=VMEM_LIMIT)


def _rms(x, g):
    return x * lax.rsqrt(jnp.mean(x * x, axis=-1, keepdims=True) + EPS) * g


def _rope_first_half(shape):
    lane = lax.broadcasted_iota(jnp.int32, shape, 1)
    return (lane % ROPE) < ROPE // 2


def _rope(x, cos, sin_signed, first=None):
    w = x.shape[1]
    half = ROPE // 2
    if first is None:
        first = _rope_first_half(x.shape)
    swapped = jnp.where(first, pltpu.roll(x, w - half, 1), pltpu.roll(x, half, 1))
    return x * cos + swapped * sin_signed


def _nt_dot(a, b):
    return lax.dot_general(a, b, (((1,), (1,)), ((), ())), preferred_element_type=F32)


def _norm_matmul_kernel(x_ref, g_ref, w_ref, o_ref, *, col_chunk):
    h = _rms(x_ref[...], g_ref[...]).astype(BF16)
    n = w_ref.shape[1]
    for c0 in range(0, n, col_chunk):
        c1 = min(c0 + col_chunk, n)
        o_ref[:, c0:c1] = jnp.dot(h, w_ref[:, c0:c1], preferred_element_type=F32).astype(o_ref.dtype)


def _norm_matmul(x, g, w, tm, col_chunk):
    m, d = x.shape
    n = w.shape[1]
    return pl.pallas_call(
        functools.partial(_norm_matmul_kernel, col_chunk=col_chunk),
        grid=(m // tm,),
        in_specs=[pl.BlockSpec((tm, d), lambda i: (i, 0)),
                  pl.BlockSpec((1, d), lambda i: (0, 0)),
                  pl.BlockSpec((d, n), lambda i: (0, 0))],
        out_specs=pl.BlockSpec((tm, n), lambda i: (i, 0)),
        out_shape=jax.ShapeDtypeStruct((m, n), BF16),
        compiler_params=_params(1),
        name="norm_inproj_ab",
    )(x, g, w)


def _decay_attention_tile(q, k, v, bcum, s_ref, mask_k, mask_v, n_chunks, mid, last):
    c = CHUNK
    assert 2 * c == LANES
    hk, hv = 4 * DK_A, 4 * DV_A
    rt = lax.broadcasted_iota(jnp.int32, (c, 4 * c), 0)
    ct = lax.broadcasted_iota(jnp.int32, (c, 4 * c), 1)
    causal = (ct % c) <= rt
    lane = lax.broadcasted_iota(jnp.int32, (DK_A, LANES), 1)
    zero_v = jnp.zeros((LANES - c, hv), BF16)
    zero_blk = jnp.zeros((DK_A, DV_A), BF16)

    outs = []
    for n in range(n_chunks):
        sl = slice(n * c, (n + 1) * c)
        qn, kn, vn, bn = q[sl], k[sl], v[sl], bcum[sl]
        b_ref = bn[mid:mid + 1]
        b_last = bn[last:last + 1]
        q_rel = (qn * jnp.exp(bn - b_ref)).astype(BF16)
        k_rel = (kn * jnp.exp(b_ref - bn)).astype(BF16)
        k_dec = kn * jnp.exp(b_last - bn)
        q_dec = (qn * jnp.exp(bn)).astype(BF16)
        k_bd = jnp.concatenate([k_rel] * 4, axis=0) * mask_k
        scores = jnp.where(causal, _nt_dot(q_rel, k_bd), 0.0).astype(BF16)
        v_bd = jnp.concatenate([vn] * 4, axis=0) * mask_v
        o = jnp.dot(scores, v_bd, preferred_element_type=F32)
        s_old = [s_ref[h] for h in range(4)]
        s_bd = jnp.concatenate(
            [jnp.concatenate([zero_blk] * h + [s_old[h].astype(BF16)] + [zero_blk] * (3 - h), axis=1)
             for h in range(4)], axis=0)
        o = o + jnp.dot(q_dec, s_bd, preferred_element_type=F32)
        decay_rows = jnp.broadcast_to(jnp.exp(b_last), (LANES - c, hk))
        kt = jnp.concatenate([k_dec, decay_rows], axis=0).T
        kt_b = kt.astype(BF16)
        v_pad = jnp.concatenate([vn, zero_v], axis=0)
        for h in range(4):
            kth = kt[h * DK_A:(h + 1) * DK_A]
            decay = jnp.where(lane < c, pltpu.roll(kth, c, 1), kth)
            kv = jnp.dot(kt_b[h * DK_A:(h + 1) * DK_A], v_pad[:, h * DV_A:(h + 1) * DV_A],
                         preferred_element_type=F32)
            s_ref[h] = s_old[h] * decay + kv
        outs.append(o)
    return outs[0] if n_chunks == 1 else jnp.concatenate(outs, axis=0)


def _linattn_kernel(qa_ref, ka_ref, va_ref, ra_ref, qb_ref, kb_ref, vb_ref, gb_ref, lo_ref,
                    wgu_ref, bg_ref, gg_ref, gr_ref, cos_ref, sin_ref, lgam_ref, tri_ref, mk_ref,
                    mv_ref, *rest, rows, has_state):
    if has_state:
        s0a_ref, s0b_ref, mix_ref, sa_out, sb_out, sa, sb = rest
    else:
        mix_ref, sa_out, sb_out, sa, sb = rest
    t = pl.program_id(1)
    c = CHUNK
    padded = max(rows, c)
    n_chunks = padded // c
    valid = min(rows, c)
    mid, last = (valid - 1) // 2, valid - 1

    @pl.when(t == 0)
    def _init():
        if has_state:
            sa[...] = s0a_ref[0]
            sb[...] = s0b_ref[0]
        else:
            sa[...] = jnp.zeros_like(sa)
            sb[...] = jnp.zeros_like(sb)

    def pad_rows(x):
        if padded == rows:
            return x
        return jnp.concatenate([x, jnp.zeros((padded - rows, x.shape[1]), x.dtype)], axis=0)

    tri = tri_ref[...]
    mask_k, mask_v = mk_ref[...], mv_ref[...]

    gate = jnp.dot(lo_ref[...], wgu_ref[...], preferred_element_type=F32) + bg_ref[...]
    log_a = (jnp.minimum(gate, 0.0) - jnp.log1p(jnp.exp(-jnp.abs(gate)))) / GATE_TAU
    log_a = pad_rows(log_a)
    hi = log_a.astype(BF16)
    lo = (log_a - hi.astype(F32)).astype(BF16)
    bcum_a = (jnp.dot(tri, hi, preferred_element_type=F32)
              + jnp.dot(tri, lo, preferred_element_type=F32))
    qa = pad_rows(qa_ref[...].astype(F32) * DK_A ** -0.5)
    ka = pad_rows(ka_ref[...].astype(F32))
    va = pad_rows(va_ref[...])
    o_a = _decay_attention_tile(qa, ka, va, bcum_a, sa, mask_k, mask_v, n_chunks, mid, last)[:rows]

    cos, sin = cos_ref[...], sin_ref[...]
    first = _rope_first_half(cos.shape)
    qb = pad_rows(_rope(qb_ref[...].astype(F32), cos, sin, first))
    kb = pad_rows(_rope(kb_ref[...].astype(F32), cos, sin, first) * DK_B ** -0.5)
    vb = pad_rows(vb_ref[...])
    pos = lax.broadcasted_iota(jnp.int32, (padded, 4 * DK_B), 0) % c
    steps = jnp.where(pos < valid, pos + 1, valid).astype(F32)
    bcum_b = steps * lgam_ref[...]
    o_b = _decay_attention_tile(qb, kb, vb, bcum_b, sb, mask_k, mask_v, n_chunks, mid, last)[:rows]

    gg, gr = gg_ref[...], gr_ref[...]
    for h in range(4):
        sl = slice(h * DV_A, (h + 1) * DV_A)
        oh = o_a[:, sl]
        oh = oh * lax.rsqrt(jnp.mean(oh * oh, axis=-1, keepdims=True) + EPS) * gg[:, sl]
        r = ra_ref[:, sl].astype(F32)
        mix_ref[:, sl] = (oh * (r * jax.nn.sigmoid(r))).astype(mix_ref.dtype)
        ob = o_b[:, sl]
        ob = ob - jnp.mean(ob, axis=-1, keepdims=True)
        ob = ob * lax.rsqrt(jnp.mean(ob * ob, axis=-1, keepdims=True) + EPS) * gr[:, sl]
        gb = gb_ref[:, sl].astype(F32)
        mix_ref[:, 4 * DV_A + h * DV_B:4 * DV_A + (h + 1) * DV_B] = (
            ob * (gb * jax.nn.sigmoid(gb))).astype(mix_ref.dtype)

    @pl.when(t == pl.num_programs(1) - 1)
    def _emit_state():
        sa_out[0] = sa[...]
        sb_out[0] = sb[...]


def _linattn(slab, batch, seq, rows, w_gu, b_gate, g_gla, g_ret, cos, sin, lgam, s0a, s0b):
    nt = seq // rows
    has_state = s0a is not None
    c = CHUNK
    padded = max(rows, c)
    r = np.arange(padded)
    tri = jnp.asarray(((r[:, None] // c == r[None, :] // c) & (r[None, :] <= r[:, None])), BF16)
    r4 = np.arange(4 * c)
    mask_k = jnp.asarray(r4[:, None] // c == np.arange(4 * DK_A)[None, :] // DK_A, BF16)
    mask_v = jnp.asarray(r4[:, None] // c == np.arange(4 * DV_A)[None, :] // DV_A, BF16)
    def col(width, idx):
        return pl.BlockSpec((rows, width), lambda b, t, idx=idx: (b * nt + t, idx))

    def const(shape):
        return pl.BlockSpec(shape, lambda b, t: (0,) * len(shape))

    hk = 4 * DK_A
    hv = 4 * DV_A
    in_specs = [col(hk, 0), col(hk, 1), col(hv, 1), col(hv, 2),
                col(hk, 6), col(hk, 7), col(hv, 4), col(hv, 5),
                col(LANES, 24),
                const((LANES, hk)), const((1, hk)), const((1, hv)), const((1, hv)),
                pl.BlockSpec((rows, hk), lambda b, t: (t, 0)),
                pl.BlockSpec((rows, hk), lambda b, t: (t, 0)),
                const((1, hk)), const(tri.shape), const(mask_k.shape), const(mask_v.shape)]
    args = [slab] * 9 + [w_gu, b_gate, g_gla, g_ret, cos, sin, lgam, tri, mask_k, mask_v]
    state_spec = pl.BlockSpec((1, 4, DK_A, DV_A), lambda b, t: (b, 0, 0, 0))
    if has_state:
        in_specs += [state_spec, state_spec]
        args += [s0a, s0b]
    state_shape = jax.ShapeDtypeStruct((batch, 4, DK_A, DV_A), F32)
    return pl.pallas_call(
        functools.partial(_linattn_kernel, rows=rows, has_state=has_state),
        grid=(batch, nt),
        in_specs=in_specs,
        out_specs=[pl.BlockSpec((rows, 2 * hv), lambda b, t: (b * nt + t, 0)), state_spec, state_spec],
        out_shape=[jax.ShapeDtypeStruct((batch * seq, 2 * hv), BF16), state_shape, state_shape],
        scratch_shapes=[pltpu.VMEM((4, DK_A, DV_A), F32), pltpu.VMEM((4, DK_B, DV_B), F32)],
        compiler_params=_params(2),
        name="decay_linear_attention",
    )(*args)


def _ffn_kernel(x_ref, mix_ref, wo_ref, g_ref, wa_ref, wu_ref, wdw_ref, bdw_ref, wout_ref, st_ref,
                gf_ref, y_ref, cv_ref, h_s, carry_s,
                *, tm, sub, prev_rows, shift, tiles_per_seq, final_norm):
    i = pl.program_id(0)
    j = pl.program_id(1)
    p = prev_rows
    subs = [slice(r, r + sub) for r in range(0, tm, sub)]

    @pl.when(j == 0)
    def _start():
        for rs in subs:
            xm = x_ref[rs] + jnp.dot(mix_ref[rs], wo_ref[...], preferred_element_type=F32)
            y_ref[rs] = xm
            h_s[rs] = _rms(xm, g_ref[...]).astype(BF16)

    at_start = (i % tiles_per_seq) == 0
    tail = jnp.where(at_start, st_ref[0], carry_s[j])
    last = j == pl.num_programs(1) - 1
    wdw = wdw_ref[...]
    bdw = bdw_ref[...]
    a = None
    for rs in subs:
        h = h_s[rs]
        a = jnp.dot(h, wa_ref[...], preferred_element_type=F32)
        u = jnp.dot(h, wu_ref[...], preferred_element_type=F32)
        ext = jnp.concatenate([tail, a], axis=0)
        conv = bdw + pltpu.roll(ext, 2 * shift, 0)[p:] * wdw[0:1]
        conv = conv + pltpu.roll(ext, shift, 0)[p:] * wdw[1:2]
        conv = conv + a * wdw[2:3]
        act = conv * (1.0 + lax.erf(conv * np.float32(np.sqrt(0.5)))) * u
        out = y_ref[rs] + jnp.dot(act.astype(BF16), wout_ref[...], preferred_element_type=F32)
        if final_norm:
            out = jnp.where(last, _rms(out, gf_ref[...]), out)
        y_ref[rs] = out
        tail = a[sub - p:]
    carry_s[j] = tail
    cv_ref[0] = a[sub - 2 * shift:]


def _ffn(x, mix, w_o, g_ffn, w_in, w_dw, b_dw, w_out, conv_state, g_final, *, tm, tf, shift,
         tiles_per_seq, final_norm):
    m, d = x.shape
    n_f = D_FF // tf
    n_seq, prev_rows, _ = conv_state.shape
    kern = functools.partial(_ffn_kernel, tm=tm, sub=min(tm, SUB_FFN), prev_rows=prev_rows,
                             shift=shift, tiles_per_seq=tiles_per_seq, final_norm=final_norm)
    return pl.pallas_call(
        kern,
        grid=(m // tm, n_f),
        in_specs=[pl.BlockSpec((tm, d), lambda i, j: (i, 0)),
                  pl.BlockSpec((tm, d), lambda i, j: (i, 0)),
                  pl.BlockSpec((d, d), lambda i, j: (0, 0)),
                  pl.BlockSpec((1, d), lambda i, j: (0, 0)),
                  pl.BlockSpec((d, tf), lambda i, j: (0, j)),
                  pl.BlockSpec((d, tf), lambda i, j: (0, n_f + j)),
                  pl.BlockSpec((CONV_W, tf), lambda i, j: (0, j)),
                  pl.BlockSpec((1, tf), lambda i, j: (0, j)),
                  pl.BlockSpec((tf, d), lambda i, j: (j, 0)),
                  pl.BlockSpec((1, prev_rows, tf), lambda i, j: (i // tiles_per_seq, 0, j)),
                  pl.BlockSpec((1, d), lambda i, j: (0, 0))],
        out_specs=[pl.BlockSpec((tm, d), lambda i, j: (i, 0)),
                   pl.BlockSpec((1, 2 * shift, tf), lambda i, j: (i, 0, j))],
        out_shape=[jax.ShapeDtypeStruct((m, d), F32),
                   jax.ShapeDtypeStruct((m // tm, 2 * shift, D_FF), F32)],
        scratch_shapes=[pltpu.VMEM((tm, d), BF16), pltpu.VMEM((n_f, prev_rows, tf), F32)],
        compiler_params=_params(2),
        name="outproj_convffn",
    )(x, mix, w_o, g_ffn, w_in, w_in, w_dw, b_dw, w_out, conv_state, g_final)


def _mla_latents(x_ref, g_ref, win_ref, gq_ref, gkv_ref, cos_ref, sin_ref, ckv_ref, kr_ref):
    h = _rms(x_ref[...], g_ref[...]).astype(BF16)
    p = jnp.dot(h, win_ref[...], preferred_element_type=F32)
    cq_n = _rms(p[:, 0:Q_LORA], gq_ref[...]).astype(BF16)
    ckv_n = _rms(p[:, Q_LORA:Q_LORA + KV_LORA], gkv_ref[...])
    ckv_ref[...] = ckv_n
    kr = _rope(p[:, Q_LORA + KV_LORA:], cos_ref[...], sin_ref[...])
    kr_ref[...] = kr[:, 0:ROPE]
    return cq_n, ckv_n, kr


def _mla_proj_prompt_kernel(x_ref, g_ref, win_ref, gq_ref, gkv_ref, wuq_ref, wukv_ref, cos_ref,
                            sin_ref, q_ref, k_ref, v_ref, ckv_ref, kr_ref, *, scale):
    cq_n, ckv_n, kr = _mla_latents(x_ref, g_ref, win_ref, gq_ref, gkv_ref, cos_ref, sin_ref,
                                   ckv_ref, kr_ref)
    cos, sin = cos_ref[...], sin_ref[...]
    kr_b = kr.astype(BF16)
    ckv_b = ckv_n.astype(BF16)
    for hh in range(H_C):
        qh = jnp.dot(cq_n, wuq_ref[:, hh * HEAD_W:(hh + 1) * HEAD_W],
                     preferred_element_type=F32) * scale
        q_ref[hh, :, 0:NOPE] = qh[:, 0:NOPE].astype(BF16)
        q_ref[hh, :, NOPE:HEAD_W] = _rope(qh[:, NOPE:HEAD_W], cos, sin).astype(BF16)
        kvh = jnp.dot(ckv_b, wukv_ref[:, hh * HEAD_W:(hh + 1) * HEAD_W],
                      preferred_element_type=F32)
        k_ref[hh, :, 0:NOPE] = kvh[:, 0:NOPE].astype(BF16)
        k_ref[hh, :, NOPE:HEAD_W] = kr_b
        v_ref[hh] = kvh[:, NOPE:HEAD_W].astype(BF16)


def _mla_proj_prompt(x, g, w_in, g_q, g_kv, w_uq, w_ukv, cos, sin, seq, tm, scale):
    m, d = x.shape
    nt = seq // tm
    const2 = lambda shape: pl.BlockSpec(shape, lambda i: (0, 0))
    return pl.pallas_call(
        functools.partial(_mla_proj_prompt_kernel, scale=scale),
        grid=(m // tm,),
        in_specs=[pl.BlockSpec((tm, d), lambda i: (i, 0)), const2((1, d)), const2(w_in.shape),
                  const2((1, Q_LORA)), const2((1, KV_LORA)), const2(w_uq.shape), const2(w_ukv.shape),
                  pl.BlockSpec((tm, LANES), lambda i: (i % nt, 0)),
                  pl.BlockSpec((tm, LANES), lambda i: (i % nt, 0))],
        out_specs=[pl.BlockSpec((H_C, tm, HEAD_W), lambda i: (0, i, 0)),
                   pl.BlockSpec((H_C, tm, HEAD_W), lambda i: (0, i, 0)),
                   pl.BlockSpec((H_C, tm, V_C), lambda i: (0, i, 0)),
                   pl.BlockSpec((tm, KV_LORA), lambda i: (i, 0)),
                   pl.BlockSpec((tm, ROPE), lambda i: (i, 0))],
        out_shape=[jax.ShapeDtypeStruct((H_C, m, HEAD_W), BF16),
                   jax.ShapeDtypeStruct((H_C, m, HEAD_W), BF16),
                   jax.ShapeDtypeStruct((H_C, m, V_C), BF16),
                   jax.ShapeDtypeStruct((m, KV_LORA), F32),
                   jax.ShapeDtypeStruct((m, ROPE), F32)],
        compiler_params=_params(1),
        name="mla_proj_prompt",
    )(x, g, w_in, g_q, g_kv, w_uq, w_ukv, cos, sin)


def _mla_proj_sample_kernel(x_ref, g_ref, win_ref, gq_ref, gkv_ref, wuq_ref, wukt_ref, cos_ref,
                            sin_ref, qlat_ref, qr_ref, ckv_ref, kr_ref, *, scale):
    cq_n, _, _ = _mla_latents(x_ref, g_ref, win_ref, gq_ref, gkv_ref, cos_ref, sin_ref,
                              ckv_ref, kr_ref)
    cos, sin = cos_ref[...], sin_ref[...]
    for hh in range(H_C):
        qh = jnp.dot(cq_n, wuq_ref[:, hh * HEAD_W:(hh + 1) * HEAD_W],
                     preferred_element_type=F32)
        q_lat = jnp.dot(qh[:, 0:NOPE].astype(BF16), wukt_ref[hh], preferred_element_type=F32)
        qlat_ref[hh] = (q_lat * scale).astype(BF16)
        qr_ref[hh] = (_rope(qh[:, NOPE:HEAD_W], cos, sin) * scale)[:, 0:ROPE].astype(BF16)


def _mla_proj_sample(x, g, w_in, g_q, g_kv, w_uq, w_ukt, cos, sin, scale):
    m, d = x.shape
    const = lambda shape: pl.BlockSpec(shape, lambda i: (0,) * len(shape))
    return pl.pallas_call(
        functools.partial(_mla_proj_sample_kernel, scale=scale),
        grid=(1,),
        in_specs=[const((m, d)), const((1, d)), const(w_in.shape), const((1, Q_LORA)),
                  const((1, KV_LORA)), const(w_uq.shape), const(w_ukt.shape),
                  const((m, LANES)), const((m, LANES))],
        out_specs=[const((H_C, m, KV_LORA)), const((H_C, m, ROPE)), const((m, KV_LORA)),
                   const((m, ROPE))],
        out_shape=[jax.ShapeDtypeStruct((H_C, m, KV_LORA), BF16),
                   jax.ShapeDtypeStruct((H_C, m, ROPE), BF16),
                   jax.ShapeDtypeStruct((m, KV_LORA), F32),
                   jax.ShapeDtypeStruct((m, ROPE), F32)],
        compiler_params=_params(1),
        name="mla_proj_sample",
    )(x, g, w_in, g_q, g_kv, w_uq, w_ukt, cos, sin)


def _flash_kernel(q_ref, k_ref, v_ref, o_ref, m_s, acc_s, *, tile, n_tiles):
    t = tile
    row = lax.broadcasted_iota(jnp.int32, (t, t), 0)
    col = lax.broadcasted_iota(jnp.int32, (t, t), 1)
    visible = (col // CHUNK) <= (row // CHUNK)
    ones = jnp.ones((t, V_C), BF16)
    for j in range(n_tiles):
        ks = slice(j * t, (j + 1) * t)
        kj = k_ref[0, ks, :]
        vj = jnp.concatenate([v_ref[0, ks, :], ones], axis=1)
        for i in range(j, n_tiles):
            qs = slice(i * t, (i + 1) * t)
            s = _nt_dot(q_ref[0, qs, :], kj)
            if i == j:
                s = jnp.where(visible, s, -jnp.inf)
            m_new = jnp.broadcast_to(jnp.max(s, axis=-1, keepdims=True), (t, LANES))
            if j > 0:
                m_prev = m_s[qs]
                m_new = jnp.maximum(m_prev, m_new)
            p = jnp.exp2(s - jnp.concatenate([m_new] * (t // LANES), axis=1))
            pv = jnp.dot(p.astype(BF16), vj, preferred_element_type=F32)
            if j > 0:
                alpha = jnp.exp2(m_prev - m_new)
                pv = pv + jnp.concatenate([alpha, alpha], axis=1) * acc_s[qs]
            if i == j:
                o_ref[qs, :] = (pv[:, :V_C] / pv[:, V_C:]).astype(o_ref.dtype)
            else:
                m_s[qs] = m_new
                acc_s[qs] = pv


def _flash_prompt(q, k, v, batch, seq, tile):
    m = batch * seq
    return pl.pallas_call(
        functools.partial(_flash_kernel, tile=tile, n_tiles=seq // tile),
        grid=(batch, H_C),
        in_specs=[pl.BlockSpec((1, seq, HEAD_W), lambda b, h: (h, b, 0)),
                  pl.BlockSpec((1, seq, HEAD_W), lambda b, h: (h, b, 0)),
                  pl.BlockSpec((1, seq, V_C), lambda b, h: (h, b, 0))],
        out_specs=pl.BlockSpec((seq, V_C), lambda b, h: (b, h)),
        out_shape=jax.ShapeDtypeStruct((m, H_C * V_C), BF16),
        scratch_shapes=[pltpu.VMEM((seq, LANES), F32), pltpu.VMEM((seq, 2 * V_C), F32)],
        compiler_params=_params(2),
        name="flash_prompt",
    )(q, k, v)


def _latent_attn_kernel(qlat_ref, qr_ref, cckv_ref, ckr_ref, nckv_ref, nkr_ref, wuv_ref, o_ref,
                        *, steps, new_rows, tk):
    rows = H_C * steps
    q_lat = qlat_ref[...].reshape(rows, KV_LORA)
    q_r = qr_ref[...].reshape(rows, ROPE)

    def update(state, ckv, kr, valid):
        ckv_b = ckv.astype(BF16)
        s = _nt_dot(q_lat, ckv_b) + _nt_dot(q_r, kr.astype(BF16))
        if valid is not None:
            col = lax.broadcasted_iota(jnp.int32, s.shape, 1)
            s = jnp.where(col < valid, s, -jnp.inf)
        m_blk = jnp.max(s, axis=-1, keepdims=True)
        if state is None:
            m_new = m_blk
            p = jnp.exp(s - m_new)
            return (m_new, jnp.sum(p, axis=-1, keepdims=True),
                    jnp.dot(p.astype(BF16), ckv_b, preferred_element_type=F32))
        m_prev, l_prev, acc = state
        m_new = jnp.maximum(m_prev, m_blk)
        p = jnp.exp(s - m_new)
        alpha = jnp.exp(m_prev - m_new)
        return (m_new, alpha * l_prev + jnp.sum(p, axis=-1, keepdims=True),
                alpha * acc + jnp.dot(p.astype(BF16), ckv_b, preferred_element_type=F32))

    state = None
    for j in range(cckv_ref.shape[1] // tk):
        ks = slice(j * tk, (j + 1) * tk)
        state = update(state, cckv_ref[0, ks, :], ckr_ref[0, ks, :], None)
    _, l_fin, acc = update(state, nckv_ref[0], nkr_ref[0], new_rows)
    o_lat = (acc / l_fin).astype(BF16)
    for hh in range(H_C):
        o_ref[:, hh * V_C:(hh + 1) * V_C] = jnp.dot(
            o_lat[hh * steps:(hh + 1) * steps], wuv_ref[hh],
            preferred_element_type=F32).astype(o_ref.dtype)


def _latent_attn_sample(q_lat, q_r, cache_ckv, cache_kr, new_ckv, new_kr, w_uv, steps, tk):
    batch, past, _ = cache_ckv.shape
    pad_rows = new_ckv.shape[1]
    return pl.pallas_call(
        functools.partial(_latent_attn_kernel, steps=steps, new_rows=steps, tk=tk),
        grid=(batch,),
        in_specs=[pl.BlockSpec((H_C, steps, KV_LORA), lambda b: (0, b, 0)),
                  pl.BlockSpec((H_C, steps, ROPE), lambda b: (0, b, 0)),
                  pl.BlockSpec((1, past, KV_LORA), lambda b: (b, 0, 0)),
                  pl.BlockSpec((1, past, ROPE), lambda b: (b, 0, 0)),
                  pl.BlockSpec((1, pad_rows, KV_LORA), lambda b: (b, 0, 0)),
                  pl.BlockSpec((1, pad_rows, ROPE), lambda b: (b, 0, 0)),
                  pl.BlockSpec((H_C, KV_LORA, V_C), lambda b: (0, 0, 0))],
        out_specs=pl.BlockSpec((steps, H_C * V_C), lambda b: (b, 0)),
        out_shape=jax.ShapeDtypeStruct((batch * steps, H_C * V_C), BF16),
        compiler_params=_params(1),
        name="latent_attn_sample",
    )(q_lat, q_r, cache_ckv, cache_kr, new_ckv, new_kr, w_uv)


def _rope_tables(pos, width):
    half = ROPE // 2
    freqs = ROPE_THETA ** (-jnp.arange(half, dtype=F32) / half)
    ang = pos.astype(F32)[:, None] * freqs[None, :]
    cos, sin = jnp.cos(ang), jnp.sin(ang)
    cos_h = jnp.concatenate([cos, cos], axis=-1)
    sin_h = jnp.concatenate([-sin, sin], axis=-1)
    reps = width // ROPE
    return jnp.tile(cos_h, (1, reps)), jnp.tile(sin_h, (1, reps))


def _prep_even(w_in_ab, w_gate_up):
    sizes = [H_A * DK_A, H_A * DK_A, H_A * DV_A, H_A * DV_A, GATE_RANK,
             H_B * DK_B, H_B * DK_B, H_B * DV_B, H_B * DV_B]
    offs = np.concatenate([[0], np.cumsum(sizes)])
    parts = [w_in_ab[:, offs[n]:offs[n + 1]] for n in range(9)]
    lo = jnp.pad(parts[4], ((0, 0), (0, LANES - GATE_RANK)))
    w = jnp.concatenate(parts[0:4] + parts[5:9] + [lo], axis=1).astype(BF16)
    w_gu = jnp.pad(w_gate_up, ((0, LANES - GATE_RANK), (0, 0))).astype(BF16)
    return w, w_gu


def _prep_odd(w_in_c, w_uq, w_uk, w_uv):
    d = w_in_c.shape[0]
    w_in = jnp.pad(w_in_c, ((0, 0), (0, LANES - ROPE))).astype(BF16)
    uq = w_uq.reshape(Q_LORA, H_C, NOPE + ROPE)
    uq = jnp.pad(uq, ((0, 0), (0, 0), (0, HEAD_W - NOPE - ROPE)))
    w_uq_p = uq.reshape(Q_LORA, H_C * HEAD_W).astype(BF16)
    w_ukv = jnp.concatenate([w_uk, w_uv], axis=2).reshape(KV_LORA, H_C * HEAD_W).astype(BF16)
    w_ukt = jnp.transpose(w_uk, (1, 2, 0)).astype(BF16)
    w_uvh = jnp.transpose(w_uv, (1, 0, 2)).astype(BF16)
    del d
    return w_in, w_uq_p, w_ukv, w_ukt, w_uvh


def _log_gamma_row():
    lg = np.log1p(-np.exp2(-5.0 - np.arange(H_B, dtype=np.float32))).astype(np.float32)
    return jnp.asarray(np.repeat(lg, DK_B)[None, :])


def _to_time_major(a, batch, steps):
    return a.reshape(batch, steps, -1).transpose(1, 0, 2).reshape(batch * steps, -1)


def _to_batch_major(a, batch, steps):
    return a.reshape(steps, batch, -1).transpose(1, 0, 2).reshape(batch * steps, -1)


def _divisor_tile(n, pref):
    t = min(n, pref)
    while n % t:
        t //= 2
    return t


def _trunk(x, pos0, gla0, ret0, ckv_past, kr_past, conv0, w, time_major_ffn):
    batch, seq, d = x.shape
    m = batch * seq
    pos = pos0 + jnp.arange(seq, dtype=jnp.int32)
    row = lambda v: v.reshape(1, -1)
    scale = float((NOPE + ROPE) ** -0.5)
    tf = TF_FFN
    tm_proj = _divisor_tile(m, TM_PROJ)

    if time_major_ffn:
        ffn_kw = dict(tm=m, tf=tf, shift=batch, tiles_per_seq=1)
        to_ffn = lambda a: _to_time_major(a, batch, seq)
        from_ffn = lambda a: _to_batch_major(a, batch, seq)
        prep_state = lambda s: s.transpose(1, 0, 2).reshape(1, 2 * batch, D_FF)
        post_state = lambda s: s.reshape(2, batch, D_FF).transpose(1, 0, 2)
    else:
        tm_ffn = _divisor_tile(seq, TM_FFN)
        ffn_kw = dict(tm=tm_ffn, tf=tf, shift=1, tiles_per_seq=seq // tm_ffn)
        to_ffn = from_ffn = lambda a: a
        prep_state = lambda s: jnp.pad(s, ((0, 0), (SUBLANES - 2, 0), (0, 0)))
        post_state = lambda s: s

    xf = x.reshape(m, d)
    conv_new = []
    gla_new = ret_new = ckv_new = kr_new = None
    for layer in range(2):
        g_mix = row(w['norm_mix'][layer])
        if layer == 0:
            w_ab, w_gu = _prep_even(w['w_in_ab'][0], w['w_gate_up'][0])
            slab = _norm_matmul(xf, g_mix, w_ab, tm_proj, 4 * HEAD_W)
            cos, sin = _rope_tables(pos, 4 * DK_B)
            rows = _divisor_tile(seq, ROWS_LINATTN)
            mix, gla_new, ret_new = _linattn(
                slab, batch, seq, rows, w_gu, row(w['b_gate'][0]), row(w['g_gla'][0]),
                row(w['g_ret'][0]), cos, sin, _log_gamma_row(),
                None if gla0 is None else gla0[0], None if ret0 is None else ret0[0])
            w_o = w['w_out_ab'][0].astype(BF16)
        else:
            w_in, w_uq_p, w_ukv, w_ukt, w_uvh = _prep_odd(w['w_in_c'][0], w['w_uq'][0], w['w_uk'][0],
                                                          w['w_uv'][0])
            cos, sin = _rope_tables(pos, LANES)
            g_q, g_kv = row(w['g_q'][0]), row(w['g_kv'][0])
            if ckv_past is None:
                q, k, v, ckv_new, kr_new = _mla_proj_prompt(
                    xf, g_mix, w_in, g_q, g_kv, w_uq_p, w_ukv, cos, sin, seq, tm_proj,
                    scale * float(np.log2(np.e)))
                mix = _flash_prompt(q, k, v, batch, seq, _divisor_tile(seq, TQ_FLASH))
            else:
                assert ckv_past.shape[2] % CHUNK == 0 and seq <= CHUNK
                cos_m, sin_m = jnp.tile(cos, (batch, 1)), jnp.tile(sin, (batch, 1))
                q_lat, q_r, ckv_new, kr_new = _mla_proj_sample(
                    xf, g_mix, w_in, g_q, g_kv, w_uq_p, w_ukt, cos_m, sin_m, scale)
                pad = ((0, 0), (0, LANES - seq), (0, 0))
                new_ckv = jnp.pad(ckv_new.reshape(batch, seq, KV_LORA), pad)
                new_kr = jnp.pad(kr_new.reshape(batch, seq, ROPE), pad)
                mix = _latent_attn_sample(q_lat, q_r, ckv_past[0], kr_past[0], new_ckv, new_kr,
                                          w_uvh, seq, _divisor_tile(ckv_past.shape[2], TK_LATENT))
            w_o = w['w_out_c'][0].astype(BF16)

        if conv0 is None:
            state = jnp.zeros((batch, SUBLANES, D_FF), F32) if not time_major_ffn else \
                jnp.zeros((1, 2 * batch, D_FF), F32)
        else:
            state = prep_state(conv0[layer])
        y, conv_rows = _ffn(to_ffn(xf), to_ffn(mix), w_o, row(w['norm_ffn'][layer]),
                            w['w_ffn_in'][layer].astype(BF16), w['w_dwconv'][layer],
                            row(w['b_dwconv'][layer]),
                            (0.5 * w['w_ffn_out'][layer]).astype(BF16), state,
                            row(w['norm_final']), final_norm=(layer == 1), **ffn_kw)
        xf = from_ffn(y)
        tps = ffn_kw['tiles_per_seq']
        conv_new.append(post_state(conv_rows[tps - 1::tps]))

    return (xf.reshape(batch, seq, d), gla_new[None], ret_new[None],
            ckv_new.reshape(1, batch, seq, KV_LORA), kr_new.reshape(1, batch, seq, ROPE),
            jnp.stack(conv_new))


def kernel(x_prompt, x_sample, state_gla, state_ret, cache_ckv, cache_krope, state_conv, norm_mix, norm_ffn, norm_final, w_in_ab, w_gate_up, b_gate, g_gla, g_ret, w_out_ab, w_in_c, g_q, g_kv, w_uq, w_uk, w_uv, w_out_c, w_ffn_in, w_dwconv, b_dwconv, w_ffn_out):
    w = {'norm_mix': norm_mix, 'norm_ffn': norm_ffn, 'norm_final': norm_final,
         'w_in_ab': w_in_ab, 'w_gate_up': w_gate_up, 'b_gate': b_gate, 'g_gla': g_gla,
         'g_ret': g_ret, 'w_out_ab': w_out_ab, 'w_in_c': w_in_c, 'g_q': g_q, 'g_kv': g_kv,
         'w_uq': w_uq, 'w_uk': w_uk, 'w_uv': w_uv, 'w_out_c': w_out_c, 'w_ffn_in': w_ffn_in,
         'w_dwconv': w_dwconv, 'b_dwconv': b_dwconv, 'w_ffn_out': w_ffn_out}
    past_len = cache_ckv.shape[2]
    y_p, gla_p, ret_p, ckv_p, kr_p, conv_p = _trunk(
        x_prompt, 0, None, None, None, None, None, w, time_major_ffn=False)
    y_s, gla_s, ret_s, ckv_s, kr_s, conv_s = _trunk(
        x_sample, past_len, state_gla, state_ret, cache_ckv, cache_krope, state_conv, w,
        time_major_ffn=True)
    return (y_p, y_s, gla_p, gla_s, ret_p, ret_s, ckv_p, ckv_s, kr_p, kr_s, conv_p, conv_s)
```

```python
import functools

import numpy as np
import jax
import jax.numpy as jnp
from jax import lax
from jax.experimental import pallas as pl
from jax.experimental.pallas import tpu as pltpu

F32 = jnp.float32
BF16 = jnp.bfloat16

D_MODEL = 1024
CHUNK = 64
EPS = 1e-6
ROPE_THETA = 10000.0
H_A, DK_A, DV_A = 4, 64, 128
GATE_RANK = 16
GATE_TAU = 16.0
H_B, DK_B, DV_B = 4, 64, 128
H_C = 8
Q_LORA, KV_LORA, NOPE, ROPE, V_C = 384, 512, 128, 64, 128
D_FF = 2816
CONV_W = 3

LANES = 128
SUBLANES = 8
VMEM_LIMIT = 56 * 1024 * 1024
AB_COLS = 3200
HEAD_W = 2 * LANES
TM_PROJ = 512
TM_FFN = 1024
TF_FFN = D_FF
SUB_FFN = 256
ROWS_LINATTN = 256
TQ_FLASH = 256
TK_LATENT = 1024


def _params(n_axes):
    return pltpu.CompilerParams(dimension_semantics=("arbitrary",) * n_axes,
                                vmem_limit_bytes=VMEM_LIMIT)


def _rms(x, g):
    return x * lax.rsqrt(jnp.mean(x * x, axis=-1, keepdims=True) + EPS) * g


def _rope_first_half(shape):
    lane = lax.broadcasted_iota(jnp.int32, shape, 1)
    return (lane % ROPE) < ROPE // 2


def _rope(x, cos, sin_signed, first=None):
    w = x.shape[1]
    half = ROPE // 2
    if first is None:
        first = _rope_first_half(x.shape)
    swapped = jnp.where(first, pltpu.roll(x, w - half, 1), pltpu.roll(x, half, 1))
    return x * cos + swapped * sin_signed


def _nt_dot(a, b):
    return lax.dot_general(a, b, (((1,), (1,)), ((), ())), preferred_element_type=F32)


def _norm_matmul_kernel(x_ref, g_ref, w_ref, o_ref, *, col_chunk):
    h = _rms(x_ref[...], g_ref[...]).astype(BF16)
    n = w_ref.shape[1]
    for c0 in range(0, n, col_chunk):
        c1 = min(c0 + col_chunk, n)
        o_ref[:, c0:c1] = jnp.dot(h, w_ref[:, c0:c1], preferred_element_type=F32).astype(o_ref.dtype)


def _norm_matmul(x, g, w, tm, col_chunk):
    m, d = x.shape
    n = w.shape[1]
    return pl.pallas_call(
        functools.partial(_norm_matmul_kernel, col_chunk=col_chunk),
        grid=(m // tm,),
        in_specs=[pl.BlockSpec((tm, d), lambda i: (i, 0)),
                  pl.BlockSpec((1, d), lambda i: (0, 0)),
                  pl.BlockSpec((d, n), lambda i: (0, 0))],
        out_specs=pl.BlockSpec((tm, n), lambda i: (i, 0)),
        out_shape=jax.ShapeDtypeStruct((m, n), BF16),
        compiler_params=_params(1),
        name="norm_inproj_ab",
    )(x, g, w)


def _decay_attention_tile(q, k, v, bcum, s_ref, mask_k, mask_v, n_chunks, mid, last):
    c = CHUNK
    assert 2 * c == LANES
    hk, hv = 4 * DK_A, 4 * DV_A
    rt = lax.broadcasted_iota(jnp.int32, (c, 4 * c), 0)
    ct = lax.broadcasted_iota(jnp.int32, (c, 4 * c), 1)
    causal = (ct % c) <= rt
    lane = lax.broadcasted_iota(jnp.int32, (DK_A, LANES), 1)
    zero_v = jnp.zeros((LANES - c, hv), BF16)
    zero_blk = jnp.zeros((DK_A, DV_A), BF16)
    bcum2 = bcum * np.float32(np.log2(np.e))

    outs = []
    for n in range(n_chunks):
        sl = slice(n * c, (n + 1) * c)
        qn, kn, vn, bn = q[sl], k[sl], v[sl], bcum2[sl]
        b_ref = bn[mid:mid + 1]
        b_last = bn[last:last + 1]
        q_rel = (qn * jnp.exp2(bn - b_ref)).astype(BF16)
        k_rel = (kn * jnp.exp2(b_ref - bn)).astype(BF16)
        k_dec = kn * jnp.exp2(b_last - bn)
        q_dec = (qn * jnp.exp2(bn)).astype(BF16)
        k_bd = jnp.concatenate([k_rel] * 4, axis=0) * mask_k
        scores = jnp.where(causal, _nt_dot(q_rel, k_bd), 0.0).astype(BF16)
        v_bd = jnp.concatenate([vn] * 4, axis=0) * mask_v
        o = jnp.dot(scores, v_bd, preferred_element_type=F32)
        s_old = [s_ref[h] for h in range(4)]
        s_bd = jnp.concatenate(
            [jnp.concatenate([zero_blk] * h + [s_old[h].astype(BF16)] + [zero_blk] * (3 - h), axis=1)
             for h in range(4)], axis=0)
        o = o + jnp.dot(q_dec, s_bd, preferred_element_type=F32)
        decay_rows = jnp.broadcast_to(jnp.exp2(b_last), (LANES - c, hk))
        kt = jnp.concatenate([k_dec, decay_rows], axis=0).T
        kt_b = kt.astype(BF16)
        v_pad = jnp.concatenate([vn, zero_v], axis=0)
        for h in range(4):
            kth = kt[h * DK_A:(h + 1) * DK_A]
            decay = jnp.where(lane < c, pltpu.roll(kth, c, 1), kth)
            kv = jnp.dot(kt_b[h * DK_A:(h + 1) * DK_A], v_pad[:, h * DV_A:(h + 1) * DV_A],
                         preferred_element_type=F32)
            s_ref[h] = s_old[h] * decay + kv
        outs.append(o)
    return outs[0] if n_chunks == 1 else jnp.concatenate(outs, axis=0)


def _linattn_kernel(qa_ref, ka_ref, va_ref, ra_ref, qb_ref, kb_ref, vb_ref, gb_ref, lo_ref,
                    wgu_ref, bg_ref, gg_ref, gr_ref, cos_ref, sin_ref, lgam_ref, tri_ref, mk_ref,
                    mv_ref, *rest, rows, has_state):
    if has_state:
        s0a_ref, s0b_ref, mix_ref, sa_out, sb_out, sa, sb = rest
    else:
        mix_ref, sa_out, sb_out, sa, sb = rest
    t = pl.program_id(1)
    c = CHUNK
    padded = max(rows, c)
    n_chunks = padded // c
    valid = min(rows, c)
    mid, last = (valid - 1) // 2, valid - 1

    @pl.when(t == 0)
    def _init():
        if has_state:
            sa[...] = s0a_ref[0]
            sb[...] = s0b_ref[0]
        else:
            sa[...] = jnp.zeros_like(sa)
            sb[...] = jnp.zeros_like(sb)

    def pad_rows(x):
        if padded == rows:
            return x
        return jnp.concatenate([x, jnp.zeros((padded - rows, x.shape[1]), x.dtype)], axis=0)

    tri = tri_ref[...]
    mask_k, mask_v = mk_ref[...], mv_ref[...]

    gate = jnp.dot(lo_ref[...], wgu_ref[...], preferred_element_type=F32) + bg_ref[...]
    log_a = (jnp.minimum(gate, 0.0) - jnp.log1p(jnp.exp(-jnp.abs(gate)))) / GATE_TAU
    log_a = pad_rows(log_a)
    hi = log_a.astype(BF16)
    lo = (log_a - hi.astype(F32)).astype(BF16)
    bcum_a = (jnp.dot(tri, hi, preferred_element_type=F32)
              + jnp.dot(tri, lo, preferred_element_type=F32))
    qa = pad_rows(qa_ref[...].astype(F32) * DK_A ** -0.5)
    ka = pad_rows(ka_ref[...].astype(F32))
    va = pad_rows(va_ref[...])
    o_a = _decay_attention_tile(qa, ka, va, bcum_a, sa, mask_k, mask_v, n_chunks, mid, last)[:rows]

    cos, sin = cos_ref[...], sin_ref[...]
    first = _rope_first_half(cos.shape)
    qb = pad_rows(_rope(qb_ref[...].astype(F32), cos, sin, first))
    kb = pad_rows(_rope(kb_ref[...].astype(F32), cos, sin, first) * DK_B ** -0.5)
    vb = pad_rows(vb_ref[...])
    pos = lax.broadcasted_iota(jnp.int32, (padded, 4 * DK_B), 0) % c
    steps = jnp.where(pos < valid, pos + 1, valid).astype(F32)
    bcum_b = steps * lgam_ref[...]
    o_b = _decay_attention_tile(qb, kb, vb, bcum_b, sb, mask_k, mask_v, n_chunks, mid, last)[:rows]

    gg, gr = gg_ref[...], gr_ref[...]
    for h in range(4):
        sl = slice(h * DV_A, (h + 1) * DV_A)
        oh = o_a[:, sl]
        oh = oh * lax.rsqrt(jnp.mean(oh * oh, axis=-1, keepdims=True) + EPS) * gg[:, sl]
        r = ra_ref[:, sl].astype(F32)
        mix_ref[:, sl] = (oh * (r * jax.nn.sigmoid(r))).astype(mix_ref.dtype)
        ob = o_b[:, sl]
        ob = ob - jnp.mean(ob, axis=-1, keepdims=True)
        ob = ob * lax.rsqrt(jnp.mean(ob * ob, axis=-1, keepdims=True) + EPS) * gr[:, sl]
        gb = gb_ref[:, sl].astype(F32)
        mix_ref[:, 4 * DV_A + h * DV_B:4 * DV_A + (h + 1) * DV_B] = (
            ob * (gb * jax.nn.sigmoid(gb))).astype(mix_ref.dtype)

    @pl.when(t == pl.num_programs(1) - 1)
    def _emit_state():
        sa_out[0] = sa[...]
        sb_out[0] = sb[...]


def _linattn(slab, batch, seq, rows, w_gu, b_gate, g_gla, g_ret, cos, sin, lgam, s0a, s0b):
    nt = seq // rows
    has_state = s0a is not None
    c = CHUNK
    padded = max(rows, c)
    r = np.arange(padded)
    tri = jnp.asarray(((r[:, None] // c == r[None, :] // c) & (r[None, :] <= r[:, None])), BF16)
    r4 = np.arange(4 * c)
    mask_k = jnp.asarray(r4[:, None] // c == np.arange(4 * DK_A)[None, :] // DK_A, BF16)
    mask_v = jnp.asarray(r4[:, None] // c == np.arange(4 * DV_A)[None, :] // DV_A, BF16)
    def col(width, idx):
        return pl.BlockSpec((rows, width), lambda b, t, idx=idx: (b * nt + t, idx))

    def const(shape):
        return pl.BlockSpec(shape, lambda b, t: (0,) * len(shape))

    hk = 4 * DK_A
    hv = 4 * DV_A
    in_specs = [col(hk, 0), col(hk, 1), col(hv, 1), col(hv, 2),
                col(hk, 6), col(hk, 7), col(hv, 4), col(hv, 5),
                col(LANES, 24),
                const((LANES, hk)), const((1, hk)), const((1, hv)), const((1, hv)),
                pl.BlockSpec((rows, hk), lambda b, t: (t, 0)),
                pl.BlockSpec((rows, hk), lambda b, t: (t, 0)),
                const((1, hk)), const(tri.shape), const(mask_k.shape), const(mask_v.shape)]
    args = [slab] * 9 + [w_gu, b_gate, g_gla, g_ret, cos, sin, lgam, tri, mask_k, mask_v]
    state_spec = pl.BlockSpec((1, 4, DK_A, DV_A), lambda b, t: (b, 0, 0, 0))
    if has_state:
        in_specs += [state_spec, state_spec]
        args += [s0a, s0b]
    state_shape = jax.ShapeDtypeStruct((batch, 4, DK_A, DV_A), F32)
    return pl.pallas_call(
        functools.partial(_linattn_kernel, rows=rows, has_state=has_state),
        grid=(batch, nt),
        in_specs=in_specs,
        out_specs=[pl.BlockSpec((rows, 2 * hv), lambda b, t: (b * nt + t, 0)), state_spec, state_spec],
        out_shape=[jax.ShapeDtypeStruct((batch * seq, 2 * hv), BF16), state_shape, state_shape],
        scratch_shapes=[pltpu.VMEM((4, DK_A, DV_A), F32), pltpu.VMEM((4, DK_B, DV_B), F32)],
        compiler_params=_params(2),
        name="decay_linear_attention",
    )(*args)


def _ffn_kernel(x_ref, mix_ref, wo_ref, g_ref, wa_ref, wu_ref, wdw_ref, bdw_ref, wout_ref, st_ref,
                gf_ref, y_ref, cv_ref, h_s, carry_s,
                *, tm, sub, prev_rows, shift, tiles_per_seq, final_norm, single_step):
    i = pl.program_id(0)
    j = pl.program_id(1)
    p = prev_rows
    subs = [slice(r, r + sub) for r in range(0, tm, sub)]

    def mid_residual(rs):
        xm = x_ref[rs] + jnp.dot(mix_ref[rs], wo_ref[...], preferred_element_type=F32)
        return xm, _rms(xm, g_ref[...]).astype(BF16)

    if not single_step:
        @pl.when(j == 0)
        def _start():
            for rs in subs:
                y_ref[rs], h_s[rs] = mid_residual(rs)

    at_start = (i % tiles_per_seq) == 0
    tail = jnp.where(at_start, st_ref[0], carry_s[j])
    last = j == pl.num_programs(1) - 1
    wdw = wdw_ref[...]
    bdw = bdw_ref[...]
    a = None
    for rs in subs:
        if single_step:
            base, h = mid_residual(rs)
        else:
            base, h = y_ref[rs], h_s[rs]
        a = jnp.dot(h, wa_ref[...], preferred_element_type=F32)
        u = jnp.dot(h, wu_ref[...], preferred_element_type=F32)
        ext = jnp.concatenate([tail, a], axis=0)
        conv = bdw + pltpu.roll(ext, 2 * shift, 0)[p:] * wdw[0:1]
        conv = conv + pltpu.roll(ext, shift, 0)[p:] * wdw[1:2]
        conv = conv + a * wdw[2:3]
        act = conv * (1.0 + lax.erf(conv * np.float32(np.sqrt(0.5)))) * u
        out = base + jnp.dot(act.astype(BF16), wout_ref[...], preferred_element_type=F32)
        if final_norm:
            normed = _rms(out, gf_ref[...])
            out = normed if single_step else jnp.where(last, normed, out)
        y_ref[rs] = out
        tail = a[sub - p:]
    carry_s[j] = tail
    cv_ref[0] = a[sub - 2 * shift:]


def _ffn(x, mix, w_o, g_ffn, w_in, w_dw, b_dw, w_out, conv_state, g_final, *, layer, tm, tf, shift,
         tiles_per_seq, final_norm):
    m, d = x.shape
    n_f = D_FF // tf
    n_seq, prev_rows, _ = conv_state.shape
    kern = functools.partial(_ffn_kernel, tm=tm, sub=min(tm, SUB_FFN), prev_rows=prev_rows,
                             shift=shift, tiles_per_seq=tiles_per_seq, final_norm=final_norm,
                             single_step=(n_f == 1))
    h_rows = SUBLANES * 2 if n_f == 1 else tm
    return pl.pallas_call(
        kern,
        grid=(m // tm, n_f),
        in_specs=[pl.BlockSpec((tm, d), lambda i, j: (i, 0)),
                  pl.BlockSpec((tm, d), lambda i, j: (i, 0)),
                  pl.BlockSpec((d, d), lambda i, j: (0, 0)),
                  pl.BlockSpec((1, d), lambda i, j: (0, 0)),
                  pl.BlockSpec((None, d, tf), lambda i, j: (layer, 0, j)),
                  pl.BlockSpec((None, d, tf), lambda i, j: (layer, 0, n_f + j)),
                  pl.BlockSpec((CONV_W, tf), lambda i, j: (0, j)),
                  pl.BlockSpec((1, tf), lambda i, j: (0, j)),
                  pl.BlockSpec((tf, d), lambda i, j: (j, 0)),
                  pl.BlockSpec((1, prev_rows, tf), lambda i, j: (i // tiles_per_seq, 0, j)),
                  pl.BlockSpec((1, d), lambda i, j: (0, 0))],
        out_specs=[pl.BlockSpec((tm, d), lambda i, j: (i, 0)),
                   pl.BlockSpec((1, 2 * shift, tf), lambda i, j: (i, 0, j))],
        out_shape=[jax.ShapeDtypeStruct((m, d), F32),
                   jax.ShapeDtypeStruct((m // tm, 2 * shift, D_FF), F32)],
        scratch_shapes=[pltpu.VMEM((h_rows, d), BF16), pltpu.VMEM((n_f, prev_rows, tf), F32)],
        compiler_params=_params(2),
        name="outproj_convffn",
    )(x, mix, w_o, g_ffn, w_in, w_in, w_dw, b_dw, w_out, conv_state, g_final)


def _mla_latents(x_ref, g_ref, win_ref, gq_ref, gkv_ref, cos_ref, sin_ref, ckv_ref, kr_ref):
    h = _rms(x_ref[...], g_ref[...]).astype(BF16)
    p = jnp.dot(h, win_ref[...], preferred_element_type=F32)
    cq_n = _rms(p[:, 0:Q_LORA], gq_ref[...]).astype(BF16)
    ckv_n = _rms(p[:, Q_LORA:Q_LORA + KV_LORA], gkv_ref[...])
    ckv_ref[...] = ckv_n
    kr = _rope(p[:, Q_LORA + KV_LORA:], cos_ref[...], sin_ref[...])
    kr_ref[...] = kr[:, 0:ROPE]
    return cq_n, ckv_n, kr


def _mla_proj_prompt_kernel(x_ref, g_ref, win_ref, gq_ref, gkv_ref, wuq_ref, wukv_ref, cos_ref,
                            sin_ref, q_ref, k_ref, v_ref, ckv_ref, kr_ref, *, scale):
    cq_n, ckv_n, kr = _mla_latents(x_ref, g_ref, win_ref, gq_ref, gkv_ref, cos_ref, sin_ref,
                                   ckv_ref, kr_ref)
    cos, sin = cos_ref[...], sin_ref[...]
    kr_b = kr.astype(BF16)
    ckv_b = ckv_n.astype(BF16)
    for hh in range(H_C):
        qh = jnp.dot(cq_n, wuq_ref[:, hh * HEAD_W:(hh + 1) * HEAD_W],
                     preferred_element_type=F32) * scale
        q_ref[hh, :, 0:NOPE] = qh[:, 0:NOPE].astype(BF16)
        q_ref[hh, :, NOPE:HEAD_W] = _rope(qh[:, NOPE:HEAD_W], cos, sin).astype(BF16)
        kvh = jnp.dot(ckv_b, wukv_ref[:, hh * HEAD_W:(hh + 1) * HEAD_W],
                      preferred_element_type=F32)
        k_ref[hh, :, 0:NOPE] = kvh[:, 0:NOPE].astype(BF16)
        k_ref[hh, :, NOPE:HEAD_W] = kr_b
        v_ref[hh] = kvh[:, NOPE:HEAD_W].astype(BF16)


def _mla_proj_prompt(x, g, w_in, g_q, g_kv, w_uq, w_ukv, cos, sin, seq, tm, scale):
    m, d = x.shape
    nt = seq // tm
    const2 = lambda shape: pl.BlockSpec(shape, lambda i: (0, 0))
    return pl.pallas_call(
        functools.partial(_mla_proj_prompt_kernel, scale=scale),
        grid=(m // tm,),
        in_specs=[pl.BlockSpec((tm, d), lambda i: (i, 0)), const2((1, d)), const2(w_in.shape),
                  const2((1, Q_LORA)), const2((1, KV_LORA)), const2(w_uq.shape), const2(w_ukv.shape),
                  pl.BlockSpec((tm, LANES), lambda i: (i % nt, 0)),
                  pl.BlockSpec((tm, LANES), lambda i: (i % nt, 0))],
        out_specs=[pl.BlockSpec((H_C, tm, HEAD_W), lambda i: (0, i, 0)),
                   pl.BlockSpec((H_C, tm, HEAD_W), lambda i: (0, i, 0)),
                   pl.BlockSpec((H_C, tm, V_C), lambda i: (0, i, 0)),
                   pl.BlockSpec((tm, KV_LORA), lambda i: (i, 0)),
                   pl.BlockSpec((tm, ROPE), lambda i: (i, 0))],
        out_shape=[jax.ShapeDtypeStruct((H_C, m, HEAD_W), BF16),
                   jax.ShapeDtypeStruct((H_C, m, HEAD_W), BF16),
                   jax.ShapeDtypeStruct((H_C, m, V_C), BF16),
                   jax.ShapeDtypeStruct((m, KV_LORA), F32),
                   jax.ShapeDtypeStruct((m, ROPE), F32)],
        compiler_params=_params(1),
        name="mla_proj_prompt",
    )(x, g, w_in, g_q, g_kv, w_uq, w_ukv, cos, sin)


def _mla_proj_sample_kernel(x_ref, g_ref, win_ref, gq_ref, gkv_ref, wuq_ref, wukt_ref, cos_ref,
                            sin_ref, qlat_ref, qr_ref, ckv_ref, kr_ref, *, scale):
    cq_n, _, _ = _mla_latents(x_ref, g_ref, win_ref, gq_ref, gkv_ref, cos_ref, sin_ref,
                              ckv_ref, kr_ref)
    cos, sin = cos_ref[...], sin_ref[...]
    for hh in range(H_C):
        qh = jnp.dot(cq_n, wuq_ref[:, hh * HEAD_W:(hh + 1) * HEAD_W],
                     preferred_element_type=F32)
        q_lat = jnp.dot(qh[:, 0:NOPE].astype(BF16), wukt_ref[hh], preferred_element_type=F32)
        qlat_ref[hh] = (q_lat * scale).astype(BF16)
        qr_ref[hh] = (_rope(qh[:, NOPE:HEAD_W], cos, sin) * scale)[:, 0:ROPE].astype(BF16)


def _mla_proj_sample(x, g, w_in, g_q, g_kv, w_uq, w_ukt, cos, sin, scale):
    m, d = x.shape
    const = lambda shape: pl.BlockSpec(shape, lambda i: (0,) * len(shape))
    return pl.pallas_call(
        functools.partial(_mla_proj_sample_kernel, scale=scale),
        grid=(1,),
        in_specs=[const((m, d)), const((1, d)), const(w_in.shape), const((1, Q_LORA)),
                  const((1, KV_LORA)), const(w_uq.shape), const(w_ukt.shape),
                  const((m, LANES)), const((m, LANES))],
        out_specs=[const((H_C, m, KV_LORA)), const((H_C, m, ROPE)), const((m, KV_LORA)),
                   const((m, ROPE))],
        out_shape=[jax.ShapeDtypeStruct((H_C, m, KV_LORA), BF16),
                   jax.ShapeDtypeStruct((H_C, m, ROPE), BF16),
                   jax.ShapeDtypeStruct((m, KV_LORA), F32),
                   jax.ShapeDtypeStruct((m, ROPE), F32)],
        compiler_params=_params(1),
        name="mla_proj_sample",
    )(x, g, w_in, g_q, g_kv, w_uq, w_ukt, cos, sin)


def _flash_kernel(q_ref, k_ref, v_ref, o_ref, m_s, acc_s, *, tile, n_tiles):
    t = tile
    row = lax.broadcasted_iota(jnp.int32, (t, t), 0)
    col = lax.broadcasted_iota(jnp.int32, (t, t), 1)
    visible = (col // CHUNK) <= (row // CHUNK)
    ones = jnp.ones((t, V_C), BF16)
    for j in range(n_tiles):
        ks = slice(j * t, (j + 1) * t)
        kj = k_ref[0, ks, :]
        vj = jnp.concatenate([v_ref[0, ks, :], ones], axis=1)
        for i in range(j, n_tiles):
            qs = slice(i * t, (i + 1) * t)
            s = _nt_dot(q_ref[0, qs, :], kj)
            if i == j:
                s = jnp.where(visible, s, -jnp.inf)
            m_new = jnp.broadcast_to(jnp.max(s, axis=-1, keepdims=True), (t, LANES))
            if j > 0:
                m_prev = m_s[qs]
                m_new = jnp.maximum(m_prev, m_new)
            p = jnp.exp2(s - jnp.concatenate([m_new] * (t // LANES), axis=1))
            pv = jnp.dot(p.astype(BF16), vj, preferred_element_type=F32)
            if j > 0:
                alpha = jnp.exp2(m_prev - m_new)
                pv = pv + jnp.concatenate([alpha, alpha], axis=1) * acc_s[qs]
            if i == j:
                o_ref[qs, :] = (pv[:, :V_C] / pv[:, V_C:]).astype(o_ref.dtype)
            else:
                m_s[qs] = m_new
                acc_s[qs] = pv


def _flash_prompt(q, k, v, batch, seq, tile):
    m = batch * seq
    return pl.pallas_call(
        functools.partial(_flash_kernel, tile=tile, n_tiles=seq // tile),
        grid=(batch, H_C),
        in_specs=[pl.BlockSpec((1, seq, HEAD_W), lambda b, h: (h, b, 0)),
                  pl.BlockSpec((1, seq, HEAD_W), lambda b, h: (h, b, 0)),
                  pl.BlockSpec((1, seq, V_C), lambda b, h: (h, b, 0))],
        out_specs=pl.BlockSpec((seq, V_C), lambda b, h: (b, h)),
        out_shape=jax.ShapeDtypeStruct((m, H_C * V_C), BF16),
        scratch_shapes=[pltpu.VMEM((seq, LANES), F32), pltpu.VMEM((seq, 2 * V_C), F32)],
        compiler_params=_params(2),
        name="flash_prompt",
    )(q, k, v)


def _latent_attn_kernel(qlat_ref, qr_ref, cckv_ref, ckr_ref, nckv_ref, nkr_ref, wuv_ref, o_ref,
                        *, steps, new_rows, tk):
    rows = H_C * steps
    q_lat = qlat_ref[...].reshape(rows, KV_LORA)
    q_r = qr_ref[...].reshape(rows, ROPE)

    def update(state, ckv, kr, valid, kr_transposed):
        ckv_b = ckv.astype(BF16)
        if kr_transposed:
            s_r = jnp.dot(q_r, kr.astype(BF16), preferred_element_type=F32)
        else:
            s_r = _nt_dot(q_r, kr.astype(BF16))
        s = _nt_dot(q_lat, ckv_b) + s_r
        if valid is not None:
            col = lax.broadcasted_iota(jnp.int32, s.shape, 1)
            s = jnp.where(col < valid, s, -jnp.inf)
        m_blk = jnp.max(s, axis=-1, keepdims=True)
        if state is None:
            m_new = m_blk
            p = jnp.exp(s - m_new)
            return (m_new, jnp.sum(p, axis=-1, keepdims=True),
                    jnp.dot(p.astype(BF16), ckv_b, preferred_element_type=F32))
        m_prev, l_prev, acc = state
        m_new = jnp.maximum(m_prev, m_blk)
        p = jnp.exp(s - m_new)
        alpha = jnp.exp(m_prev - m_new)
        return (m_new, alpha * l_prev + jnp.sum(p, axis=-1, keepdims=True),
                alpha * acc + jnp.dot(p.astype(BF16), ckv_b, preferred_element_type=F32))

    state = None
    for j in range(cckv_ref.shape[1] // tk):
        ks = slice(j * tk, (j + 1) * tk)
        state = update(state, cckv_ref[0, ks, :], ckr_ref[0, :, ks], None, True)
    _, l_fin, acc = update(state, nckv_ref[0], nkr_ref[0], new_rows, False)
    o_lat = (acc / l_fin).astype(BF16)
    for hh in range(H_C):
        o_ref[:, hh * V_C:(hh + 1) * V_C] = jnp.dot(
            o_lat[hh * steps:(hh + 1) * steps], wuv_ref[hh],
            preferred_element_type=F32).astype(o_ref.dtype)


def _latent_attn_sample(q_lat, q_r, cache_ckv, cache_kr, new_ckv, new_kr, w_uv, steps, tk):
    batch, past, _ = cache_ckv.shape
    pad_rows = new_ckv.shape[1]
    return pl.pallas_call(
        functools.partial(_latent_attn_kernel, steps=steps, new_rows=steps, tk=tk),
        grid=(batch,),
        in_specs=[pl.BlockSpec((H_C, steps, KV_LORA), lambda b: (0, b, 0)),
                  pl.BlockSpec((H_C, steps, ROPE), lambda b: (0, b, 0)),
                  pl.BlockSpec((1, past, KV_LORA), lambda b: (b, 0, 0)),
                  pl.BlockSpec((1, ROPE, past), lambda b: (b, 0, 0)),
                  pl.BlockSpec((1, pad_rows, KV_LORA), lambda b: (b, 0, 0)),
                  pl.BlockSpec((1, pad_rows, ROPE), lambda b: (b, 0, 0)),
                  pl.BlockSpec((H_C, KV_LORA, V_C), lambda b: (0, 0, 0))],
        out_specs=pl.BlockSpec((steps, H_C * V_C), lambda b: (b, 0)),
        out_shape=jax.ShapeDtypeStruct((batch * steps, H_C * V_C), BF16),
        compiler_params=_params(1),
        name="latent_attn_sample",
    )(q_lat, q_r, cache_ckv, cache_kr, new_ckv, new_kr, w_uv)


def _rope_tables(pos, width):
    half = ROPE // 2
    freqs = ROPE_THETA ** (-jnp.arange(half, dtype=F32) / half)
    ang = pos.astype(F32)[:, None] * freqs[None, :]
    cos, sin = jnp.cos(ang), jnp.sin(ang)
    cos_h = jnp.concatenate([cos, cos], axis=-1)
    sin_h = jnp.concatenate([-sin, sin], axis=-1)
    reps = width // ROPE
    return jnp.tile(cos_h, (1, reps)), jnp.tile(sin_h, (1, reps))


def _prep_even(w_in_ab, w_gate_up):
    sizes = [H_A * DK_A, H_A * DK_A, H_A * DV_A, H_A * DV_A, GATE_RANK,
             H_B * DK_B, H_B * DK_B, H_B * DV_B, H_B * DV_B]
    offs = np.concatenate([[0], np.cumsum(sizes)])
    parts = [w_in_ab[:, offs[n]:offs[n + 1]] for n in range(9)]
    lo = jnp.pad(parts[4], ((0, 0), (0, LANES - GATE_RANK)))
    w = jnp.concatenate(parts[0:4] + parts[5:9] + [lo], axis=1).astype(BF16)
    w_gu = jnp.pad(w_gate_up, ((0, LANES - GATE_RANK), (0, 0))).astype(BF16)
    return w, w_gu


def _prep_odd(w_in_c, w_uq, w_uk, w_uv):
    d = w_in_c.shape[0]
    w_in = jnp.pad(w_in_c, ((0, 0), (0, LANES - ROPE))).astype(BF16)
    uq = w_uq.reshape(Q_LORA, H_C, NOPE + ROPE)
    uq = jnp.pad(uq, ((0, 0), (0, 0), (0, HEAD_W - NOPE - ROPE)))
    w_uq_p = uq.reshape(Q_LORA, H_C * HEAD_W).astype(BF16)
    w_ukv = jnp.concatenate([w_uk, w_uv], axis=2).reshape(KV_LORA, H_C * HEAD_W).astype(BF16)
    w_ukt = jnp.transpose(w_uk, (1, 2, 0)).astype(BF16)
    w_uvh = jnp.transpose(w_uv, (1, 0, 2)).astype(BF16)
    del d
    return w_in, w_uq_p, w_ukv, w_ukt, w_uvh


def _log_gamma_row():
    lg = np.log1p(-np.exp2(-5.0 - np.arange(H_B, dtype=np.float32))).astype(np.float32)
    return jnp.asarray(np.repeat(lg, DK_B)[None, :])


def _to_time_major(a, batch, steps):
    return a.reshape(batch, steps, -1).transpose(1, 0, 2).reshape(batch * steps, -1)


def _to_batch_major(a, batch, steps):
    return a.reshape(steps, batch, -1).transpose(1, 0, 2).reshape(batch * steps, -1)


def _divisor_tile(n, pref):
    t = min(n, pref)
    while n % t:
        t //= 2
    return t


def _trunk(x, pos0, gla0, ret0, ckv_past, kr_past, conv0, w, time_major_ffn):
    batch, seq, d = x.shape
    m = batch * seq
    pos = pos0 + jnp.arange(seq, dtype=jnp.int32)
    row = lambda v: v.reshape(1, -1)
    scale = float((NOPE + ROPE) ** -0.5)
    tf = TF_FFN
    tm_proj = _divisor_tile(m, TM_PROJ)

    if time_major_ffn:
        ffn_kw = dict(tm=m, tf=tf, shift=batch, tiles_per_seq=1)
        to_ffn = lambda a: _to_time_major(a, batch, seq)
        from_ffn = lambda a: _to_batch_major(a, batch, seq)
        prep_state = lambda s: s.transpose(1, 0, 2).reshape(1, 2 * batch, D_FF)
        post_state = lambda s: s.reshape(2, batch, D_FF).transpose(1, 0, 2)
    else:
        tm_ffn = _divisor_tile(seq, TM_FFN)
        ffn_kw = dict(tm=tm_ffn, tf=tf, shift=1, tiles_per_seq=seq // tm_ffn)
        to_ffn = from_ffn = lambda a: a
        prep_state = lambda s: jnp.pad(s, ((0, 0), (SUBLANES - 2, 0), (0, 0)))
        post_state = lambda s: s

    xf = x.reshape(m, d)
    conv_new = []
    gla_new = ret_new = ckv_new = kr_new = None
    for layer in range(2):
        g_mix = row(w['norm_mix'][layer])
        if layer == 0:
            w_ab, w_gu = _prep_even(w['w_in_ab'][0], w['w_gate_up'][0])
            slab = _norm_matmul(xf, g_mix, w_ab, tm_proj, 4 * HEAD_W)
            cos, sin = _rope_tables(pos, 4 * DK_B)
            rows = _divisor_tile(seq, ROWS_LINATTN)
            mix, gla_new, ret_new = _linattn(
                slab, batch, seq, rows, w_gu, row(w['b_gate'][0]), row(w['g_gla'][0]),
                row(w['g_ret'][0]), cos, sin, _log_gamma_row(),
                None if gla0 is None else gla0[0], None if ret0 is None else ret0[0])
            w_o = w['w_out_ab'][0].astype(BF16)
        else:
            w_in, w_uq_p, w_ukv, w_ukt, w_uvh = _prep_odd(w['w_in_c'][0], w['w_uq'][0], w['w_uk'][0],
                                                          w['w_uv'][0])
            cos, sin = _rope_tables(pos, LANES)
            g_q, g_kv = row(w['g_q'][0]), row(w['g_kv'][0])
            if ckv_past is None:
                q, k, v, ckv_new, kr_new = _mla_proj_prompt(
                    xf, g_mix, w_in, g_q, g_kv, w_uq_p, w_ukv, cos, sin, seq, tm_proj,
                    scale * float(np.log2(np.e)))
                mix = _flash_prompt(q, k, v, batch, seq, _divisor_tile(seq, TQ_FLASH))
            else:
                assert ckv_past.shape[2] % CHUNK == 0 and seq <= CHUNK
                cos_m, sin_m = jnp.tile(cos, (batch, 1)), jnp.tile(sin, (batch, 1))
                q_lat, q_r, ckv_new, kr_new = _mla_proj_sample(
                    xf, g_mix, w_in, g_q, g_kv, w_uq_p, w_ukt, cos_m, sin_m, scale)
                pad = ((0, 0), (0, LANES - seq), (0, 0))
                new_ckv = jnp.pad(ckv_new.reshape(batch, seq, KV_LORA), pad)
                new_kr = jnp.pad(kr_new.reshape(batch, seq, ROPE), pad)
                mix = _latent_attn_sample(q_lat, q_r, ckv_past[0],
                                          jnp.swapaxes(kr_past[0], 1, 2), new_ckv, new_kr,
                                          w_uvh, seq, _divisor_tile(ckv_past.shape[2], TK_LATENT))
            w_o = w['w_out_c'][0].astype(BF16)

        if conv0 is None:
            state = jnp.zeros((batch, SUBLANES, D_FF), F32) if not time_major_ffn else \
                jnp.zeros((1, 2 * batch, D_FF), F32)
        else:
            state = prep_state(conv0[layer])
        y, conv_rows = _ffn(to_ffn(xf), to_ffn(mix), w_o, row(w['norm_ffn'][layer]),
                            w['w_ffn_in'].astype(BF16), w['w_dwconv'][layer],
                            row(w['b_dwconv'][layer]),
                            (0.5 * w['w_ffn_out'][layer]).astype(BF16), state,
                            row(w['norm_final']), layer=layer, final_norm=(layer == 1), **ffn_kw)
        xf = from_ffn(y)
        tps = ffn_kw['tiles_per_seq']
        conv_new.append(post_state(conv_rows[tps - 1::tps]))

    return (xf.reshape(batch, seq, d), gla_new[None], ret_new[None],
            ckv_new.reshape(1, batch, seq, KV_LORA), kr_new.reshape(1, batch, seq, ROPE),
            jnp.stack(conv_new))


def kernel(x_prompt, x_sample, state_gla, state_ret, cache_ckv, cache_krope, state_conv, norm_mix, norm_ffn, norm_final, w_in_ab, w_gate_up, b_gate, g_gla, g_ret, w_out_ab, w_in_c, g_q, g_kv, w_uq, w_uk, w_uv, w_out_c, w_ffn_in, w_dwconv, b_dwconv, w_ffn_out):
    w = {'norm_mix': norm_mix, 'norm_ffn': norm_ffn, 'norm_final': norm_final,
         'w_in_ab': w_in_ab, 'w_gate_up': w_gate_up, 'b_gate': b_gate, 'g_gla': g_gla,
         'g_ret': g_ret, 'w_out_ab': w_out_ab, 'w_in_c': w_in_c, 'g_q': g_q, 'g_kv': g_kv,
         'w_uq': w_uq, 'w_uk': w_uk, 'w_uv': w_uv, 'w_out_c': w_out_c, 'w_ffn_in': w_ffn_in,
         'w_dwconv': w_dwconv, 'b_dwconv': b_dwconv, 'w_ffn_out': w_ffn_out}
    past_len = cache_ckv.shape[2]
    y_p, gla_p, ret_p, ckv_p, kr_p, conv_p = _trunk(
        x_prompt, 0, None, None, None, None, None, w, time_major_ffn=False)
    y_s, gla_s, ret_s, ckv_s, kr_s, conv_s = _trunk(
        x_sample, past_len, state_gla, state_ret, cache_ckv, cache_krope, state_conv, w,
        time_major_ffn=True)
    return (y_p, y_s, gla_p, gla_s, ret_p, ret_s, ckv_p, ckv_s, kr_p, kr_s, conv_p, conv_s)
```

```python
import functools

import numpy as np
import jax
import jax.numpy as jnp
from jax import lax
from jax.experimental import pallas as pl
from jax.experimental.pallas import tpu as pltpu

F32 = jnp.float32
BF16 = jnp.bfloat16

D_MODEL = 1024
CHUNK = 64
EPS = 1e-6
ROPE_THETA = 10000.0
H_A, DK_A, DV_A = 4, 64, 128
GATE_RANK = 16
GATE_TAU = 16.0
H_B, DK_B, DV_B = 4, 64, 128
H_C = 8
Q_LORA, KV_LORA, NOPE, ROPE, V_C = 384, 512, 128, 64, 128
D_FF = 2816
CONV_W = 3

LANES = 128
SUBLANES = 8
VMEM_LIMIT = 56 * 1024 * 1024
AB_COLS = 3200
HEAD_W = 2 * LANES
TM_PROJ = 512
TM_FFN = 1024
TF_FFN = D_FF
SUB_FFN = 256
ROWS_LINATTN = 256
TQ_FLASH = 256
TK_LATENT = 1024


def _params(n_axes):
    return pltpu.CompilerParams(dimension_semantics=("arbitrary",) * n_axes,
                                vmem_limit_bytes=VMEM_LIMIT)


def _rms(x, g):
    return x * lax.rsqrt(jnp.mean(x * x, axis=-1, keepdims=True) + EPS) * g


def _rope_first_half(shape):
    lane = lax.broadcasted_iota(jnp.int32, shape, 1)
    return (lane % ROPE) < ROPE // 2


def _rope(x, cos, sin_signed, first=None):
    w = x.shape[1]
    half = ROPE // 2
    if first is None:
        first = _rope_first_half(x.shape)
    swapped = jnp.where(first, pltpu.roll(x, w - half, 1), pltpu.roll(x, half, 1))
    return x * cos + swapped * sin_signed


def _nt_dot(a, b):
    return lax.dot_general(a, b, (((1,), (1,)), ((), ())), preferred_element_type=F32)


def _norm_matmul_kernel(x_ref, g_ref, w_ref, o_ref, *, col_chunk):
    h = _rms(x_ref[...], g_ref[...]).astype(BF16)
    n = w_ref.shape[1]
    for c0 in range(0, n, col_chunk):
        c1 = min(c0 + col_chunk, n)
        o_ref[:, c0:c1] = jnp.dot(h, w_ref[:, c0:c1], preferred_element_type=F32).astype(o_ref.dtype)


def _norm_matmul(x, g, w, tm, col_chunk):
    m, d = x.shape
    n = w.shape[1]
    return pl.pallas_call(
        functools.partial(_norm_matmul_kernel, col_chunk=col_chunk),
        grid=(m // tm,),
        in_specs=[pl.BlockSpec((tm, d), lambda i: (i, 0)),
                  pl.BlockSpec((1, d), lambda i: (0, 0)),
                  pl.BlockSpec((d, n), lambda i: (0, 0))],
        out_specs=pl.BlockSpec((tm, n), lambda i: (i, 0)),
        out_shape=jax.ShapeDtypeStruct((m, n), BF16),
        compiler_params=_params(1),
        name="norm_inproj_ab",
    )(x, g, w)


def _decay_attention_streams(streams, mask_k, mask_v, n_chunks, mid, last, between):
    c = CHUNK
    assert 2 * c == LANES
    hk, hv = 4 * DK_A, 4 * DV_A
    rt = lax.broadcasted_iota(jnp.int32, (c, 4 * c), 0)
    ct = lax.broadcasted_iota(jnp.int32, (c, 4 * c), 1)
    causal = (ct % c) <= rt
    lane = lax.broadcasted_iota(jnp.int32, (DK_A, LANES), 1)
    zero_v = jnp.zeros((LANES - c, hv), BF16)
    zero_blk = jnp.zeros((DK_A, DV_A), BF16)
    log2e = np.float32(np.log2(np.e))

    items = []
    for q, k, v, bcum, s_ref in streams:
        bcum2 = bcum * log2e
        for n in range(n_chunks):
            sl = slice(n * c, (n + 1) * c)
            qn, kn, vn, bn = q[sl], k[sl], v[sl], bcum2[sl]
            b_ref = bn[mid:mid + 1]
            b_last = bn[last:last + 1]
            it = dict(s_ref=s_ref, vn=vn)
            it['q_rel'] = (qn * jnp.exp2(bn - b_ref)).astype(BF16)
            k_rel = (kn * jnp.exp2(b_ref - bn)).astype(BF16)
            it['k_bd'] = jnp.concatenate([k_rel] * 4, axis=0) * mask_k
            it['q_dec'] = (qn * jnp.exp2(bn)).astype(BF16)
            k_dec = kn * jnp.exp2(b_last - bn)
            decay_rows = jnp.broadcast_to(jnp.exp2(b_last), (LANES - c, hk))
            it['kt'] = jnp.concatenate([k_dec, decay_rows], axis=0).T
            items.append(it)
    between()
    for it in items:
        it['scores'] = jnp.where(causal, _nt_dot(it['q_rel'], it['k_bd']), 0.0).astype(BF16)
    between()
    for it in items:
        v_bd = jnp.concatenate([it['vn']] * 4, axis=0) * mask_v
        it['o'] = jnp.dot(it['scores'], v_bd, preferred_element_type=F32)
        kt_b = it['kt'].astype(BF16)
        v_pad = jnp.concatenate([it['vn'], zero_v], axis=0)
        it['kv'] = [jnp.dot(kt_b[h * DK_A:(h + 1) * DK_A], v_pad[:, h * DV_A:(h + 1) * DV_A],
                            preferred_element_type=F32) for h in range(4)]
    between()
    for it in items:
        s_ref = it['s_ref']
        s_old = [s_ref[h] for h in range(4)]
        it['s_bd'] = jnp.concatenate(
            [jnp.concatenate([zero_blk] * h + [s_old[h].astype(BF16)] + [zero_blk] * (3 - h), axis=1)
             for h in range(4)], axis=0)
        for h in range(4):
            kth = it['kt'][h * DK_A:(h + 1) * DK_A]
            decay = jnp.where(lane < c, pltpu.roll(kth, c, 1), kth)
            s_ref[h] = s_old[h] * decay + it['kv'][h]
    between()
    outs = []
    for si in range(len(streams)):
        chunk_out = [it['o'] + jnp.dot(it['q_dec'], it['s_bd'], preferred_element_type=F32)
                     for it in items[si * n_chunks:(si + 1) * n_chunks]]
        outs.append(chunk_out[0] if n_chunks == 1 else jnp.concatenate(chunk_out, axis=0))
    between()
    return outs


_SLAB = dict(qa=(0, 256), ka=(256, 256), va=(512, 512), ra=(1024, 512),
             qb=(1536, 256), kb=(1792, 256), vb=(2048, 512), gb=(2560, 512), lo=(3072, LANES))


def _linattn_compute(cols, wgu_ref, bg_ref, gg_ref, gr_ref, cos_ref, sin_ref, lgam_ref, tri_ref,
                     mk_ref, mv_ref, mix_ref, sa, sb, rows, between=lambda: None):
    c = CHUNK
    padded = max(rows, c)
    n_chunks = padded // c
    valid = min(rows, c)
    mid, last = (valid - 1) // 2, valid - 1

    def pad_rows(x):
        if padded == rows:
            return x
        return jnp.concatenate([x, jnp.zeros((padded - rows, x.shape[1]), x.dtype)], axis=0)

    tri = tri_ref[...]
    mask_k, mask_v = mk_ref[...], mv_ref[...]

    gate = jnp.dot(cols['lo'], wgu_ref[...], preferred_element_type=F32) + bg_ref[...]
    log_a = (jnp.minimum(gate, 0.0) - jnp.log1p(jnp.exp(-jnp.abs(gate)))) / GATE_TAU
    log_a = pad_rows(log_a)
    hi = log_a.astype(BF16)
    lo = (log_a - hi.astype(F32)).astype(BF16)
    bcum_a = (jnp.dot(tri, hi, preferred_element_type=F32)
              + jnp.dot(tri, lo, preferred_element_type=F32))
    qa = pad_rows(cols['qa'].astype(F32) * DK_A ** -0.5)
    ka = pad_rows(cols['ka'].astype(F32))
    va = pad_rows(cols['va'])
    between()

    cos, sin = cos_ref[...], sin_ref[...]
    first = _rope_first_half(cos.shape)
    qb = pad_rows(_rope(cols['qb'].astype(F32), cos, sin, first))
    kb = pad_rows(_rope(cols['kb'].astype(F32), cos, sin, first) * DK_B ** -0.5)
    vb = pad_rows(cols['vb'])
    pos = lax.broadcasted_iota(jnp.int32, (padded, 4 * DK_B), 0) % c
    steps = jnp.where(pos < valid, pos + 1, valid).astype(F32)
    bcum_b = steps * lgam_ref[...]
    o_a, o_b = _decay_attention_streams(
        [(qa, ka, va, bcum_a, sa), (qb, kb, vb, bcum_b, sb)], mask_k, mask_v, n_chunks, mid, last,
        between)
    o_a, o_b = o_a[:rows], o_b[:rows]

    gg, gr = gg_ref[...], gr_ref[...]
    for h in range(4):
        sl = slice(h * DV_A, (h + 1) * DV_A)
        oh = o_a[:, sl]
        oh = oh * lax.rsqrt(jnp.mean(oh * oh, axis=-1, keepdims=True) + EPS) * gg[:, sl]
        r = cols['ra'][:, sl].astype(F32)
        mix_ref[:, sl] = (oh * (r * jax.nn.sigmoid(r))).astype(mix_ref.dtype)
        ob = o_b[:, sl]
        ob = ob - jnp.mean(ob, axis=-1, keepdims=True)
        ob = ob * lax.rsqrt(jnp.mean(ob * ob, axis=-1, keepdims=True) + EPS) * gr[:, sl]
        gb = cols['gb'][:, sl].astype(F32)
        mix_ref[:, 4 * DV_A + h * DV_B:4 * DV_A + (h + 1) * DV_B] = (
            ob * (gb * jax.nn.sigmoid(gb))).astype(mix_ref.dtype)


def _linattn_kernel(qa_ref, ka_ref, va_ref, ra_ref, qb_ref, kb_ref, vb_ref, gb_ref, lo_ref,
                    wgu_ref, bg_ref, gg_ref, gr_ref, cos_ref, sin_ref, lgam_ref, tri_ref, mk_ref,
                    mv_ref, *rest, rows, has_state):
    if has_state:
        s0a_ref, s0b_ref, mix_ref, sa_out, sb_out, sa, sb = rest
    else:
        mix_ref, sa_out, sb_out, sa, sb = rest
    t = pl.program_id(1)

    @pl.when(t == 0)
    def _init():
        if has_state:
            sa[...] = s0a_ref[0]
            sb[...] = s0b_ref[0]
        else:
            sa[...] = jnp.zeros_like(sa)
            sb[...] = jnp.zeros_like(sb)

    cols = dict(qa=qa_ref[...], ka=ka_ref[...], va=va_ref[...], ra=ra_ref[...], qb=qb_ref[...],
                kb=kb_ref[...], vb=vb_ref[...], gb=gb_ref[...], lo=lo_ref[...])
    _linattn_compute(cols, wgu_ref, bg_ref, gg_ref, gr_ref, cos_ref, sin_ref, lgam_ref, tri_ref,
                     mk_ref, mv_ref, mix_ref, sa, sb, rows)

    @pl.when(t == pl.num_programs(1) - 1)
    def _emit_state():
        sa_out[0] = sa[...]
        sb_out[0] = sb[...]


def _proj_linattn_kernel(x_ref, g_ref, w_ref, wgu_ref, bg_ref, gg_ref, gr_ref, cos_ref, sin_ref,
                         lgam_ref, tri_ref, mk_ref, mv_ref, mix_ref, sa_out, sb_out, slab_even,
                         slab_odd, sa, sb, *, rows, nt, col_chunk):
    s = pl.program_id(0)
    t = jnp.maximum(s - 1, 0) % nt

    @pl.when(s == 0)
    def _zero_slab():
        slab_odd[...] = jnp.zeros_like(slab_odd)

    @pl.when(t == 0)
    def _init():
        sa[...] = jnp.zeros_like(sa)
        sb[...] = jnp.zeros_like(sb)

    def step(write_ref, read_ref):
        cols = {name: read_ref[:, off:off + width] for name, (off, width) in _SLAB.items()}
        h = _rms(x_ref[...], g_ref[...]).astype(BF16)
        n = w_ref.shape[1]
        pieces = list(range(0, n, col_chunk))
        shares = (5, 1, 0, 1, 1, 5)
        total = len(pieces)
        quotas = [total * sum(shares[:k + 1]) // sum(shares) - total * sum(shares[:k]) // sum(shares)
                  for k in range(len(shares))]

        def project(count):
            for _ in range(min(count, len(pieces))):
                c0 = pieces.pop(0)
                c1 = min(c0 + col_chunk, n)
                write_ref[:, c0:c1] = jnp.dot(h, w_ref[:, c0:c1],
                                              preferred_element_type=F32).astype(write_ref.dtype)

        _linattn_compute(cols, wgu_ref, bg_ref, gg_ref, gr_ref, cos_ref, sin_ref, lgam_ref,
                         tri_ref, mk_ref, mv_ref, mix_ref, sa, sb, rows,
                         between=lambda: project(quotas.pop(0) if quotas else 0))
        project(len(pieces))

    @pl.when(s % 2 == 0)
    def _even():
        step(slab_even, slab_odd)

    @pl.when(s % 2 == 1)
    def _odd():
        step(slab_odd, slab_even)

    @pl.when(t == nt - 1)
    def _emit_state():
        sa_out[0] = sa[...]
        sb_out[0] = sb[...]


def _linattn_masks(rows):
    c = CHUNK
    padded = max(rows, c)
    r = np.arange(padded)
    tri = jnp.asarray(((r[:, None] // c == r[None, :] // c) & (r[None, :] <= r[:, None])), BF16)
    r4 = np.arange(4 * c)
    mask_k = jnp.asarray(r4[:, None] // c == np.arange(4 * DK_A)[None, :] // DK_A, BF16)
    mask_v = jnp.asarray(r4[:, None] // c == np.arange(4 * DV_A)[None, :] // DV_A, BF16)
    return tri, mask_k, mask_v


def _proj_linattn(x, g, w_ab, batch, seq, rows, w_gu, b_gate, g_gla, g_ret, cos, sin, lgam):
    m, d = x.shape
    nt = seq // rows
    n_tiles = batch * nt
    tri, mask_k, mask_v = _linattn_masks(rows)
    hk, hv = 4 * DK_A, 4 * DV_A
    prev = lambda s: jnp.maximum(s - 1, 0)

    def const(shape):
        return pl.BlockSpec(shape, lambda s: (0,) * len(shape))

    state_spec = pl.BlockSpec((1, 4, DK_A, DV_A), lambda s: (prev(s) // nt, 0, 0, 0))
    state_shape = jax.ShapeDtypeStruct((batch, 4, DK_A, DV_A), F32)
    return pl.pallas_call(
        functools.partial(_proj_linattn_kernel, rows=rows, nt=nt, col_chunk=HEAD_W),
        grid=(n_tiles + 1,),
        in_specs=[pl.BlockSpec((rows, d), lambda s: (jnp.minimum(s, n_tiles - 1), 0)),
                  const((1, d)), const(w_ab.shape),
                  const((LANES, hk)), const((1, hk)), const((1, hv)), const((1, hv)),
                  pl.BlockSpec((rows, hk), lambda s: (prev(s) % nt, 0)),
                  pl.BlockSpec((rows, hk), lambda s: (prev(s) % nt, 0)),
                  const((1, hk)), const(tri.shape), const(mask_k.shape), const(mask_v.shape)],
        out_specs=[pl.BlockSpec((rows, 2 * hv), lambda s: (prev(s), 0)), state_spec, state_spec],
        out_shape=[jax.ShapeDtypeStruct((m, 2 * hv), BF16), state_shape, state_shape],
        scratch_shapes=[pltpu.VMEM((rows, AB_COLS), BF16), pltpu.VMEM((rows, AB_COLS), BF16),
                        pltpu.VMEM((4, DK_A, DV_A), F32), pltpu.VMEM((4, DK_B, DV_B), F32)],
        compiler_params=_params(1),
        name="inproj_decay_linear_attention",
    )(x, g, w_ab, w_gu, b_gate, g_gla, g_ret, cos, sin, lgam, tri, mask_k, mask_v)


def _linattn(slab, batch, seq, rows, w_gu, b_gate, g_gla, g_ret, cos, sin, lgam, s0a, s0b):
    nt = seq // rows
    has_state = s0a is not None
    tri, mask_k, mask_v = _linattn_masks(rows)

    def col(width, idx):
        return pl.BlockSpec((rows, width), lambda b, t, idx=idx: (b * nt + t, idx))

    def const(shape):
        return pl.BlockSpec(shape, lambda b, t: (0,) * len(shape))

    hk = 4 * DK_A
    hv = 4 * DV_A
    in_specs = [col(hk, 0), col(hk, 1), col(hv, 1), col(hv, 2),
                col(hk, 6), col(hk, 7), col(hv, 4), col(hv, 5),
                col(LANES, 24),
                const((LANES, hk)), const((1, hk)), const((1, hv)), const((1, hv)),
                pl.BlockSpec((rows, hk), lambda b, t: (t, 0)),
                pl.BlockSpec((rows, hk), lambda b, t: (t, 0)),
                const((1, hk)), const(tri.shape), const(mask_k.shape), const(mask_v.shape)]
    args = [slab] * 9 + [w_gu, b_gate, g_gla, g_ret, cos, sin, lgam, tri, mask_k, mask_v]
    state_spec = pl.BlockSpec((1, 4, DK_A, DV_A), lambda b, t: (b, 0, 0, 0))
    if has_state:
        in_specs += [state_spec, state_spec]
        args += [s0a, s0b]
    state_shape = jax.ShapeDtypeStruct((batch, 4, DK_A, DV_A), F32)
    return pl.pallas_call(
        functools.partial(_linattn_kernel, rows=rows, has_state=has_state),
        grid=(batch, nt),
        in_specs=in_specs,
        out_specs=[pl.BlockSpec((rows, 2 * hv), lambda b, t: (b * nt + t, 0)), state_spec, state_spec],
        out_shape=[jax.ShapeDtypeStruct((batch * seq, 2 * hv), BF16), state_shape, state_shape],
        scratch_shapes=[pltpu.VMEM((4, DK_A, DV_A), F32), pltpu.VMEM((4, DK_B, DV_B), F32)],
        compiler_params=_params(2),
        name="decay_linear_attention",
    )(*args)


def _ffn_kernel(x_ref, mix_ref, wo_ref, g_ref, wa_ref, wu_ref, wdw_ref, bdw_ref, wout_ref, st_ref,
                gf_ref, y_ref, cv_ref, h_s, carry_s,
                *, tm, sub, prev_rows, shift, tiles_per_seq, final_norm, single_step):
    i = pl.program_id(0)
    j = pl.program_id(1)
    p = prev_rows
    subs = [slice(r, r + sub) for r in range(0, tm, sub)]

    def mid_residual(rs):
        xm = x_ref[rs] + jnp.dot(mix_ref[rs], wo_ref[...], preferred_element_type=F32)
        return xm, _rms(xm, g_ref[...]).astype(BF16)

    if not single_step:
        @pl.when(j == 0)
        def _start():
            for rs in subs:
                y_ref[rs], h_s[rs] = mid_residual(rs)

    at_start = (i % tiles_per_seq) == 0
    tail = jnp.where(at_start, st_ref[0], carry_s[j])
    last = j == pl.num_programs(1) - 1
    wdw = wdw_ref[...]
    bdw = bdw_ref[...]
    a = None
    for rs in subs:
        if single_step:
            base, h = mid_residual(rs)
        else:
            base, h = y_ref[rs], h_s[rs]
        a = jnp.dot(h, wa_ref[...], preferred_element_type=F32)
        u = jnp.dot(h, wu_ref[...], preferred_element_type=F32)
        ext = jnp.concatenate([tail, a], axis=0)
        conv = bdw + pltpu.roll(ext, 2 * shift, 0)[p:] * wdw[0:1]
        conv = conv + pltpu.roll(ext, shift, 0)[p:] * wdw[1:2]
        conv = conv + a * wdw[2:3]
        act = conv * (1.0 + lax.erf(conv * np.float32(np.sqrt(0.5)))) * u
        out = base + jnp.dot(act.astype(BF16), wout_ref[...], preferred_element_type=F32)
        if final_norm:
            normed = _rms(out, gf_ref[...])
            out = normed if single_step else jnp.where(last, normed, out)
        y_ref[rs] = out
        tail = a[sub - p:]
    carry_s[j] = tail
    cv_ref[0] = a[sub - 2 * shift:]


def _ffn(x, mix, w_o, g_ffn, w_in, w_dw, b_dw, w_out, conv_state, g_final, *, layer, tm, tf, shift,
         tiles_per_seq, final_norm):
    m, d = x.shape
    n_f = D_FF // tf
    n_seq, prev_rows, _ = conv_state.shape
    kern = functools.partial(_ffn_kernel, tm=tm, sub=min(tm, SUB_FFN), prev_rows=prev_rows,
                             shift=shift, tiles_per_seq=tiles_per_seq, final_norm=final_norm,
                             single_step=(n_f == 1))
    h_rows = SUBLANES * 2 if n_f == 1 else tm
    return pl.pallas_call(
        kern,
        grid=(m // tm, n_f),
        in_specs=[pl.BlockSpec((tm, d), lambda i, j: (i, 0)),
                  pl.BlockSpec((tm, d), lambda i, j: (i, 0)),
                  pl.BlockSpec((d, d), lambda i, j: (0, 0)),
                  pl.BlockSpec((1, d), lambda i, j: (0, 0)),
                  pl.BlockSpec((None, d, tf), lambda i, j: (layer, 0, j)),
                  pl.BlockSpec((None, d, tf), lambda i, j: (layer, 0, n_f + j)),
                  pl.BlockSpec((CONV_W, tf), lambda i, j: (0, j)),
                  pl.BlockSpec((1, tf), lambda i, j: (0, j)),
                  pl.BlockSpec((tf, d), lambda i, j: (j, 0)),
                  pl.BlockSpec((1, prev_rows, tf), lambda i, j: (i // tiles_per_seq, 0, j)),
                  pl.BlockSpec((1, d), lambda i, j: (0, 0))],
        out_specs=[pl.BlockSpec((tm, d), lambda i, j: (i, 0)),
                   pl.BlockSpec((1, 2 * shift, tf), lambda i, j: (i, 0, j))],
        out_shape=[jax.ShapeDtypeStruct((m, d), F32),
                   jax.ShapeDtypeStruct((m // tm, 2 * shift, D_FF), F32)],
        scratch_shapes=[pltpu.VMEM((h_rows, d), BF16), pltpu.VMEM((n_f, prev_rows, tf), F32)],
        compiler_params=_params(2),
        name="outproj_convffn",
    )(x, mix, w_o, g_ffn, w_in, w_in, w_dw, b_dw, w_out, conv_state, g_final)


def _mla_latents(x_ref, g_ref, win_ref, gq_ref, gkv_ref, cos_ref, sin_ref, ckv_ref, kr_ref):
    h = _rms(x_ref[...], g_ref[...]).astype(BF16)
    p = jnp.dot(h, win_ref[...], preferred_element_type=F32)
    cq_n = _rms(p[:, 0:Q_LORA], gq_ref[...]).astype(BF16)
    ckv_n = _rms(p[:, Q_LORA:Q_LORA + KV_LORA], gkv_ref[...])
    ckv_ref[...] = ckv_n
    kr = _rope(p[:, Q_LORA + KV_LORA:], cos_ref[...], sin_ref[...])
    kr_ref[...] = kr[:, 0:ROPE]
    return cq_n, ckv_n, kr


def _mla_proj_prompt_kernel(x_ref, g_ref, win_ref, gq_ref, gkv_ref, wuq_ref, wukv_ref, cos_ref,
                            sin_ref, q_ref, k_ref, v_ref, ckv_ref, kr_ref, *, scale):
    cq_n, ckv_n, kr = _mla_latents(x_ref, g_ref, win_ref, gq_ref, gkv_ref, cos_ref, sin_ref,
                                   ckv_ref, kr_ref)
    cos, sin = cos_ref[...], sin_ref[...]
    kr_b = kr.astype(BF16)
    ckv_b = ckv_n.astype(BF16)
    for hh in range(H_C):
        qh = jnp.dot(cq_n, wuq_ref[:, hh * HEAD_W:(hh + 1) * HEAD_W],
                     preferred_element_type=F32) * scale
        q_ref[hh, :, 0:NOPE] = qh[:, 0:NOPE].astype(BF16)
        q_ref[hh, :, NOPE:HEAD_W] = _rope(qh[:, NOPE:HEAD_W], cos, sin).astype(BF16)
        kvh = jnp.dot(ckv_b, wukv_ref[:, hh * HEAD_W:(hh + 1) * HEAD_W],
                      preferred_element_type=F32)
        k_ref[hh, :, 0:NOPE] = kvh[:, 0:NOPE].astype(BF16)
        k_ref[hh, :, NOPE:HEAD_W] = kr_b
        v_ref[hh] = kvh[:, NOPE:HEAD_W].astype(BF16)


def _mla_proj_prompt(x, g, w_in, g_q, g_kv, w_uq, w_ukv, cos, sin, seq, tm, scale):
    m, d = x.shape
    nt = seq // tm
    const2 = lambda shape: pl.BlockSpec(shape, lambda i: (0, 0))
    return pl.pallas_call(
        functools.partial(_mla_proj_prompt_kernel, scale=scale),
        grid=(m // tm,),
        in_specs=[pl.BlockSpec((tm, d), lambda i: (i, 0)), const2((1, d)), const2(w_in.shape),
                  const2((1, Q_LORA)), const2((1, KV_LORA)), const2(w_uq.shape), const2(w_ukv.shape),
                  pl.BlockSpec((tm, LANES), lambda i: (i % nt, 0)),
                  pl.BlockSpec((tm, LANES), lambda i: (i % nt, 0))],
        out_specs=[pl.BlockSpec((H_C, tm, HEAD_W), lambda i: (0, i, 0)),
                   pl.BlockSpec((H_C, tm, HEAD_W), lambda i: (0, i, 0)),
                   pl.BlockSpec((H_C, tm, V_C), lambda i: (0, i, 0)),
                   pl.BlockSpec((tm, KV_LORA), lambda i: (i, 0)),
                   pl.BlockSpec((tm, ROPE), lambda i: (i, 0))],
        out_shape=[jax.ShapeDtypeStruct((H_C, m, HEAD_W), BF16),
                   jax.ShapeDtypeStruct((H_C, m, HEAD_W), BF16),
                   jax.ShapeDtypeStruct((H_C, m, V_C), BF16),
                   jax.ShapeDtypeStruct((m, KV_LORA), F32),
                   jax.ShapeDtypeStruct((m, ROPE), F32)],
        compiler_params=_params(1),
        name="mla_proj_prompt",
    )(x, g, w_in, g_q, g_kv, w_uq, w_ukv, cos, sin)


def _mla_proj_sample_kernel(x_ref, g_ref, win_ref, gq_ref, gkv_ref, wuq_ref, wukt_ref, cos_ref,
                            sin_ref, qlat_ref, qr_ref, ckv_ref, kr_ref, *, scale):
    cq_n, _, _ = _mla_latents(x_ref, g_ref, win_ref, gq_ref, gkv_ref, cos_ref, sin_ref,
                              ckv_ref, kr_ref)
    cos, sin = cos_ref[...], sin_ref[...]
    for hh in range(H_C):
        qh = jnp.dot(cq_n, wuq_ref[:, hh * HEAD_W:(hh + 1) * HEAD_W],
                     preferred_element_type=F32)
        q_lat = jnp.dot(qh[:, 0:NOPE].astype(BF16), wukt_ref[hh], preferred_element_type=F32)
        qlat_ref[hh] = (q_lat * scale).astype(BF16)
        qr_ref[hh] = (_rope(qh[:, NOPE:HEAD_W], cos, sin) * scale)[:, 0:ROPE].astype(BF16)


def _mla_proj_sample(x, g, w_in, g_q, g_kv, w_uq, w_ukt, cos, sin, scale):
    m, d = x.shape
    const = lambda shape: pl.BlockSpec(shape, lambda i: (0,) * len(shape))
    return pl.pallas_call(
        functools.partial(_mla_proj_sample_kernel, scale=scale),
        grid=(1,),
        in_specs=[const((m, d)), const((1, d)), const(w_in.shape), const((1, Q_LORA)),
                  const((1, KV_LORA)), const(w_uq.shape), const(w_ukt.shape),
                  const((m, LANES)), const((m, LANES))],
        out_specs=[const((H_C, m, KV_LORA)), const((H_C, m, ROPE)), const((m, KV_LORA)),
                   const((m, ROPE))],
        out_shape=[jax.ShapeDtypeStruct((H_C, m, KV_LORA), BF16),
                   jax.ShapeDtypeStruct((H_C, m, ROPE), BF16),
                   jax.ShapeDtypeStruct((m, KV_LORA), F32),
                   jax.ShapeDtypeStruct((m, ROPE), F32)],
        compiler_params=_params(1),
        name="mla_proj_sample",
    )(x, g, w_in, g_q, g_kv, w_uq, w_ukt, cos, sin)


def _flash_kernel(q_ref, k_ref, v_ref, o_ref, m_s, acc_s, *, tile, n_tiles):
    t = tile
    row = lax.broadcasted_iota(jnp.int32, (t, t), 0)
    col = lax.broadcasted_iota(jnp.int32, (t, t), 1)
    visible = (col // CHUNK) <= (row // CHUNK)
    ones = jnp.ones((t, V_C), BF16)
    for j in range(n_tiles):
        ks = slice(j * t, (j + 1) * t)
        kj = k_ref[0, ks, :]
        vj = jnp.concatenate([v_ref[0, ks, :], ones], axis=1)
        for i in range(j, n_tiles):
            qs = slice(i * t, (i + 1) * t)
            s = _nt_dot(q_ref[0, qs, :], kj)
            if i == j:
                s = jnp.where(visible, s, -jnp.inf)
            m_new = jnp.broadcast_to(jnp.max(s, axis=-1, keepdims=True), (t, LANES))
            if j > 0:
                m_prev = m_s[qs]
                m_new = jnp.maximum(m_prev, m_new)
            p = jnp.exp2(s - jnp.concatenate([m_new] * (t // LANES), axis=1))
            pv = jnp.dot(p.astype(BF16), vj, preferred_element_type=F32)
            if j > 0:
                alpha = jnp.exp2(m_prev - m_new)
                pv = pv + jnp.concatenate([alpha, alpha], axis=1) * acc_s[qs]
            if i == j:
                o_ref[qs, :] = (pv[:, :V_C] / pv[:, V_C:]).astype(o_ref.dtype)
            else:
                m_s[qs] = m_new
                acc_s[qs] = pv


def _flash_prompt(q, k, v, batch, seq, tile):
    m = batch * seq
    return pl.pallas_call(
        functools.partial(_flash_kernel, tile=tile, n_tiles=seq // tile),
        grid=(batch, H_C),
        in_specs=[pl.BlockSpec((1, seq, HEAD_W), lambda b, h: (h, b, 0)),
                  pl.BlockSpec((1, seq, HEAD_W), lambda b, h: (h, b, 0)),
                  pl.BlockSpec((1, seq, V_C), lambda b, h: (h, b, 0))],
        out_specs=pl.BlockSpec((seq, V_C), lambda b, h: (b, h)),
        out_shape=jax.ShapeDtypeStruct((m, H_C * V_C), BF16),
        scratch_shapes=[pltpu.VMEM((seq, LANES), F32), pltpu.VMEM((seq, 2 * V_C), F32)],
        compiler_params=_params(2),
        name="flash_prompt",
    )(q, k, v)


def _latent_attn_kernel(qlat_ref, qr_ref, cckv_ref, ckr_ref, nckv_ref, nkr_ref, wuv_ref, o_ref,
                        *, steps, new_rows, tk):
    rows = H_C * steps
    q_lat = qlat_ref[...].reshape(rows, KV_LORA)
    q_r = qr_ref[...].reshape(rows, ROPE)

    def update(state, ckv, kr, valid, kr_transposed):
        ckv_b = ckv.astype(BF16)
        if kr_transposed:
            s_r = jnp.dot(q_r, kr.astype(BF16), preferred_element_type=F32)
        else:
            s_r = _nt_dot(q_r, kr.astype(BF16))
        s = _nt_dot(q_lat, ckv_b) + s_r
        if valid is not None:
            col = lax.broadcasted_iota(jnp.int32, s.shape, 1)
            s = jnp.where(col < valid, s, -jnp.inf)
        m_blk = jnp.max(s, axis=-1, keepdims=True)
        if state is None:
            m_new = m_blk
            p = jnp.exp(s - m_new)
            return (m_new, jnp.sum(p, axis=-1, keepdims=True),
                    jnp.dot(p.astype(BF16), ckv_b, preferred_element_type=F32))
        m_prev, l_prev, acc = state
        m_new = jnp.maximum(m_prev, m_blk)
        p = jnp.exp(s - m_new)
        alpha = jnp.exp(m_prev - m_new)
        return (m_new, alpha * l_prev + jnp.sum(p, axis=-1, keepdims=True),
                alpha * acc + jnp.dot(p.astype(BF16), ckv_b, preferred_element_type=F32))

    state = None
    for j in range(cckv_ref.shape[1] // tk):
        ks = slice(j * tk, (j + 1) * tk)
        state = update(state, cckv_ref[0, ks, :], ckr_ref[0, :, ks], None, True)
    _, l_fin, acc = update(state, nckv_ref[0], nkr_ref[0], new_rows, False)
    o_lat = (acc / l_fin).astype(BF16)
    for hh in range(H_C):
        o_ref[:, hh * V_C:(hh + 1) * V_C] = jnp.dot(
            o_lat[hh * steps:(hh + 1) * steps], wuv_ref[hh],
            preferred_element_type=F32).astype(o_ref.dtype)


def _latent_attn_sample(q_lat, q_r, cache_ckv, cache_kr, new_ckv, new_kr, w_uv, steps, tk):
    batch, past, _ = cache_ckv.shape
    pad_rows = new_ckv.shape[1]
    return pl.pallas_call(
        functools.partial(_latent_attn_kernel, steps=steps, new_rows=steps, tk=tk),
        grid=(batch,),
        in_specs=[pl.BlockSpec((H_C, steps, KV_LORA), lambda b: (0, b, 0)),
                  pl.BlockSpec((H_C, steps, ROPE), lambda b: (0, b, 0)),
                  pl.BlockSpec((1, past, KV_LORA), lambda b: (b, 0, 0)),
                  pl.BlockSpec((1, ROPE, past), lambda b: (b, 0, 0)),
                  pl.BlockSpec((1, pad_rows, KV_LORA), lambda b: (b, 0, 0)),
                  pl.BlockSpec((1, pad_rows, ROPE), lambda b: (b, 0, 0)),
                  pl.BlockSpec((H_C, KV_LORA, V_C), lambda b: (0, 0, 0))],
        out_specs=pl.BlockSpec((steps, H_C * V_C), lambda b: (b, 0)),
        out_shape=jax.ShapeDtypeStruct((batch * steps, H_C * V_C), BF16),
        compiler_params=_params(1),
        name="latent_attn_sample",
    )(q_lat, q_r, cache_ckv, cache_kr, new_ckv, new_kr, w_uv)


def _rope_tables(pos, width):
    half = ROPE // 2
    freqs = ROPE_THETA ** (-jnp.arange(half, dtype=F32) / half)
    ang = pos.astype(F32)[:, None] * freqs[None, :]
    cos, sin = jnp.cos(ang), jnp.sin(ang)
    cos_h = jnp.concatenate([cos, cos], axis=-1)
    sin_h = jnp.concatenate([-sin, sin], axis=-1)
    reps = width // ROPE
    return jnp.tile(cos_h, (1, reps)), jnp.tile(sin_h, (1, reps))


def _prep_even(w_in_ab, w_gate_up):
    sizes = [H_A * DK_A, H_A * DK_A, H_A * DV_A, H_A * DV_A, GATE_RANK,
             H_B * DK_B, H_B * DK_B, H_B * DV_B, H_B * DV_B]
    offs = np.concatenate([[0], np.cumsum(sizes)])
    parts = [w_in_ab[:, offs[n]:offs[n + 1]] for n in range(9)]
    lo = jnp.pad(parts[4], ((0, 0), (0, LANES - GATE_RANK)))
    w = jnp.concatenate(parts[0:4] + parts[5:9] + [lo], axis=1).astype(BF16)
    w_gu = jnp.pad(w_gate_up, ((0, LANES - GATE_RANK), (0, 0))).astype(BF16)
    return w, w_gu


def _prep_odd(w_in_c, w_uq, w_uk, w_uv):
    d = w_in_c.shape[0]
    w_in = jnp.pad(w_in_c, ((0, 0), (0, LANES - ROPE))).astype(BF16)
    uq = w_uq.reshape(Q_LORA, H_C, NOPE + ROPE)
    uq = jnp.pad(uq, ((0, 0), (0, 0), (0, HEAD_W - NOPE - ROPE)))
    w_uq_p = uq.reshape(Q_LORA, H_C * HEAD_W).astype(BF16)
    w_ukv = jnp.concatenate([w_uk, w_uv], axis=2).reshape(KV_LORA, H_C * HEAD_W).astype(BF16)
    w_ukt = jnp.transpose(w_uk, (1, 2, 0)).astype(BF16)
    w_uvh = jnp.transpose(w_uv, (1, 0, 2)).astype(BF16)
    del d
    return w_in, w_uq_p, w_ukv, w_ukt, w_uvh


def _log_gamma_row():
    lg = np.log1p(-np.exp2(-5.0 - np.arange(H_B, dtype=np.float32))).astype(np.float32)
    return jnp.asarray(np.repeat(lg, DK_B)[None, :])


def _to_time_major(a, batch, steps):
    return a.reshape(batch, steps, -1).transpose(1, 0, 2).reshape(batch * steps, -1)


def _to_batch_major(a, batch, steps):
    return a.reshape(steps, batch, -1).transpose(1, 0, 2).reshape(batch * steps, -1)


def _divisor_tile(n, pref):
    t = min(n, pref)
    while n % t:
        t //= 2
    return t


def _trunk(x, pos0, gla0, ret0, ckv_past, kr_past, conv0, w, time_major_ffn):
    batch, seq, d = x.shape
    m = batch * seq
    pos = pos0 + jnp.arange(seq, dtype=jnp.int32)
    row = lambda v: v.reshape(1, -1)
    scale = float((NOPE + ROPE) ** -0.5)
    tf = TF_FFN
    tm_proj = _divisor_tile(m, TM_PROJ)

    if time_major_ffn:
        ffn_kw = dict(tm=m, tf=tf, shift=batch, tiles_per_seq=1)
        to_ffn = lambda a: _to_time_major(a, batch, seq)
        from_ffn = lambda a: _to_batch_major(a, batch, seq)
        prep_state = lambda s: s.transpose(1, 0, 2).reshape(1, 2 * batch, D_FF)
        post_state = lambda s: s.reshape(2, batch, D_FF).transpose(1, 0, 2)
    else:
        tm_ffn = _divisor_tile(seq, TM_FFN)
        ffn_kw = dict(tm=tm_ffn, tf=tf, shift=1, tiles_per_seq=seq // tm_ffn)
        to_ffn = from_ffn = lambda a: a
        prep_state = lambda s: jnp.pad(s, ((0, 0), (SUBLANES - 2, 0), (0, 0)))
        post_state = lambda s: s

    xf = x.reshape(m, d)
    conv_new = []
    gla_new = ret_new = ckv_new = kr_new = None
    for layer in range(2):
        g_mix = row(w['norm_mix'][layer])
        if layer == 0:
            w_ab, w_gu = _prep_even(w['w_in_ab'][0], w['w_gate_up'][0])
            cos, sin = _rope_tables(pos, 4 * DK_B)
            rows = _divisor_tile(seq, ROWS_LINATTN)
            small = (w_gu, row(w['b_gate'][0]), row(w['g_gla'][0]), row(w['g_ret'][0]), cos, sin,
                     _log_gamma_row())
            if gla0 is None and rows >= 2 * CHUNK:
                mix, gla_new, ret_new = _proj_linattn(xf, g_mix, w_ab, batch, seq, rows, *small)
            else:
                slab = _norm_matmul(xf, g_mix, w_ab, tm_proj, 4 * HEAD_W)
                mix, gla_new, ret_new = _linattn(
                    slab, batch, seq, rows, *small,
                    None if gla0 is None else gla0[0], None if ret0 is None else ret0[0])
            w_o = w['w_out_ab'][0].astype(BF16)
        else:
            w_in, w_uq_p, w_ukv, w_ukt, w_uvh = _prep_odd(w['w_in_c'][0], w['w_uq'][0], w['w_uk'][0],
                                                          w['w_uv'][0])
            cos, sin = _rope_tables(pos, LANES)
            g_q, g_kv = row(w['g_q'][0]), row(w['g_kv'][0])
            if ckv_past is None:
                q, k, v, ckv_new, kr_new = _mla_proj_prompt(
                    xf, g_mix, w_in, g_q, g_kv, w_uq_p, w_ukv, cos, sin, seq, tm_proj,
                    scale * float(np.log2(np.e)))
                mix = _flash_prompt(q, k, v, batch, seq, _divisor_tile(seq, TQ_FLASH))
            else:
                assert ckv_past.shape[2] % CHUNK == 0 and seq <= CHUNK
                cos_m, sin_m = jnp.tile(cos, (batch, 1)), jnp.tile(sin, (batch, 1))
                q_lat, q_r, ckv_new, kr_new = _mla_proj_sample(
                    xf, g_mix, w_in, g_q, g_kv, w_uq_p, w_ukt, cos_m, sin_m, scale)
                pad = ((0, 0), (0, LANES - seq), (0, 0))
                new_ckv = jnp.pad(ckv_new.reshape(batch, seq, KV_LORA), pad)
                new_kr = jnp.pad(kr_new.reshape(batch, seq, ROPE), pad)
                mix = _latent_attn_sample(q_lat, q_r, ckv_past[0],
                                          jnp.swapaxes(kr_past[0], 1, 2), new_ckv, new_kr,
                                          w_uvh, seq, _divisor_tile(ckv_past.shape[2], TK_LATENT))
            w_o = w['w_out_c'][0].astype(BF16)

        if conv0 is None:
            state = jnp.zeros((batch, SUBLANES, D_FF), F32) if not time_major_ffn else \
                jnp.zeros((1, 2 * batch, D_FF), F32)
        else:
            state = prep_state(conv0[layer])
        y, conv_rows = _ffn(to_ffn(xf), to_ffn(mix), w_o, row(w['norm_ffn'][layer]),
                            w['w_ffn_in'].astype(BF16), w['w_dwconv'][layer],
                            row(w['b_dwconv'][layer]),
                            (0.5 * w['w_ffn_out'][layer]).astype(BF16), state,
                            row(w['norm_final']), layer=layer, final_norm=(layer == 1), **ffn_kw)
        xf = from_ffn(y)
        tps = ffn_kw['tiles_per_seq']
        conv_new.append(post_state(conv_rows[tps - 1::tps]))

    return (xf.reshape(batch, seq, d), gla_new[None], ret_new[None],
            ckv_new.reshape(1, batch, seq, KV_LORA), kr_new.reshape(1, batch, seq, ROPE),
            jnp.stack(conv_new))


def kernel(x_prompt, x_sample, state_gla, state_ret, cache_ckv, cache_krope, state_conv, norm_mix, norm_ffn, norm_final, w_in_ab, w_gate_up, b_gate, g_gla, g_ret, w_out_ab, w_in_c, g_q, g_kv, w_uq, w_uk, w_uv, w_out_c, w_ffn_in, w_dwconv, b_dwconv, w_ffn_out):
    w = {'norm_mix': norm_mix, 'norm_ffn': norm_ffn, 'norm_final': norm_final,
         'w_in_ab': w_in_ab, 'w_gate_up': w_gate_up, 'b_gate': b_gate, 'g_gla': g_gla,
         'g_ret': g_ret, 'w_out_ab': w_out_ab, 'w_in_c': w_in_c, 'g_q': g_q, 'g_kv': g_kv,
         'w_uq': w_uq, 'w_uk': w_uk, 'w_uv': w_uv, 'w_out_c': w_out_c, 'w_ffn_in': w_ffn_in,
         'w_dwconv': w_dwconv, 'b_dwconv': b_dwconv, 'w_ffn_out': w_ffn_out}
    past_len = cache_ckv.shape[2]
    y_p, gla_p, ret_p, ckv_p, kr_p, conv_p = _trunk(
        x_prompt, 0, None, None, None, None, None, w, time_major_ffn=False)
    y_s, gla_s, ret_s, ckv_s, kr_s, conv_s = _trunk(
        x_sample, past_len, state_gla, state_ret, cache_ckv, cache_krope, state_conv, w,
        time_major_ffn=True)
    return (y_p, y_s, gla_p, gla_s, ret_p, ret_s, ckv_p, ckv_s, kr_p, kr_s, conv_p, conv_s)
```

```python
import functools

import numpy as np
import jax
import jax.numpy as jnp
from jax import lax
from jax.experimental import pallas as pl
from jax.experimental.pallas import tpu as pltpu

F32 = jnp.float32
BF16 = jnp.bfloat16

D_MODEL = 1024
CHUNK = 64
EPS = 1e-6
ROPE_THETA = 10000.0
H_A, DK_A, DV_A = 4, 64, 128
GATE_RANK = 16
GATE_TAU = 16.0
H_B, DK_B, DV_B = 4, 64, 128
H_C = 8
Q_LORA, KV_LORA, NOPE, ROPE, V_C = 384, 512, 128, 64, 128
D_FF = 2816
CONV_W = 3

LANES = 128
SUBLANES = 8
VMEM_LIMIT = 56 * 1024 * 1024
AB_COLS = 3200
HEAD_W = 2 * LANES
TM_PROJ = 1024
TM_FFN = 1024
TF_FFN = D_FF
SUB_FFN = 256
ROWS_LINATTN = 256
TQ_FLASH = 256
TK_LATENT = 1024


def _params(n_axes):
    return pltpu.CompilerParams(dimension_semantics=("arbitrary",) * n_axes,
                                vmem_limit_bytes=VMEM_LIMIT)


def _rms(x, g):
    return x * lax.rsqrt(jnp.mean(x * x, axis=-1, keepdims=True) + EPS) * g


def _rope_first_half(shape):
    lane = lax.broadcasted_iota(jnp.int32, shape, 1)
    return (lane % ROPE) < ROPE // 2


def _rope(x, cos, sin_signed, first=None):
    w = x.shape[1]
    half = ROPE // 2
    if first is None:
        first = _rope_first_half(x.shape)
    swapped = jnp.where(first, pltpu.roll(x, w - half, 1), pltpu.roll(x, half, 1))
    return x * cos + swapped * sin_signed


def _nt_dot(a, b):
    return lax.dot_general(a, b, (((1,), (1,)), ((), ())), preferred_element_type=F32)


def _norm_matmul_kernel(x_ref, g_ref, w_ref, o_ref, *, col_chunk):
    h = _rms(x_ref[...], g_ref[...]).astype(BF16)
    n = w_ref.shape[1]
    for c0 in range(0, n, col_chunk):
        c1 = min(c0 + col_chunk, n)
        o_ref[:, c0:c1] = jnp.dot(h, w_ref[:, c0:c1], preferred_element_type=F32).astype(o_ref.dtype)


def _norm_matmul(x, g, w, tm, col_chunk):
    m, d = x.shape
    n = w.shape[1]
    return pl.pallas_call(
        functools.partial(_norm_matmul_kernel, col_chunk=col_chunk),
        grid=(m // tm,),
        in_specs=[pl.BlockSpec((tm, d), lambda i: (i, 0)),
                  pl.BlockSpec((1, d), lambda i: (0, 0)),
                  pl.BlockSpec((d, n), lambda i: (0, 0))],
        out_specs=pl.BlockSpec((tm, n), lambda i: (i, 0)),
        out_shape=jax.ShapeDtypeStruct((m, n), BF16),
        compiler_params=_params(1),
        name="norm_inproj_ab",
    )(x, g, w)


def _decay_attention_streams(streams, mask_k, mask_v, n_chunks, mid, last, between):
    c = CHUNK
    assert 2 * c == LANES
    hk, hv = 4 * DK_A, 4 * DV_A
    rt = lax.broadcasted_iota(jnp.int32, (c, 4 * c), 0)
    ct = lax.broadcasted_iota(jnp.int32, (c, 4 * c), 1)
    causal = (ct % c) <= rt
    lane = lax.broadcasted_iota(jnp.int32, (DK_A, LANES), 1)
    zero_v = jnp.zeros((LANES - c, hv), BF16)
    zero_blk = jnp.zeros((DK_A, DV_A), BF16)
    log2e = np.float32(np.log2(np.e))

    items = []
    for q, k, v, bcum, s_ref in streams:
        bcum2 = bcum * log2e
        for n in range(n_chunks):
            sl = slice(n * c, (n + 1) * c)
            qn, kn, vn, bn = q[sl], k[sl], v[sl], bcum2[sl]
            b_ref = bn[mid:mid + 1]
            b_last = bn[last:last + 1]
            it = dict(s_ref=s_ref, vn=vn)
            it['q_rel'] = (qn * jnp.exp2(bn - b_ref)).astype(BF16)
            k_rel = (kn * jnp.exp2(b_ref - bn)).astype(BF16)
            it['k_bd'] = jnp.concatenate([k_rel] * 4, axis=0) * mask_k
            it['q_dec'] = (qn * jnp.exp2(bn)).astype(BF16)
            k_dec = kn * jnp.exp2(b_last - bn)
            decay_rows = jnp.broadcast_to(jnp.exp2(b_last), (LANES - c, hk))
            it['kt'] = jnp.concatenate([k_dec, decay_rows], axis=0).T
            items.append(it)
    between()
    for it in items:
        it['scores'] = jnp.where(causal, _nt_dot(it['q_rel'], it['k_bd']), 0.0).astype(BF16)
    between()
    for it in items:
        v_bd = jnp.concatenate([it['vn']] * 4, axis=0) * mask_v
        it['o'] = jnp.dot(it['scores'], v_bd, preferred_element_type=F32)
        kt_b = it['kt'].astype(BF16)
        v_pad = jnp.concatenate([it['vn'], zero_v], axis=0)
        it['kv'] = [jnp.dot(kt_b[h * DK_A:(h + 1) * DK_A], v_pad[:, h * DV_A:(h + 1) * DV_A],
                            preferred_element_type=F32) for h in range(4)]
    between()
    for it in items:
        s_ref = it['s_ref']
        s_old = [s_ref[h] for h in range(4)]
        it['s_bd'] = jnp.concatenate(
            [jnp.concatenate([zero_blk] * h + [s_old[h].astype(BF16)] + [zero_blk] * (3 - h), axis=1)
             for h in range(4)], axis=0)
        for h in range(4):
            kth = it['kt'][h * DK_A:(h + 1) * DK_A]
            decay = jnp.where(lane < c, pltpu.roll(kth, c, 1), kth)
            s_ref[h] = s_old[h] * decay + it['kv'][h]
    between()
    outs = []
    for si in range(len(streams)):
        chunk_out = [it['o'] + jnp.dot(it['q_dec'], it['s_bd'], preferred_element_type=F32)
                     for it in items[si * n_chunks:(si + 1) * n_chunks]]
        outs.append(chunk_out[0] if n_chunks == 1 else jnp.concatenate(chunk_out, axis=0))
    between()
    return outs


_SLAB = dict(qa=(0, 256), ka=(256, 256), va=(512, 512), ra=(1024, 512),
             qb=(1536, 256), kb=(1792, 256), vb=(2048, 512), gb=(2560, 512), lo=(3072, LANES))


def _linattn_compute(cols, wgu_ref, bg_ref, gg_ref, gr_ref, cos_ref, sin_ref, lgam_ref, tri_ref,
                     mk_ref, mv_ref, mix_ref, sa, sb, rows, between=lambda: None):
    c = CHUNK
    padded = max(rows, c)
    n_chunks = padded // c
    valid = min(rows, c)
    mid, last = (valid - 1) // 2, valid - 1

    def pad_rows(x):
        if padded == rows:
            return x
        return jnp.concatenate([x, jnp.zeros((padded - rows, x.shape[1]), x.dtype)], axis=0)

    tri = tri_ref[...]
    mask_k, mask_v = mk_ref[...], mv_ref[...]

    gate = jnp.dot(cols['lo'], wgu_ref[...], preferred_element_type=F32) + bg_ref[...]
    log_a = (jnp.minimum(gate, 0.0) - jnp.log1p(jnp.exp(-jnp.abs(gate)))) / GATE_TAU
    log_a = pad_rows(log_a)
    hi = log_a.astype(BF16)
    lo = (log_a - hi.astype(F32)).astype(BF16)
    bcum_a = (jnp.dot(tri, hi, preferred_element_type=F32)
              + jnp.dot(tri, lo, preferred_element_type=F32))
    qa = pad_rows(cols['qa'].astype(F32) * DK_A ** -0.5)
    ka = pad_rows(cols['ka'].astype(F32))
    va = pad_rows(cols['va'])
    between()

    cos, sin = cos_ref[...], sin_ref[...]
    first = _rope_first_half(cos.shape)
    qb = pad_rows(_rope(cols['qb'].astype(F32), cos, sin, first))
    kb = pad_rows(_rope(cols['kb'].astype(F32), cos, sin, first) * DK_B ** -0.5)
    vb = pad_rows(cols['vb'])
    pos = lax.broadcasted_iota(jnp.int32, (padded, 4 * DK_B), 0) % c
    steps = jnp.where(pos < valid, pos + 1, valid).astype(F32)
    bcum_b = steps * lgam_ref[...]
    o_a, o_b = _decay_attention_streams(
        [(qa, ka, va, bcum_a, sa), (qb, kb, vb, bcum_b, sb)], mask_k, mask_v, n_chunks, mid, last,
        between)
    o_a, o_b = o_a[:rows], o_b[:rows]

    gg, gr = gg_ref[...], gr_ref[...]
    for h in range(4):
        sl = slice(h * DV_A, (h + 1) * DV_A)
        oh = o_a[:, sl]
        oh = oh * lax.rsqrt(jnp.mean(oh * oh, axis=-1, keepdims=True) + EPS) * gg[:, sl]
        r = cols['ra'][:, sl].astype(F32)
        mix_ref[:, sl] = (oh * (r * jax.nn.sigmoid(r))).astype(mix_ref.dtype)
        ob = o_b[:, sl]
        ob = ob - jnp.mean(ob, axis=-1, keepdims=True)
        ob = ob * lax.rsqrt(jnp.mean(ob * ob, axis=-1, keepdims=True) + EPS) * gr[:, sl]
        gb = cols['gb'][:, sl].astype(F32)
        mix_ref[:, 4 * DV_A + h * DV_B:4 * DV_A + (h + 1) * DV_B] = (
            ob * (gb * jax.nn.sigmoid(gb))).astype(mix_ref.dtype)


def _linattn_kernel(qa_ref, ka_ref, va_ref, ra_ref, qb_ref, kb_ref, vb_ref, gb_ref, lo_ref,
                    wgu_ref, bg_ref, gg_ref, gr_ref, cos_ref, sin_ref, lgam_ref, tri_ref, mk_ref,
                    mv_ref, *rest, rows, has_state):
    if has_state:
        s0a_ref, s0b_ref, mix_ref, sa_out, sb_out, sa, sb = rest
    else:
        mix_ref, sa_out, sb_out, sa, sb = rest
    t = pl.program_id(1)

    @pl.when(t == 0)
    def _init():
        if has_state:
            sa[...] = s0a_ref[0]
            sb[...] = s0b_ref[0]
        else:
            sa[...] = jnp.zeros_like(sa)
            sb[...] = jnp.zeros_like(sb)

    cols = dict(qa=qa_ref[...], ka=ka_ref[...], va=va_ref[...], ra=ra_ref[...], qb=qb_ref[...],
                kb=kb_ref[...], vb=vb_ref[...], gb=gb_ref[...], lo=lo_ref[...])
    _linattn_compute(cols, wgu_ref, bg_ref, gg_ref, gr_ref, cos_ref, sin_ref, lgam_ref, tri_ref,
                     mk_ref, mv_ref, mix_ref, sa, sb, rows)

    @pl.when(t == pl.num_programs(1) - 1)
    def _emit_state():
        sa_out[0] = sa[...]
        sb_out[0] = sb[...]


def _proj_linattn_kernel(x_ref, g_ref, w_ref, wgu_ref, bg_ref, gg_ref, gr_ref, cos_ref, sin_ref,
                         lgam_ref, tri_ref, mk_ref, mv_ref, mix_ref, sa_out, sb_out, slab_even,
                         slab_odd, sa, sb, *, rows, nt, col_chunk):
    s = pl.program_id(0)
    t = jnp.maximum(s - 1, 0) % nt

    @pl.when(s == 0)
    def _zero_slab():
        slab_odd[...] = jnp.zeros_like(slab_odd)

    @pl.when(t == 0)
    def _init():
        sa[...] = jnp.zeros_like(sa)
        sb[...] = jnp.zeros_like(sb)

    def step(write_ref, read_ref):
        cols = {name: read_ref[:, off:off + width] for name, (off, width) in _SLAB.items()}
        h = _rms(x_ref[...], g_ref[...]).astype(BF16)
        n = w_ref.shape[1]
        pieces = list(range(0, n, col_chunk))
        shares = (5, 1, 0, 1, 1, 5)
        total = len(pieces)
        quotas = [total * sum(shares[:k + 1]) // sum(shares) - total * sum(shares[:k]) // sum(shares)
                  for k in range(len(shares))]

        def project(count):
            for _ in range(min(count, len(pieces))):
                c0 = pieces.pop(0)
                c1 = min(c0 + col_chunk, n)
                write_ref[:, c0:c1] = jnp.dot(h, w_ref[:, c0:c1],
                                              preferred_element_type=F32).astype(write_ref.dtype)

        _linattn_compute(cols, wgu_ref, bg_ref, gg_ref, gr_ref, cos_ref, sin_ref, lgam_ref,
                         tri_ref, mk_ref, mv_ref, mix_ref, sa, sb, rows,
                         between=lambda: project(quotas.pop(0) if quotas else 0))
        project(len(pieces))

    @pl.when(s % 2 == 0)
    def _even():
        step(slab_even, slab_odd)

    @pl.when(s % 2 == 1)
    def _odd():
        step(slab_odd, slab_even)

    @pl.when(t == nt - 1)
    def _emit_state():
        sa_out[0] = sa[...]
        sb_out[0] = sb[...]


def _linattn_masks(rows):
    c = CHUNK
    padded = max(rows, c)
    r = np.arange(padded)
    tri = jnp.asarray(((r[:, None] // c == r[None, :] // c) & (r[None, :] <= r[:, None])), BF16)
    r4 = np.arange(4 * c)
    mask_k = jnp.asarray(r4[:, None] // c == np.arange(4 * DK_A)[None, :] // DK_A, BF16)
    mask_v = jnp.asarray(r4[:, None] // c == np.arange(4 * DV_A)[None, :] // DV_A, BF16)
    return tri, mask_k, mask_v


def _proj_linattn(x, g, w_ab, batch, seq, rows, w_gu, b_gate, g_gla, g_ret, cos, sin, lgam):
    m, d = x.shape
    nt = seq // rows
    n_tiles = batch * nt
    tri, mask_k, mask_v = _linattn_masks(rows)
    hk, hv = 4 * DK_A, 4 * DV_A
    prev = lambda s: jnp.maximum(s - 1, 0)

    def const(shape):
        return pl.BlockSpec(shape, lambda s: (0,) * len(shape))

    state_spec = pl.BlockSpec((1, 4, DK_A, DV_A), lambda s: (prev(s) // nt, 0, 0, 0))
    state_shape = jax.ShapeDtypeStruct((batch, 4, DK_A, DV_A), F32)
    return pl.pallas_call(
        functools.partial(_proj_linattn_kernel, rows=rows, nt=nt, col_chunk=HEAD_W),
        grid=(n_tiles + 1,),
        in_specs=[pl.BlockSpec((rows, d), lambda s: (jnp.minimum(s, n_tiles - 1), 0)),
                  const((1, d)), const(w_ab.shape),
                  const((LANES, hk)), const((1, hk)), const((1, hv)), const((1, hv)),
                  pl.BlockSpec((rows, hk), lambda s: (prev(s) % nt, 0)),
                  pl.BlockSpec((rows, hk), lambda s: (prev(s) % nt, 0)),
                  const((1, hk)), const(tri.shape), const(mask_k.shape), const(mask_v.shape)],
        out_specs=[pl.BlockSpec((rows, 2 * hv), lambda s: (prev(s), 0)), state_spec, state_spec],
        out_shape=[jax.ShapeDtypeStruct((m, 2 * hv), BF16), state_shape, state_shape],
        scratch_shapes=[pltpu.VMEM((rows, AB_COLS), BF16), pltpu.VMEM((rows, AB_COLS), BF16),
                        pltpu.VMEM((4, DK_A, DV_A), F32), pltpu.VMEM((4, DK_B, DV_B), F32)],
        compiler_params=_params(1),
        name="inproj_decay_linear_attention",
    )(x, g, w_ab, w_gu, b_gate, g_gla, g_ret, cos, sin, lgam, tri, mask_k, mask_v)


def _linattn(slab, batch, seq, rows, w_gu, b_gate, g_gla, g_ret, cos, sin, lgam, s0a, s0b):
    nt = seq // rows
    has_state = s0a is not None
    tri, mask_k, mask_v = _linattn_masks(rows)

    def col(width, idx):
        return pl.BlockSpec((rows, width), lambda b, t, idx=idx: (b * nt + t, idx))

    def const(shape):
        return pl.BlockSpec(shape, lambda b, t: (0,) * len(shape))

    hk = 4 * DK_A
    hv = 4 * DV_A
    in_specs = [col(hk, 0), col(hk, 1), col(hv, 1), col(hv, 2),
                col(hk, 6), col(hk, 7), col(hv, 4), col(hv, 5),
                col(LANES, 24),
                const((LANES, hk)), const((1, hk)), const((1, hv)), const((1, hv)),
                pl.BlockSpec((rows, hk), lambda b, t: (t, 0)),
                pl.BlockSpec((rows, hk), lambda b, t: (t, 0)),
                const((1, hk)), const(tri.shape), const(mask_k.shape), const(mask_v.shape)]
    args = [slab] * 9 + [w_gu, b_gate, g_gla, g_ret, cos, sin, lgam, tri, mask_k, mask_v]
    state_spec = pl.BlockSpec((1, 4, DK_A, DV_A), lambda b, t: (b, 0, 0, 0))
    if has_state:
        in_specs += [state_spec, state_spec]
        args += [s0a, s0b]
    state_shape = jax.ShapeDtypeStruct((batch, 4, DK_A, DV_A), F32)
    return pl.pallas_call(
        functools.partial(_linattn_kernel, rows=rows, has_state=has_state),
        grid=(batch, nt),
        in_specs=in_specs,
        out_specs=[pl.BlockSpec((rows, 2 * hv), lambda b, t: (b * nt + t, 0)), state_spec, state_spec],
        out_shape=[jax.ShapeDtypeStruct((batch * seq, 2 * hv), BF16), state_shape, state_shape],
        scratch_shapes=[pltpu.VMEM((4, DK_A, DV_A), F32), pltpu.VMEM((4, DK_B, DV_B), F32)],
        compiler_params=_params(2),
        name="decay_linear_attention",
    )(*args)


def _ffn_kernel(x_ref, mix_ref, wo_ref, g_ref, wa_ref, wu_ref, wdw_ref, bdw_ref, wout_ref, st_ref,
                gf_ref, y_ref, cv_ref, h_s, carry_s,
                *, tm, sub, prev_rows, shift, tiles_per_seq, final_norm, single_step):
    i = pl.program_id(0)
    j = pl.program_id(1)
    p = prev_rows
    subs = [slice(r, r + sub) for r in range(0, tm, sub)]

    def mid_residual(rs):
        xm = x_ref[rs] + jnp.dot(mix_ref[rs], wo_ref[...], preferred_element_type=F32)
        return xm, _rms(xm, g_ref[...]).astype(BF16)

    if not single_step:
        @pl.when(j == 0)
        def _start():
            for rs in subs:
                y_ref[rs], h_s[rs] = mid_residual(rs)

    at_start = (i % tiles_per_seq) == 0
    tail = jnp.where(at_start, st_ref[0], carry_s[j])
    last = j == pl.num_programs(1) - 1
    wdw = wdw_ref[...]
    bdw = bdw_ref[...]
    a = None
    for rs in subs:
        if single_step:
            base, h = mid_residual(rs)
        else:
            base, h = y_ref[rs], h_s[rs]
        a = jnp.dot(h, wa_ref[...], preferred_element_type=F32)
        u = jnp.dot(h, wu_ref[...], preferred_element_type=F32)
        ext = jnp.concatenate([tail, a], axis=0)
        conv = bdw + pltpu.roll(ext, 2 * shift, 0)[p:] * wdw[0:1]
        conv = conv + pltpu.roll(ext, shift, 0)[p:] * wdw[1:2]
        conv = conv + a * wdw[2:3]
        act = conv * (1.0 + lax.erf(conv * np.float32(np.sqrt(0.5)))) * u
        out = base + jnp.dot(act.astype(BF16), wout_ref[...], preferred_element_type=F32)
        if final_norm:
            normed = _rms(out, gf_ref[...])
            out = normed if single_step else jnp.where(last, normed, out)
        y_ref[rs] = out
        tail = a[sub - p:]
    carry_s[j] = tail
    cv_ref[0] = a[sub - 2 * shift:]


def _ffn(x, mix, w_o, g_ffn, w_in, w_dw, b_dw, w_out, conv_state, g_final, *, layer, tm, tf, shift,
         tiles_per_seq, final_norm):
    m, d = x.shape
    n_f = D_FF // tf
    n_seq, prev_rows, _ = conv_state.shape
    kern = functools.partial(_ffn_kernel, tm=tm, sub=min(tm, SUB_FFN), prev_rows=prev_rows,
                             shift=shift, tiles_per_seq=tiles_per_seq, final_norm=final_norm,
                             single_step=(n_f == 1))
    h_rows = SUBLANES * 2 if n_f == 1 else tm
    return pl.pallas_call(
        kern,
        grid=(m // tm, n_f),
        in_specs=[pl.BlockSpec((tm, d), lambda i, j: (i, 0)),
                  pl.BlockSpec((tm, d), lambda i, j: (i, 0)),
                  pl.BlockSpec((d, d), lambda i, j: (0, 0)),
                  pl.BlockSpec((1, d), lambda i, j: (0, 0)),
                  pl.BlockSpec((None, d, tf), lambda i, j: (layer, 0, j)),
                  pl.BlockSpec((None, d, tf), lambda i, j: (layer, 0, n_f + j)),
                  pl.BlockSpec((CONV_W, tf), lambda i, j: (0, j)),
                  pl.BlockSpec((1, tf), lambda i, j: (0, j)),
                  pl.BlockSpec((tf, d), lambda i, j: (j, 0)),
                  pl.BlockSpec((1, prev_rows, tf), lambda i, j: (i // tiles_per_seq, 0, j)),
                  pl.BlockSpec((1, d), lambda i, j: (0, 0))],
        out_specs=[pl.BlockSpec((tm, d), lambda i, j: (i, 0)),
                   pl.BlockSpec((1, 2 * shift, tf), lambda i, j: (i, 0, j))],
        out_shape=[jax.ShapeDtypeStruct((m, d), F32),
                   jax.ShapeDtypeStruct((m // tm, 2 * shift, D_FF), F32)],
        scratch_shapes=[pltpu.VMEM((h_rows, d), BF16), pltpu.VMEM((n_f, prev_rows, tf), F32)],
        compiler_params=_params(2),
        name="outproj_convffn",
    )(x, mix, w_o, g_ffn, w_in, w_in, w_dw, b_dw, w_out, conv_state, g_final)


def _mla_latents(x_ref, g_ref, win_ref, gq_ref, gkv_ref, cos_ref, sin_ref, ckv_ref, kr_ref):
    h = _rms(x_ref[...], g_ref[...]).astype(BF16)
    p = jnp.dot(h, win_ref[...], preferred_element_type=F32)
    cq_n = _rms(p[:, 0:Q_LORA], gq_ref[...]).astype(BF16)
    ckv_n = _rms(p[:, Q_LORA:Q_LORA + KV_LORA], gkv_ref[...])
    ckv_ref[...] = ckv_n
    kr = _rope(p[:, Q_LORA + KV_LORA:], cos_ref[...], sin_ref[...])
    kr_ref[...] = kr[:, 0:ROPE]
    return cq_n, ckv_n, kr


def _q_heads(cq_n, wuq_ref, cos, sin, scale):
    n_nope = H_C * NOPE
    nope, rope = [], []
    for c0 in range(0, n_nope, HEAD_W):
        blk = jnp.dot(cq_n, wuq_ref[:, c0:c0 + HEAD_W], preferred_element_type=F32) * scale
        nope += [blk[:, :NOPE], blk[:, NOPE:]]
    low = lax.broadcasted_iota(jnp.int32, cos.shape, 1) < ROPE
    for c0 in range(n_nope, n_nope + H_C * ROPE, HEAD_W):
        blk = jnp.dot(cq_n, wuq_ref[:, c0:c0 + HEAD_W], preferred_element_type=F32) * scale
        for half in range(HEAD_W // LANES):
            pair = _rope(blk[:, half * LANES:(half + 1) * LANES], cos, sin)
            rope.append(jnp.where(low, pair, 0.0))
            rope.append(jnp.where(low, pltpu.roll(pair, ROPE, 1), 0.0))
    return nope, rope


def _mla_proj_prompt_kernel(x_ref, g_ref, win_ref, gq_ref, gkv_ref, wuq_ref, wukv_ref, cos_ref,
                            sin_ref, q_ref, k_ref, v_ref, ckv_ref, kr_ref, *, scale):
    cq_n, ckv_n, kr = _mla_latents(x_ref, g_ref, win_ref, gq_ref, gkv_ref, cos_ref, sin_ref,
                                   ckv_ref, kr_ref)
    q_nope, q_rope = _q_heads(cq_n, wuq_ref, cos_ref[...], sin_ref[...], scale)
    kr_b = kr.astype(BF16)
    ckv_b = ckv_n.astype(BF16)
    for hh in range(H_C):
        q_ref[hh, :, 0:NOPE] = q_nope[hh].astype(BF16)
        q_ref[hh, :, NOPE:HEAD_W] = q_rope[hh].astype(BF16)
        kvh = jnp.dot(ckv_b, wukv_ref[:, hh * HEAD_W:(hh + 1) * HEAD_W],
                      preferred_element_type=F32)
        k_ref[hh, :, 0:NOPE] = kvh[:, 0:NOPE].astype(BF16)
        k_ref[hh, :, NOPE:HEAD_W] = kr_b
        v_ref[hh] = kvh[:, NOPE:HEAD_W].astype(BF16)


def _mla_proj_prompt(x, g, w_in, g_q, g_kv, w_uq, w_ukv, cos, sin, seq, tm, scale):
    m, d = x.shape
    nt = seq // tm
    const2 = lambda shape: pl.BlockSpec(shape, lambda i: (0, 0))
    return pl.pallas_call(
        functools.partial(_mla_proj_prompt_kernel, scale=scale),
        grid=(m // tm,),
        in_specs=[pl.BlockSpec((tm, d), lambda i: (i, 0)), const2((1, d)), const2(w_in.shape),
                  const2((1, Q_LORA)), const2((1, KV_LORA)), const2(w_uq.shape), const2(w_ukv.shape),
                  pl.BlockSpec((tm, LANES), lambda i: (i % nt, 0)),
                  pl.BlockSpec((tm, LANES), lambda i: (i % nt, 0))],
        out_specs=[pl.BlockSpec((H_C, tm, HEAD_W), lambda i: (0, i, 0)),
                   pl.BlockSpec((H_C, tm, HEAD_W), lambda i: (0, i, 0)),
                   pl.BlockSpec((H_C, tm, V_C), lambda i: (0, i, 0)),
                   pl.BlockSpec((tm, KV_LORA), lambda i: (i, 0)),
                   pl.BlockSpec((tm, ROPE), lambda i: (i, 0))],
        out_shape=[jax.ShapeDtypeStruct((H_C, m, HEAD_W), BF16),
                   jax.ShapeDtypeStruct((H_C, m, HEAD_W), BF16),
                   jax.ShapeDtypeStruct((H_C, m, V_C), BF16),
                   jax.ShapeDtypeStruct((m, KV_LORA), F32),
                   jax.ShapeDtypeStruct((m, ROPE), F32)],
        compiler_params=_params(1),
        name="mla_proj_prompt",
    )(x, g, w_in, g_q, g_kv, w_uq, w_ukv, cos, sin)


def _mla_proj_sample_kernel(x_ref, g_ref, win_ref, gq_ref, gkv_ref, wuq_ref, wukt_ref, cos_ref,
                            sin_ref, qlat_ref, qr_ref, ckv_ref, kr_ref, *, scale):
    cq_n, _, _ = _mla_latents(x_ref, g_ref, win_ref, gq_ref, gkv_ref, cos_ref, sin_ref,
                              ckv_ref, kr_ref)
    q_nope, q_rope = _q_heads(cq_n, wuq_ref, cos_ref[...], sin_ref[...], scale)
    for hh in range(H_C):
        q_lat = jnp.dot(q_nope[hh].astype(BF16), wukt_ref[hh], preferred_element_type=F32)
        qlat_ref[hh] = q_lat.astype(BF16)
        qr_ref[hh] = q_rope[hh][:, 0:ROPE].astype(BF16)


def _mla_proj_sample(x, g, w_in, g_q, g_kv, w_uq, w_ukt, cos, sin, scale):
    m, d = x.shape
    const = lambda shape: pl.BlockSpec(shape, lambda i: (0,) * len(shape))
    return pl.pallas_call(
        functools.partial(_mla_proj_sample_kernel, scale=scale),
        grid=(1,),
        in_specs=[const((m, d)), const((1, d)), const(w_in.shape), const((1, Q_LORA)),
                  const((1, KV_LORA)), const(w_uq.shape), const(w_ukt.shape),
                  const((m, LANES)), const((m, LANES))],
        out_specs=[const((H_C, m, KV_LORA)), const((H_C, m, ROPE)), const((m, KV_LORA)),
                   const((m, ROPE))],
        out_shape=[jax.ShapeDtypeStruct((H_C, m, KV_LORA), BF16),
                   jax.ShapeDtypeStruct((H_C, m, ROPE), BF16),
                   jax.ShapeDtypeStruct((m, KV_LORA), F32),
                   jax.ShapeDtypeStruct((m, ROPE), F32)],
        compiler_params=_params(1),
        name="mla_proj_sample",
    )(x, g, w_in, g_q, g_kv, w_uq, w_ukt, cos, sin)


def _flash_kernel(q_ref, k_ref, v_ref, o_ref, m_s, acc_s, *, tile, n_tiles):
    t = tile
    row = lax.broadcasted_iota(jnp.int32, (t, t), 0)
    col = lax.broadcasted_iota(jnp.int32, (t, t), 1)
    visible = (col // CHUNK) <= (row // CHUNK)
    ones = jnp.ones((t, V_C), BF16)
    for j in range(n_tiles):
        ks = slice(j * t, (j + 1) * t)
        kj = k_ref[0, ks, :]
        vj = jnp.concatenate([v_ref[0, ks, :], ones], axis=1)
        for i in range(j, n_tiles):
            qs = slice(i * t, (i + 1) * t)
            s = _nt_dot(q_ref[0, qs, :], kj)
            if i == j:
                s = jnp.where(visible, s, -jnp.inf)
            m_new = jnp.broadcast_to(jnp.max(s, axis=-1, keepdims=True), (t, LANES))
            if j > 0:
                m_prev = m_s[qs]
                m_new = jnp.maximum(m_prev, m_new)
            p = jnp.exp2(s - jnp.concatenate([m_new] * (t // LANES), axis=1))
            pv = jnp.dot(p.astype(BF16), vj, preferred_element_type=F32)
            if j > 0:
                alpha = jnp.exp2(m_prev - m_new)
                pv = pv + jnp.concatenate([alpha, alpha], axis=1) * acc_s[qs]
            if i == j:
                o_ref[qs, :] = (pv[:, :V_C] / pv[:, V_C:]).astype(o_ref.dtype)
            else:
                m_s[qs] = m_new
                acc_s[qs] = pv


def _flash_prompt(q, k, v, batch, seq, tile):
    m = batch * seq
    return pl.pallas_call(
        functools.partial(_flash_kernel, tile=tile, n_tiles=seq // tile),
        grid=(batch, H_C),
        in_specs=[pl.BlockSpec((1, seq, HEAD_W), lambda b, h: (h, b, 0)),
                  pl.BlockSpec((1, seq, HEAD_W), lambda b, h: (h, b, 0)),
                  pl.BlockSpec((1, seq, V_C), lambda b, h: (h, b, 0))],
        out_specs=pl.BlockSpec((seq, V_C), lambda b, h: (b, h)),
        out_shape=jax.ShapeDtypeStruct((m, H_C * V_C), BF16),
        scratch_shapes=[pltpu.VMEM((seq, LANES), F32), pltpu.VMEM((seq, 2 * V_C), F32)],
        compiler_params=_params(2),
        name="flash_prompt",
    )(q, k, v)


def _latent_attn_kernel(qlat_ref, qr_ref, cckv_ref, ckr_ref, nckv_ref, nkr_ref, wuv_ref, o_ref,
                        *, steps, new_rows, tk):
    rows = H_C * steps
    q_lat = qlat_ref[...].reshape(rows, KV_LORA)
    q_r = qr_ref[...].reshape(rows, ROPE)

    def update(state, ckv, kr, valid, kr_transposed):
        ckv_b = ckv.astype(BF16)
        if kr_transposed:
            s_r = jnp.dot(q_r, kr.astype(BF16), preferred_element_type=F32)
        else:
            s_r = _nt_dot(q_r, kr.astype(BF16))
        s = _nt_dot(q_lat, ckv_b) + s_r
        if valid is not None:
            col = lax.broadcasted_iota(jnp.int32, s.shape, 1)
            s = jnp.where(col < valid, s, -jnp.inf)
        m_blk = jnp.max(s, axis=-1, keepdims=True)
        if state is None:
            m_new = m_blk
            p = jnp.exp(s - m_new)
            return (m_new, jnp.sum(p, axis=-1, keepdims=True),
                    jnp.dot(p.astype(BF16), ckv_b, preferred_element_type=F32))
        m_prev, l_prev, acc = state
        m_new = jnp.maximum(m_prev, m_blk)
        p = jnp.exp(s - m_new)
        alpha = jnp.exp(m_prev - m_new)
        return (m_new, alpha * l_prev + jnp.sum(p, axis=-1, keepdims=True),
                alpha * acc + jnp.dot(p.astype(BF16), ckv_b, preferred_element_type=F32))

    state = None
    for j in range(cckv_ref.shape[1] // tk):
        ks = slice(j * tk, (j + 1) * tk)
        state = update(state, cckv_ref[0, ks, :], ckr_ref[0, :, ks], None, True)
    _, l_fin, acc = update(state, nckv_ref[0], nkr_ref[0], new_rows, False)
    o_lat = (acc / l_fin).astype(BF16)
    for hh in range(H_C):
        o_ref[:, hh * V_C:(hh + 1) * V_C] = jnp.dot(
            o_lat[hh * steps:(hh + 1) * steps], wuv_ref[hh],
            preferred_element_type=F32).astype(o_ref.dtype)


def _latent_attn_sample(q_lat, q_r, cache_ckv, cache_kr, new_ckv, new_kr, w_uv, steps, tk):
    batch, past, _ = cache_ckv.shape
    pad_rows = new_ckv.shape[1]
    return pl.pallas_call(
        functools.partial(_latent_attn_kernel, steps=steps, new_rows=steps, tk=tk),
        grid=(batch,),
        in_specs=[pl.BlockSpec((H_C, steps, KV_LORA), lambda b: (0, b, 0)),
                  pl.BlockSpec((H_C, steps, ROPE), lambda b: (0, b, 0)),
                  pl.BlockSpec((1, past, KV_LORA), lambda b: (b, 0, 0)),
                  pl.BlockSpec((1, ROPE, past), lambda b: (b, 0, 0)),
                  pl.BlockSpec((1, pad_rows, KV_LORA), lambda b: (b, 0, 0)),
                  pl.BlockSpec((1, pad_rows, ROPE), lambda b: (b, 0, 0)),
                  pl.BlockSpec((H_C, KV_LORA, V_C), lambda b: (0, 0, 0))],
        out_specs=pl.BlockSpec((steps, H_C * V_C), lambda b: (b, 0)),
        out_shape=jax.ShapeDtypeStruct((batch * steps, H_C * V_C), BF16),
        compiler_params=_params(1),
        name="latent_attn_sample",
    )(q_lat, q_r, cache_ckv, cache_kr, new_ckv, new_kr, w_uv)


def _rope_tables(pos, width):
    half = ROPE // 2
    freqs = ROPE_THETA ** (-jnp.arange(half, dtype=F32) / half)
    ang = pos.astype(F32)[:, None] * freqs[None, :]
    cos, sin = jnp.cos(ang), jnp.sin(ang)
    cos_h = jnp.concatenate([cos, cos], axis=-1)
    sin_h = jnp.concatenate([-sin, sin], axis=-1)
    reps = width // ROPE
    return jnp.tile(cos_h, (1, reps)), jnp.tile(sin_h, (1, reps))


def _prep_even(w_in_ab, w_gate_up):
    sizes = [H_A * DK_A, H_A * DK_A, H_A * DV_A, H_A * DV_A, GATE_RANK,
             H_B * DK_B, H_B * DK_B, H_B * DV_B, H_B * DV_B]
    offs = np.concatenate([[0], np.cumsum(sizes)])
    parts = [w_in_ab[:, offs[n]:offs[n + 1]] for n in range(9)]
    lo = jnp.pad(parts[4], ((0, 0), (0, LANES - GATE_RANK)))
    w = jnp.concatenate(parts[0:4] + parts[5:9] + [lo], axis=1).astype(BF16)
    w_gu = jnp.pad(w_gate_up, ((0, LANES - GATE_RANK), (0, 0))).astype(BF16)
    return w, w_gu


def _prep_odd(w_in_c, w_uq, w_uk, w_uv):
    d = w_in_c.shape[0]
    w_in = jnp.pad(w_in_c, ((0, 0), (0, LANES - ROPE))).astype(BF16)
    uq = w_uq.reshape(Q_LORA, H_C, NOPE + ROPE)
    w_uq_p = jnp.concatenate([uq[:, :, :NOPE].reshape(Q_LORA, H_C * NOPE),
                              uq[:, :, NOPE:].reshape(Q_LORA, H_C * ROPE)],
                             axis=1).astype(BF16)
    w_ukv = jnp.concatenate([w_uk, w_uv], axis=2).reshape(KV_LORA, H_C * HEAD_W).astype(BF16)
    w_ukt = jnp.transpose(w_uk, (1, 2, 0)).astype(BF16)
    w_uvh = jnp.transpose(w_uv, (1, 0, 2)).astype(BF16)
    del d
    return w_in, w_uq_p, w_ukv, w_ukt, w_uvh


def _log_gamma_row():
    lg = np.log1p(-np.exp2(-5.0 - np.arange(H_B, dtype=np.float32))).astype(np.float32)
    return jnp.asarray(np.repeat(lg, DK_B)[None, :])


def _to_time_major(a, batch, steps):
    return a.reshape(batch, steps, -1).transpose(1, 0, 2).reshape(batch * steps, -1)


def _to_batch_major(a, batch, steps):
    return a.reshape(steps, batch, -1).transpose(1, 0, 2).reshape(batch * steps, -1)


def _divisor_tile(n, pref):
    t = min(n, pref)
    while n % t:
        t //= 2
    return t


def _trunk(x, pos0, gla0, ret0, ckv_past, kr_past, conv0, w, time_major_ffn):
    batch, seq, d = x.shape
    m = batch * seq
    pos = pos0 + jnp.arange(seq, dtype=jnp.int32)
    row = lambda v: v.reshape(1, -1)
    scale = float((NOPE + ROPE) ** -0.5)
    tf = TF_FFN
    tm_proj = _divisor_tile(m, TM_PROJ)

    if time_major_ffn:
        ffn_kw = dict(tm=m, tf=tf, shift=batch, tiles_per_seq=1)
        to_ffn = lambda a: _to_time_major(a, batch, seq)
        from_ffn = lambda a: _to_batch_major(a, batch, seq)
        prep_state = lambda s: s.transpose(1, 0, 2).reshape(1, 2 * batch, D_FF)
        post_state = lambda s: s.reshape(2, batch, D_FF).transpose(1, 0, 2)
    else:
        tm_ffn = _divisor_tile(seq, TM_FFN)
        ffn_kw = dict(tm=tm_ffn, tf=tf, shift=1, tiles_per_seq=seq // tm_ffn)
        to_ffn = from_ffn = lambda a: a
        prep_state = lambda s: jnp.pad(s, ((0, 0), (SUBLANES - 2, 0), (0, 0)))
        post_state = lambda s: s

    xf = x.reshape(m, d)
    conv_new = []
    gla_new = ret_new = ckv_new = kr_new = None
    for layer in range(2):
        g_mix = row(w['norm_mix'][layer])
        if layer == 0:
            w_ab, w_gu = _prep_even(w['w_in_ab'][0], w['w_gate_up'][0])
            cos, sin = _rope_tables(pos, 4 * DK_B)
            rows = _divisor_tile(seq, ROWS_LINATTN)
            small = (w_gu, row(w['b_gate'][0]), row(w['g_gla'][0]), row(w['g_ret'][0]), cos, sin,
                     _log_gamma_row())
            if gla0 is None and rows >= 2 * CHUNK:
                mix, gla_new, ret_new = _proj_linattn(xf, g_mix, w_ab, batch, seq, rows, *small)
            else:
                slab = _norm_matmul(xf, g_mix, w_ab, tm_proj, 4 * HEAD_W)
                mix, gla_new, ret_new = _linattn(
                    slab, batch, seq, rows, *small,
                    None if gla0 is None else gla0[0], None if ret0 is None else ret0[0])
            w_o = w['w_out_ab'][0].astype(BF16)
        else:
            w_in, w_uq_p, w_ukv, w_ukt, w_uvh = _prep_odd(w['w_in_c'][0], w['w_uq'][0], w['w_uk'][0],
                                                          w['w_uv'][0])
            cos, sin = _rope_tables(pos, LANES)
            g_q, g_kv = row(w['g_q'][0]), row(w['g_kv'][0])
            if ckv_past is None:
                q, k, v, ckv_new, kr_new = _mla_proj_prompt(
                    xf, g_mix, w_in, g_q, g_kv, w_uq_p, w_ukv, cos, sin, seq, tm_proj,
                    scale * float(np.log2(np.e)))
                mix = _flash_prompt(q, k, v, batch, seq, _divisor_tile(seq, TQ_FLASH))
            else:
                assert ckv_past.shape[2] % CHUNK == 0 and seq <= CHUNK
                cos_m, sin_m = jnp.tile(cos, (batch, 1)), jnp.tile(sin, (batch, 1))
                q_lat, q_r, ckv_new, kr_new = _mla_proj_sample(
                    xf, g_mix, w_in, g_q, g_kv, w_uq_p, w_ukt, cos_m, sin_m, scale)
                pad = ((0, 0), (0, LANES - seq), (0, 0))
                new_ckv = jnp.pad(ckv_new.reshape(batch, seq, KV_LORA), pad)
                new_kr = jnp.pad(kr_new.reshape(batch, seq, ROPE), pad)
                mix = _latent_attn_sample(q_lat, q_r, ckv_past[0],
                                          jnp.swapaxes(kr_past[0], 1, 2), new_ckv, new_kr,
                                          w_uvh, seq, _divisor_tile(ckv_past.shape[2], TK_LATENT))
            w_o = w['w_out_c'][0].astype(BF16)

        if conv0 is None:
            state = jnp.zeros((batch, SUBLANES, D_FF), F32) if not time_major_ffn else \
                jnp.zeros((1, 2 * batch, D_FF), F32)
        else:
            state = prep_state(conv0[layer])
        y, conv_rows = _ffn(to_ffn(xf), to_ffn(mix), w_o, row(w['norm_ffn'][layer]),
                            w['w_ffn_in'].astype(BF16), w['w_dwconv'][layer],
                            row(w['b_dwconv'][layer]),
                            (0.5 * w['w_ffn_out'][layer]).astype(BF16), state,
                            row(w['norm_final']), layer=layer, final_norm=(layer == 1), **ffn_kw)
        xf = from_ffn(y)
        tps = ffn_kw['tiles_per_seq']
        conv_new.append(post_state(conv_rows[tps - 1::tps]))

    return (xf.reshape(batch, seq, d), gla_new[None], ret_new[None],
            ckv_new.reshape(1, batch, seq, KV_LORA), kr_new.reshape(1, batch, seq, ROPE),
            jnp.stack(conv_new))


def kernel(x_prompt, x_sample, state_gla, state_ret, cache_ckv, cache_krope, state_conv, norm_mix, norm_ffn, norm_final, w_in_ab, w_gate_up, b_gate, g_gla, g_ret, w_out_ab, w_in_c, g_q, g_kv, w_uq, w_uk, w_uv, w_out_c, w_ffn_in, w_dwconv, b_dwconv, w_ffn_out):
    w = {'norm_mix': norm_mix, 'norm_ffn': norm_ffn, 'norm_final': norm_final,
         'w_in_ab': w_in_ab, 'w_gate_up': w_gate_up, 'b_gate': b_gate, 'g_gla': g_gla,
         'g_ret': g_ret, 'w_out_ab': w_out_ab, 'w_in_c': w_in_c, 'g_q': g_q, 'g_kv': g_kv,
         'w_uq': w_uq, 'w_uk': w_uk, 'w_uv': w_uv, 'w_out_c': w_out_c, 'w_ffn_in': w_ffn_in,
         'w_dwconv': w_dwconv, 'b_dwconv': b_dwconv, 'w_ffn_out': w_ffn_out}
    past_len = cache_ckv.shape[2]
    y_p, gla_p, ret_p, ckv_p, kr_p, conv_p = _trunk(
        x_prompt, 0, None, None, None, None, None, w, time_major_ffn=False)
    y_s, gla_s, ret_s, ckv_s, kr_s, conv_s = _trunk(
        x_sample, past_len, state_gla, state_ret, cache_ckv, cache_krope, state_conv, w,
        time_major_ffn=True)
    return (y_p, y_s, gla_p, gla_s, ret_p, ret_s, ckv_p, ckv_s, kr_p, kr_s, conv_p, conv_s)
```

```python
import functools

import numpy as np
import jax
import jax.numpy as jnp
from jax import lax
from jax.experimental import pallas as pl
from jax.experimental.pallas import tpu as pltpu

F32 = jnp.float32
BF16 = jnp.bfloat16

D_MODEL = 1024
CHUNK = 64
EPS = 1e-6
ROPE_THETA = 10000.0
H_A, DK_A, DV_A = 4, 64, 128
GATE_RANK = 16
GATE_TAU = 16.0
H_B, DK_B, DV_B = 4, 64, 128
H_C = 8
Q_LORA, KV_LORA, NOPE, ROPE, V_C = 384, 512, 128, 64, 128
D_FF = 2816
CONV_W = 3

LANES = 128
SUBLANES = 8
VMEM_LIMIT = 56 * 1024 * 1024
AB_COLS = 3200
HEAD_W = 2 * LANES
TM_PROJ = 1024
TM_FFN = 1024
TF_FFN = D_FF
SUB_FFN = 512
ROWS_LINATTN = 256
TQ_FLASH = 256
TK_LATENT = 4096


def _params(n_axes):
    return pltpu.CompilerParams(dimension_semantics=("arbitrary",) * n_axes,
                                vmem_limit_bytes=VMEM_LIMIT)


def _rms(x, g):
    return x * lax.rsqrt(jnp.mean(x * x, axis=-1, keepdims=True) + EPS) * g


def _rope_first_half(shape):
    lane = lax.broadcasted_iota(jnp.int32, shape, 1)
    return (lane % ROPE) < ROPE // 2


def _rope(x, cos, sin_signed, first=None):
    w = x.shape[1]
    half = ROPE // 2
    if first is None:
        first = _rope_first_half(x.shape)
    swapped = jnp.where(first, pltpu.roll(x, w - half, 1), pltpu.roll(x, half, 1))
    return x * cos + swapped * sin_signed


def _nt_dot(a, b):
    return lax.dot_general(a, b, (((1,), (1,)), ((), ())), preferred_element_type=F32)


def _norm_matmul_kernel(x_ref, g_ref, w_ref, o_ref, *, col_chunk):
    h = _rms(x_ref[...], g_ref[...]).astype(BF16)
    n = w_ref.shape[1]
    for c0 in range(0, n, col_chunk):
        c1 = min(c0 + col_chunk, n)
        o_ref[:, c0:c1] = jnp.dot(h, w_ref[:, c0:c1], preferred_element_type=F32).astype(o_ref.dtype)


def _norm_matmul(x, g, w, tm, col_chunk):
    m, d = x.shape
    n = w.shape[1]
    return pl.pallas_call(
        functools.partial(_norm_matmul_kernel, col_chunk=col_chunk),
        grid=(m // tm,),
        in_specs=[pl.BlockSpec((tm, d), lambda i: (i, 0)),
                  pl.BlockSpec((1, d), lambda i: (0, 0)),
                  pl.BlockSpec((d, n), lambda i: (0, 0))],
        out_specs=pl.BlockSpec((tm, n), lambda i: (i, 0)),
        out_shape=jax.ShapeDtypeStruct((m, n), BF16),
        compiler_params=_params(1),
        name="norm_inproj_ab",
    )(x, g, w)


def _decay_attention_streams(streams, mask_k, mask_v, n_chunks, mid, last, between):
    c = CHUNK
    assert 2 * c == LANES
    hk, hv = 4 * DK_A, 4 * DV_A
    rt = lax.broadcasted_iota(jnp.int32, (c, 4 * c), 0)
    ct = lax.broadcasted_iota(jnp.int32, (c, 4 * c), 1)
    causal = (ct % c) <= rt
    lane = lax.broadcasted_iota(jnp.int32, (DK_A, LANES), 1)
    zero_v = jnp.zeros((LANES - c, hv), BF16)
    zero_blk = jnp.zeros((DK_A, DV_A), BF16)
    log2e = np.float32(np.log2(np.e))

    items = []
    for q, k, v, bcum, s_ref in streams:
        bcum2 = bcum * log2e
        for n in range(n_chunks):
            sl = slice(n * c, (n + 1) * c)
            qn, kn, vn, bn = q[sl], k[sl], v[sl], bcum2[sl]
            b_ref = bn[mid:mid + 1]
            b_last = bn[last:last + 1]
            it = dict(s_ref=s_ref, vn=vn)
            it['q_rel'] = (qn * jnp.exp2(bn - b_ref)).astype(BF16)
            k_rel = (kn * jnp.exp2(b_ref - bn)).astype(BF16)
            it['k_bd'] = jnp.concatenate([k_rel] * 4, axis=0) * mask_k
            it['q_dec'] = (qn * jnp.exp2(bn)).astype(BF16)
            k_dec = kn * jnp.exp2(b_last - bn)
            decay_rows = jnp.broadcast_to(jnp.exp2(b_last), (LANES - c, hk))
            it['kt'] = jnp.concatenate([k_dec, decay_rows], axis=0).T
            items.append(it)
    between()
    for it in items:
        it['scores'] = jnp.where(causal, _nt_dot(it['q_rel'], it['k_bd']), 0.0).astype(BF16)
    between()
    for it in items:
        v_bd = jnp.concatenate([it['vn']] * 4, axis=0) * mask_v
        it['o'] = jnp.dot(it['scores'], v_bd, preferred_element_type=F32)
        kt_b = it['kt'].astype(BF16)
        v_pad = jnp.concatenate([it['vn'], zero_v], axis=0)
        it['kv'] = [jnp.dot(kt_b[h * DK_A:(h + 1) * DK_A], v_pad[:, h * DV_A:(h + 1) * DV_A],
                            preferred_element_type=F32) for h in range(4)]
    between()
    for it in items:
        s_ref = it['s_ref']
        s_old = [s_ref[h] for h in range(4)]
        it['s_bd'] = jnp.concatenate(
            [jnp.concatenate([zero_blk] * h + [s_old[h].astype(BF16)] + [zero_blk] * (3 - h), axis=1)
             for h in range(4)], axis=0)
        for h in range(4):
            kth = it['kt'][h * DK_A:(h + 1) * DK_A]
            decay = jnp.where(lane < c, pltpu.roll(kth, c, 1), kth)
            s_ref[h] = s_old[h] * decay + it['kv'][h]
    between()
    outs = []
    for si in range(len(streams)):
        chunk_out = [it['o'] + jnp.dot(it['q_dec'], it['s_bd'], preferred_element_type=F32)
                     for it in items[si * n_chunks:(si + 1) * n_chunks]]
        outs.append(chunk_out[0] if n_chunks == 1 else jnp.concatenate(chunk_out, axis=0))
    between()
    return outs


_SLAB = dict(qa=(0, 256), ka=(256, 256), va=(512, 512), ra=(1024, 512),
             qb=(1536, 256), kb=(1792, 256), vb=(2048, 512), gb=(2560, 512), lo=(3072, LANES))


def _linattn_compute(cols, wgu_ref, bg_ref, gg_ref, gr_ref, cos_ref, sin_ref, lgam_ref, tri_ref,
                     mk_ref, mv_ref, mix_ref, sa, sb, rows, between=lambda: None):
    c = CHUNK
    padded = max(rows, c)
    n_chunks = padded // c
    valid = min(rows, c)
    mid, last = (valid - 1) // 2, valid - 1

    def pad_rows(x):
        if padded == rows:
            return x
        return jnp.concatenate([x, jnp.zeros((padded - rows, x.shape[1]), x.dtype)], axis=0)

    tri = tri_ref[...]
    mask_k, mask_v = mk_ref[...], mv_ref[...]

    gate = jnp.dot(cols['lo'], wgu_ref[...], preferred_element_type=F32) + bg_ref[...]
    log_a = (jnp.minimum(gate, 0.0) - jnp.log1p(jnp.exp(-jnp.abs(gate)))) / GATE_TAU
    log_a = pad_rows(log_a)
    hi = log_a.astype(BF16)
    lo = (log_a - hi.astype(F32)).astype(BF16)
    bcum_a = (jnp.dot(tri, hi, preferred_element_type=F32)
              + jnp.dot(tri, lo, preferred_element_type=F32))
    qa = pad_rows(cols['qa'].astype(F32) * DK_A ** -0.5)
    ka = pad_rows(cols['ka'].astype(F32))
    va = pad_rows(cols['va'])
    between()

    cos, sin = cos_ref[...], sin_ref[...]
    first = _rope_first_half(cos.shape)
    qb = pad_rows(_rope(cols['qb'].astype(F32), cos, sin, first))
    kb = pad_rows(_rope(cols['kb'].astype(F32), cos, sin, first) * DK_B ** -0.5)
    vb = pad_rows(cols['vb'])
    pos = lax.broadcasted_iota(jnp.int32, (padded, 4 * DK_B), 0) % c
    steps = jnp.where(pos < valid, pos + 1, valid).astype(F32)
    bcum_b = steps * lgam_ref[...]
    o_a, o_b = _decay_attention_streams(
        [(qa, ka, va, bcum_a, sa), (qb, kb, vb, bcum_b, sb)], mask_k, mask_v, n_chunks, mid, last,
        between)
    o_a, o_b = o_a[:rows], o_b[:rows]

    gg, gr = gg_ref[...], gr_ref[...]
    for h in range(4):
        sl = slice(h * DV_A, (h + 1) * DV_A)
        oh = o_a[:, sl]
        oh = oh * lax.rsqrt(jnp.mean(oh * oh, axis=-1, keepdims=True) + EPS) * gg[:, sl]
        r = cols['ra'][:, sl].astype(F32)
        mix_ref[:, sl] = (oh * (r * jax.nn.sigmoid(r))).astype(mix_ref.dtype)
        ob = o_b[:, sl]
        ob = ob - jnp.mean(ob, axis=-1, keepdims=True)
        ob = ob * lax.rsqrt(jnp.mean(ob * ob, axis=-1, keepdims=True) + EPS) * gr[:, sl]
        gb = cols['gb'][:, sl].astype(F32)
        mix_ref[:, 4 * DV_A + h * DV_B:4 * DV_A + (h + 1) * DV_B] = (
            ob * (gb * jax.nn.sigmoid(gb))).astype(mix_ref.dtype)


def _linattn_kernel(qa_ref, ka_ref, va_ref, ra_ref, qb_ref, kb_ref, vb_ref, gb_ref, lo_ref,
                    wgu_ref, bg_ref, gg_ref, gr_ref, cos_ref, sin_ref, lgam_ref, tri_ref, mk_ref,
                    mv_ref, *rest, rows, has_state):
    if has_state:
        s0a_ref, s0b_ref, mix_ref, sa_out, sb_out, sa, sb = rest
    else:
        mix_ref, sa_out, sb_out, sa, sb = rest
    t = pl.program_id(1)

    @pl.when(t == 0)
    def _init():
        if has_state:
            sa[...] = s0a_ref[0]
            sb[...] = s0b_ref[0]
        else:
            sa[...] = jnp.zeros_like(sa)
            sb[...] = jnp.zeros_like(sb)

    cols = dict(qa=qa_ref[...], ka=ka_ref[...], va=va_ref[...], ra=ra_ref[...], qb=qb_ref[...],
                kb=kb_ref[...], vb=vb_ref[...], gb=gb_ref[...], lo=lo_ref[...])
    _linattn_compute(cols, wgu_ref, bg_ref, gg_ref, gr_ref, cos_ref, sin_ref, lgam_ref, tri_ref,
                     mk_ref, mv_ref, mix_ref, sa, sb, rows)

    @pl.when(t == pl.num_programs(1) - 1)
    def _emit_state():
        sa_out[0] = sa[...]
        sb_out[0] = sb[...]


def _proj_linattn_kernel(x_ref, g_ref, w_ref, wgu_ref, bg_ref, gg_ref, gr_ref, cos_ref, sin_ref,
                         lgam_ref, tri_ref, mk_ref, mv_ref, mix_ref, sa_out, sb_out, slab_even,
                         slab_odd, sa, sb, *, rows, nt, col_chunk):
    s = pl.program_id(0)
    t = jnp.maximum(s - 1, 0) % nt

    @pl.when(s == 0)
    def _zero_slab():
        slab_odd[...] = jnp.zeros_like(slab_odd)

    @pl.when(t == 0)
    def _init():
        sa[...] = jnp.zeros_like(sa)
        sb[...] = jnp.zeros_like(sb)

    def step(write_ref, read_ref):
        cols = {name: read_ref[:, off:off + width] for name, (off, width) in _SLAB.items()}
        h = _rms(x_ref[...], g_ref[...]).astype(BF16)
        n = w_ref.shape[1]
        pieces = list(range(0, n, col_chunk))
        shares = (5, 1, 0, 1, 1, 5)
        total = len(pieces)
        quotas = [total * sum(shares[:k + 1]) // sum(shares) - total * sum(shares[:k]) // sum(shares)
                  for k in range(len(shares))]

        def project(count):
            for _ in range(min(count, len(pieces))):
                c0 = pieces.pop(0)
                c1 = min(c0 + col_chunk, n)
                write_ref[:, c0:c1] = jnp.dot(h, w_ref[:, c0:c1],
                                              preferred_element_type=F32).astype(write_ref.dtype)

        _linattn_compute(cols, wgu_ref, bg_ref, gg_ref, gr_ref, cos_ref, sin_ref, lgam_ref,
                         tri_ref, mk_ref, mv_ref, mix_ref, sa, sb, rows,
                         between=lambda: project(quotas.pop(0) if quotas else 0))
        project(len(pieces))

    @pl.when(s % 2 == 0)
    def _even():
        step(slab_even, slab_odd)

    @pl.when(s % 2 == 1)
    def _odd():
        step(slab_odd, slab_even)

    @pl.when(t == nt - 1)
    def _emit_state():
        sa_out[0] = sa[...]
        sb_out[0] = sb[...]


def _linattn_masks(rows):
    c = CHUNK
    padded = max(rows, c)
    r = np.arange(padded)
    tri = jnp.asarray(((r[:, None] // c == r[None, :] // c) & (r[None, :] <= r[:, None])), BF16)
    r4 = np.arange(4 * c)
    mask_k = jnp.asarray(r4[:, None] // c == np.arange(4 * DK_A)[None, :] // DK_A, BF16)
    mask_v = jnp.asarray(r4[:, None] // c == np.arange(4 * DV_A)[None, :] // DV_A, BF16)
    return tri, mask_k, mask_v


def _proj_linattn(x, g, w_ab, batch, seq, rows, w_gu, b_gate, g_gla, g_ret, cos, sin, lgam):
    m, d = x.shape
    nt = seq // rows
    n_tiles = batch * nt
    tri, mask_k, mask_v = _linattn_masks(rows)
    hk, hv = 4 * DK_A, 4 * DV_A
    prev = lambda s: jnp.maximum(s - 1, 0)

    def const(shape):
        return pl.BlockSpec(shape, lambda s: (0,) * len(shape))

    state_spec = pl.BlockSpec((1, 4, DK_A, DV_A), lambda s: (prev(s) // nt, 0, 0, 0))
    state_shape = jax.ShapeDtypeStruct((batch, 4, DK_A, DV_A), F32)
    return pl.pallas_call(
        functools.partial(_proj_linattn_kernel, rows=rows, nt=nt, col_chunk=HEAD_W),
        grid=(n_tiles + 1,),
        in_specs=[pl.BlockSpec((rows, d), lambda s: (jnp.minimum(s, n_tiles - 1), 0)),
                  const((1, d)), const(w_ab.shape),
                  const((LANES, hk)), const((1, hk)), const((1, hv)), const((1, hv)),
                  pl.BlockSpec((rows, hk), lambda s: (prev(s) % nt, 0)),
                  pl.BlockSpec((rows, hk), lambda s: (prev(s) % nt, 0)),
                  const((1, hk)), const(tri.shape), const(mask_k.shape), const(mask_v.shape)],
        out_specs=[pl.BlockSpec((rows, 2 * hv), lambda s: (prev(s), 0)), state_spec, state_spec],
        out_shape=[jax.ShapeDtypeStruct((m, 2 * hv), BF16), state_shape, state_shape],
        scratch_shapes=[pltpu.VMEM((rows, AB_COLS), BF16), pltpu.VMEM((rows, AB_COLS), BF16),
                        pltpu.VMEM((4, DK_A, DV_A), F32), pltpu.VMEM((4, DK_B, DV_B), F32)],
        compiler_params=_params(1),
        name="inproj_decay_linear_attention",
    )(x, g, w_ab, w_gu, b_gate, g_gla, g_ret, cos, sin, lgam, tri, mask_k, mask_v)


def _linattn(slab, batch, seq, rows, w_gu, b_gate, g_gla, g_ret, cos, sin, lgam, s0a, s0b):
    nt = seq // rows
    has_state = s0a is not None
    tri, mask_k, mask_v = _linattn_masks(rows)

    def col(width, idx):
        return pl.BlockSpec((rows, width), lambda b, t, idx=idx: (b * nt + t, idx))

    def const(shape):
        return pl.BlockSpec(shape, lambda b, t: (0,) * len(shape))

    hk = 4 * DK_A
    hv = 4 * DV_A
    in_specs = [col(hk, 0), col(hk, 1), col(hv, 1), col(hv, 2),
                col(hk, 6), col(hk, 7), col(hv, 4), col(hv, 5),
                col(LANES, 24),
                const((LANES, hk)), const((1, hk)), const((1, hv)), const((1, hv)),
                pl.BlockSpec((rows, hk), lambda b, t: (t, 0)),
                pl.BlockSpec((rows, hk), lambda b, t: (t, 0)),
                const((1, hk)), const(tri.shape), const(mask_k.shape), const(mask_v.shape)]
    args = [slab] * 9 + [w_gu, b_gate, g_gla, g_ret, cos, sin, lgam, tri, mask_k, mask_v]
    state_spec = pl.BlockSpec((1, 4, DK_A, DV_A), lambda b, t: (b, 0, 0, 0))
    if has_state:
        in_specs += [state_spec, state_spec]
        args += [s0a, s0b]
    state_shape = jax.ShapeDtypeStruct((batch, 4, DK_A, DV_A), F32)
    return pl.pallas_call(
        functools.partial(_linattn_kernel, rows=rows, has_state=has_state),
        grid=(batch, nt),
        in_specs=in_specs,
        out_specs=[pl.BlockSpec((rows, 2 * hv), lambda b, t: (b * nt + t, 0)), state_spec, state_spec],
        out_shape=[jax.ShapeDtypeStruct((batch * seq, 2 * hv), BF16), state_shape, state_shape],
        scratch_shapes=[pltpu.VMEM((4, DK_A, DV_A), F32), pltpu.VMEM((4, DK_B, DV_B), F32)],
        compiler_params=_params(2),
        name="decay_linear_attention",
    )(*args)


def _ffn_kernel(x_ref, mix_ref, wo_ref, g_ref, wa_ref, wu_ref, wdw_ref, bdw_ref, wout_ref, st_ref,
                gf_ref, y_ref, cv_ref, h_s, carry_s,
                *, tm, sub, prev_rows, shift, tiles_per_seq, final_norm, single_step):
    i = pl.program_id(0)
    j = pl.program_id(1)
    p = prev_rows
    subs = [slice(r, r + sub) for r in range(0, tm, sub)]

    def mid_residual(rs):
        xm = x_ref[rs] + jnp.dot(mix_ref[rs], wo_ref[...], preferred_element_type=F32)
        return xm, _rms(xm, g_ref[...]).astype(BF16)

    if not single_step:
        @pl.when(j == 0)
        def _start():
            for rs in subs:
                y_ref[rs], h_s[rs] = mid_residual(rs)

    at_start = (i % tiles_per_seq) == 0
    tail = jnp.where(at_start, st_ref[0], carry_s[j])
    last = j == pl.num_programs(1) - 1
    wdw = wdw_ref[...]
    bdw = bdw_ref[...]
    a = None
    for rs in subs:
        if single_step:
            base, h = mid_residual(rs)
        else:
            base, h = y_ref[rs], h_s[rs]
        a = jnp.dot(h, wa_ref[...], preferred_element_type=F32)
        u = jnp.dot(h, wu_ref[...], preferred_element_type=F32)
        ext = jnp.concatenate([tail, a], axis=0)
        conv = bdw + pltpu.roll(ext, 2 * shift, 0)[p:] * wdw[0:1]
        conv = conv + pltpu.roll(ext, shift, 0)[p:] * wdw[1:2]
        conv = conv + a * wdw[2:3]
        act = conv * (1.0 + lax.erf(conv * np.float32(np.sqrt(0.5)))) * u
        out = base + jnp.dot(act.astype(BF16), wout_ref[...], preferred_element_type=F32)
        if final_norm:
            normed = _rms(out, gf_ref[...])
            out = normed if single_step else jnp.where(last, normed, out)
        y_ref[rs] = out
        tail = a[sub - p:]
    carry_s[j] = tail
    cv_ref[0] = a[sub - 2 * shift:]


def _ffn(x, mix, w_o, g_ffn, w_in, w_dw, b_dw, w_out, conv_state, g_final, *, layer, tm, tf, shift,
         tiles_per_seq, final_norm):
    m, d = x.shape
    n_f = D_FF // tf
    n_seq, prev_rows, _ = conv_state.shape
    kern = functools.partial(_ffn_kernel, tm=tm, sub=min(tm, SUB_FFN), prev_rows=prev_rows,
                             shift=shift, tiles_per_seq=tiles_per_seq, final_norm=final_norm,
                             single_step=(n_f == 1))
    h_rows = SUBLANES * 2 if n_f == 1 else tm
    return pl.pallas_call(
        kern,
        grid=(m // tm, n_f),
        in_specs=[pl.BlockSpec((tm, d), lambda i, j: (i, 0)),
                  pl.BlockSpec((tm, d), lambda i, j: (i, 0)),
                  pl.BlockSpec((d, d), lambda i, j: (0, 0)),
                  pl.BlockSpec((1, d), lambda i, j: (0, 0)),
                  pl.BlockSpec((None, d, tf), lambda i, j: (layer, 0, j)),
                  pl.BlockSpec((None, d, tf), lambda i, j: (layer, 0, n_f + j)),
                  pl.BlockSpec((CONV_W, tf), lambda i, j: (0, j)),
                  pl.BlockSpec((1, tf), lambda i, j: (0, j)),
                  pl.BlockSpec((tf, d), lambda i, j: (j, 0)),
                  pl.BlockSpec((1, prev_rows, tf), lambda i, j: (i // tiles_per_seq, 0, j)),
                  pl.BlockSpec((1, d), lambda i, j: (0, 0))],
        out_specs=[pl.BlockSpec((tm, d), lambda i, j: (i, 0)),
                   pl.BlockSpec((1, 2 * shift, tf), lambda i, j: (i, 0, j))],
        out_shape=[jax.ShapeDtypeStruct((m, d), F32),
                   jax.ShapeDtypeStruct((m // tm, 2 * shift, D_FF), F32)],
        scratch_shapes=[pltpu.VMEM((h_rows, d), BF16), pltpu.VMEM((n_f, prev_rows, tf), F32)],
        compiler_params=_params(2),
        name="outproj_convffn",
    )(x, mix, w_o, g_ffn, w_in, w_in, w_dw, b_dw, w_out, conv_state, g_final)


def _mla_latents(x_ref, g_ref, win_ref, gq_ref, gkv_ref, cos_ref, sin_ref, ckv_ref, kr_ref):
    h = _rms(x_ref[...], g_ref[...]).astype(BF16)
    p = jnp.dot(h, win_ref[...], preferred_element_type=F32)
    cq_n = _rms(p[:, 0:Q_LORA], gq_ref[...]).astype(BF16)
    ckv_n = _rms(p[:, Q_LORA:Q_LORA + KV_LORA], gkv_ref[...])
    ckv_ref[...] = ckv_n
    kr = _rope(p[:, Q_LORA + KV_LORA:], cos_ref[...], sin_ref[...])
    kr_ref[...] = kr[:, 0:ROPE]
    return cq_n, ckv_n, kr


def _q_heads(cq_n, wuq_ref, cos, sin, scale):
    n_nope = H_C * NOPE
    nope, rope = [], []
    for c0 in range(0, n_nope, HEAD_W):
        blk = jnp.dot(cq_n, wuq_ref[:, c0:c0 + HEAD_W], preferred_element_type=F32) * scale
        nope += [blk[:, :NOPE], blk[:, NOPE:]]
    low = lax.broadcasted_iota(jnp.int32, cos.shape, 1) < ROPE
    for c0 in range(n_nope, n_nope + H_C * ROPE, HEAD_W):
        blk = jnp.dot(cq_n, wuq_ref[:, c0:c0 + HEAD_W], preferred_element_type=F32) * scale
        for half in range(HEAD_W // LANES):
            pair = _rope(blk[:, half * LANES:(half + 1) * LANES], cos, sin)
            rope.append(jnp.where(low, pair, 0.0))
            rope.append(jnp.where(low, pltpu.roll(pair, ROPE, 1), 0.0))
    return nope, rope


def _mla_proj_prompt_kernel(x_ref, g_ref, win_ref, gq_ref, gkv_ref, wuq_ref, wukv_ref, cos_ref,
                            sin_ref, q_ref, k_ref, v_ref, ckv_ref, kr_ref, *, scale):
    cq_n, ckv_n, kr = _mla_latents(x_ref, g_ref, win_ref, gq_ref, gkv_ref, cos_ref, sin_ref,
                                   ckv_ref, kr_ref)
    q_nope, q_rope = _q_heads(cq_n, wuq_ref, cos_ref[...], sin_ref[...], scale)
    kr_b = kr.astype(BF16)
    ckv_b = ckv_n.astype(BF16)
    for hh in range(H_C):
        q_ref[hh, :, 0:NOPE] = q_nope[hh].astype(BF16)
        q_ref[hh, :, NOPE:HEAD_W] = q_rope[hh].astype(BF16)
        kvh = jnp.dot(ckv_b, wukv_ref[:, hh * HEAD_W:(hh + 1) * HEAD_W],
                      preferred_element_type=F32)
        k_ref[hh, :, 0:NOPE] = kvh[:, 0:NOPE].astype(BF16)
        k_ref[hh, :, NOPE:HEAD_W] = kr_b
        v_ref[hh] = kvh[:, NOPE:HEAD_W].astype(BF16)


def _mla_proj_prompt(x, g, w_in, g_q, g_kv, w_uq, w_ukv, cos, sin, seq, tm, scale):
    m, d = x.shape
    nt = seq // tm
    const2 = lambda shape: pl.BlockSpec(shape, lambda i: (0, 0))
    return pl.pallas_call(
        functools.partial(_mla_proj_prompt_kernel, scale=scale),
        grid=(m // tm,),
        in_specs=[pl.BlockSpec((tm, d), lambda i: (i, 0)), const2((1, d)), const2(w_in.shape),
                  const2((1, Q_LORA)), const2((1, KV_LORA)), const2(w_uq.shape), const2(w_ukv.shape),
                  pl.BlockSpec((tm, LANES), lambda i: (i % nt, 0)),
                  pl.BlockSpec((tm, LANES), lambda i: (i % nt, 0))],
        out_specs=[pl.BlockSpec((H_C, tm, HEAD_W), lambda i: (0, i, 0)),
                   pl.BlockSpec((H_C, tm, HEAD_W), lambda i: (0, i, 0)),
                   pl.BlockSpec((H_C, tm, V_C), lambda i: (0, i, 0)),
                   pl.BlockSpec((tm, KV_LORA), lambda i: (i, 0)),
                   pl.BlockSpec((tm, ROPE), lambda i: (i, 0))],
        out_shape=[jax.ShapeDtypeStruct((H_C, m, HEAD_W), BF16),
                   jax.ShapeDtypeStruct((H_C, m, HEAD_W), BF16),
                   jax.ShapeDtypeStruct((H_C, m, V_C), BF16),
                   jax.ShapeDtypeStruct((m, KV_LORA), F32),
                   jax.ShapeDtypeStruct((m, ROPE), F32)],
        compiler_params=_params(1),
        name="mla_proj_prompt",
    )(x, g, w_in, g_q, g_kv, w_uq, w_ukv, cos, sin)


def _mla_proj_sample_kernel(x_ref, g_ref, win_ref, gq_ref, gkv_ref, wuq_ref, wukt_ref, cos_ref,
                            sin_ref, qlat_ref, qr_ref, ckv_ref, kr_ref, *, scale):
    cq_n, _, _ = _mla_latents(x_ref, g_ref, win_ref, gq_ref, gkv_ref, cos_ref, sin_ref,
                              ckv_ref, kr_ref)
    q_nope, q_rope = _q_heads(cq_n, wuq_ref, cos_ref[...], sin_ref[...], scale)
    for hh in range(H_C):
        q_lat = jnp.dot(q_nope[hh].astype(BF16), wukt_ref[hh], preferred_element_type=F32)
        qlat_ref[hh] = q_lat.astype(BF16)
        qr_ref[hh] = q_rope[hh][:, 0:ROPE].astype(BF16)


def _mla_proj_sample(x, g, w_in, g_q, g_kv, w_uq, w_ukt, cos, sin, scale):
    m, d = x.shape
    const = lambda shape: pl.BlockSpec(shape, lambda i: (0,) * len(shape))
    return pl.pallas_call(
        functools.partial(_mla_proj_sample_kernel, scale=scale),
        grid=(1,),
        in_specs=[const((m, d)), const((1, d)), const(w_in.shape), const((1, Q_LORA)),
                  const((1, KV_LORA)), const(w_uq.shape), const(w_ukt.shape),
                  const((m, LANES)), const((m, LANES))],
        out_specs=[const((H_C, m, KV_LORA)), const((H_C, m, ROPE)), const((m, KV_LORA)),
                   const((m, ROPE))],
        out_shape=[jax.ShapeDtypeStruct((H_C, m, KV_LORA), BF16),
                   jax.ShapeDtypeStruct((H_C, m, ROPE), BF16),
                   jax.ShapeDtypeStruct((m, KV_LORA), F32),
                   jax.ShapeDtypeStruct((m, ROPE), F32)],
        compiler_params=_params(1),
        name="mla_proj_sample",
    )(x, g, w_in, g_q, g_kv, w_uq, w_ukt, cos, sin)


def _flash_kernel(q_ref, k_ref, v_ref, o_ref, m_s, acc_s, *, tile, n_tiles):
    t = tile
    row = lax.broadcasted_iota(jnp.int32, (t, t), 0)
    col = lax.broadcasted_iota(jnp.int32, (t, t), 1)
    visible = (col // CHUNK) <= (row // CHUNK)
    ones = jnp.ones((t, V_C), BF16)
    for j in range(n_tiles):
        ks = slice(j * t, (j + 1) * t)
        kj = k_ref[0, ks, :]
        vj = jnp.concatenate([v_ref[0, ks, :], ones], axis=1)
        for i in range(j, n_tiles):
            qs = slice(i * t, (i + 1) * t)
            s = _nt_dot(q_ref[0, qs, :], kj)
            if i == j:
                s = jnp.where(visible, s, -jnp.inf)
            m_new = jnp.broadcast_to(jnp.max(s, axis=-1, keepdims=True), (t, LANES))
            if j > 0:
                m_prev = m_s[qs]
                m_new = jnp.maximum(m_prev, m_new)
            p = jnp.exp2(s - jnp.concatenate([m_new] * (t // LANES), axis=1))
            pv = jnp.dot(p.astype(BF16), vj, preferred_element_type=F32)
            if j > 0:
                alpha = jnp.exp2(m_prev - m_new)
                pv = pv + jnp.concatenate([alpha, alpha], axis=1) * acc_s[qs]
            if i == j:
                o_ref[qs, :] = (pv[:, :V_C] / pv[:, V_C:]).astype(o_ref.dtype)
            else:
                m_s[qs] = m_new
                acc_s[qs] = pv


def _flash_prompt(q, k, v, batch, seq, tile):
    m = batch * seq
    return pl.pallas_call(
        functools.partial(_flash_kernel, tile=tile, n_tiles=seq // tile),
        grid=(batch, H_C),
        in_specs=[pl.BlockSpec((1, seq, HEAD_W), lambda b, h: (h, b, 0)),
                  pl.BlockSpec((1, seq, HEAD_W), lambda b, h: (h, b, 0)),
                  pl.BlockSpec((1, seq, V_C), lambda b, h: (h, b, 0))],
        out_specs=pl.BlockSpec((seq, V_C), lambda b, h: (b, h)),
        out_shape=jax.ShapeDtypeStruct((m, H_C * V_C), BF16),
        scratch_shapes=[pltpu.VMEM((seq, LANES), F32), pltpu.VMEM((seq, 2 * V_C), F32)],
        compiler_params=_params(2),
        name="flash_prompt",
    )(q, k, v)


def _latent_attn_kernel(qlat_ref, qr_ref, cckv_ref, ckr_ref, nckv_ref, nkr_ref, wuv_ref, o_ref,
                        *, steps, new_rows, tk):
    rows = H_C * steps
    q_lat = qlat_ref[...].reshape(rows, KV_LORA)
    q_r = qr_ref[...].reshape(rows, ROPE)

    def update(state, ckv, kr, valid, kr_transposed):
        ckv_b = ckv.astype(BF16)
        if kr_transposed:
            s_r = jnp.dot(q_r, kr.astype(BF16), preferred_element_type=F32)
        else:
            s_r = _nt_dot(q_r, kr.astype(BF16))
        s = _nt_dot(q_lat, ckv_b) + s_r
        if valid is not None:
            col = lax.broadcasted_iota(jnp.int32, s.shape, 1)
            s = jnp.where(col < valid, s, -jnp.inf)
        m_blk = jnp.max(s, axis=-1, keepdims=True)
        if state is None:
            m_new = m_blk
            p = jnp.exp(s - m_new)
            return (m_new, jnp.sum(p, axis=-1, keepdims=True),
                    jnp.dot(p.astype(BF16), ckv_b, preferred_element_type=F32))
        m_prev, l_prev, acc = state
        m_new = jnp.maximum(m_prev, m_blk)
        p = jnp.exp(s - m_new)
        alpha = jnp.exp(m_prev - m_new)
        return (m_new, alpha * l_prev + jnp.sum(p, axis=-1, keepdims=True),
                alpha * acc + jnp.dot(p.astype(BF16), ckv_b, preferred_element_type=F32))

    state = None
    for j in range(cckv_ref.shape[1] // tk):
        ks = slice(j * tk, (j + 1) * tk)
        state = update(state, cckv_ref[0, ks, :], ckr_ref[0, :, ks], None, True)
    _, l_fin, acc = update(state, nckv_ref[0], nkr_ref[0], new_rows, False)
    o_lat = (acc / l_fin).astype(BF16)
    for hh in range(H_C):
        o_ref[:, hh * V_C:(hh + 1) * V_C] = jnp.dot(
            o_lat[hh * steps:(hh + 1) * steps], wuv_ref[hh],
            preferred_element_type=F32).astype(o_ref.dtype)


def _latent_attn_sample(q_lat, q_r, cache_ckv, cache_kr, new_ckv, new_kr, w_uv, steps, tk):
    batch, past, _ = cache_ckv.shape
    pad_rows = new_ckv.shape[1]
    return pl.pallas_call(
        functools.partial(_latent_attn_kernel, steps=steps, new_rows=steps, tk=tk),
        grid=(batch,),
        in_specs=[pl.BlockSpec((H_C, steps, KV_LORA), lambda b: (0, b, 0)),
                  pl.BlockSpec((H_C, steps, ROPE), lambda b: (0, b, 0)),
                  pl.BlockSpec((1, past, KV_LORA), lambda b: (b, 0, 0)),
                  pl.BlockSpec((1, ROPE, past), lambda b: (b, 0, 0)),
                  pl.BlockSpec((1, pad_rows, KV_LORA), lambda b: (b, 0, 0)),
                  pl.BlockSpec((1, pad_rows, ROPE), lambda b: (b, 0, 0)),
                  pl.BlockSpec((H_C, KV_LORA, V_C), lambda b: (0, 0, 0))],
        out_specs=pl.BlockSpec((steps, H_C * V_C), lambda b: (b, 0)),
        out_shape=jax.ShapeDtypeStruct((batch * steps, H_C * V_C), BF16),
        compiler_params=_params(1),
        name="latent_attn_sample",
    )(q_lat, q_r, cache_ckv, cache_kr, new_ckv, new_kr, w_uv)


def _rope_tables(pos, width):
    half = ROPE // 2
    freqs = ROPE_THETA ** (-jnp.arange(half, dtype=F32) / half)
    ang = pos.astype(F32)[:, None] * freqs[None, :]
    cos, sin = jnp.cos(ang), jnp.sin(ang)
    cos_h = jnp.concatenate([cos, cos], axis=-1)
    sin_h = jnp.concatenate([-sin, sin], axis=-1)
    reps = width // ROPE
    return jnp.tile(cos_h, (1, reps)), jnp.tile(sin_h, (1, reps))


def _prep_even(w_in_ab, w_gate_up):
    sizes = [H_A * DK_A, H_A * DK_A, H_A * DV_A, H_A * DV_A, GATE_RANK,
             H_B * DK_B, H_B * DK_B, H_B * DV_B, H_B * DV_B]
    offs = np.concatenate([[0], np.cumsum(sizes)])
    parts = [w_in_ab[:, offs[n]:offs[n + 1]] for n in range(9)]
    lo = jnp.pad(parts[4], ((0, 0), (0, LANES - GATE_RANK)))
    w = jnp.concatenate(parts[0:4] + parts[5:9] + [lo], axis=1).astype(BF16)
    w_gu = jnp.pad(w_gate_up, ((0, LANES - GATE_RANK), (0, 0))).astype(BF16)
    return w, w_gu


def _prep_odd(w_in_c, w_uq, w_uk, w_uv):
    d = w_in_c.shape[0]
    w_in = jnp.pad(w_in_c, ((0, 0), (0, LANES - ROPE))).astype(BF16)
    uq = w_uq.reshape(Q_LORA, H_C, NOPE + ROPE)
    w_uq_p = jnp.concatenate([uq[:, :, :NOPE].reshape(Q_LORA, H_C * NOPE),
                              uq[:, :, NOPE:].reshape(Q_LORA, H_C * ROPE)],
                             axis=1).astype(BF16)
    w_ukv = jnp.concatenate([w_uk, w_uv], axis=2).reshape(KV_LORA, H_C * HEAD_W).astype(BF16)
    w_ukt = jnp.transpose(w_uk, (1, 2, 0)).astype(BF16)
    w_uvh = jnp.transpose(w_uv, (1, 0, 2)).astype(BF16)
    del d
    return w_in, w_uq_p, w_ukv, w_ukt, w_uvh


def _log_gamma_row():
    lg = np.log1p(-np.exp2(-5.0 - np.arange(H_B, dtype=np.float32))).astype(np.float32)
    return jnp.asarray(np.repeat(lg, DK_B)[None, :])


def _to_time_major(a, batch, steps):
    return a.reshape(batch, steps, -1).transpose(1, 0, 2).reshape(batch * steps, -1)


def _to_batch_major(a, batch, steps):
    return a.reshape(steps, batch, -1).transpose(1, 0, 2).reshape(batch * steps, -1)


def _divisor_tile(n, pref):
    t = min(n, pref)
    while n % t:
        t //= 2
    return t


def _trunk(x, pos0, gla0, ret0, ckv_past, kr_past, conv0, w, time_major_ffn):
    batch, seq, d = x.shape
    m = batch * seq
    pos = pos0 + jnp.arange(seq, dtype=jnp.int32)
    row = lambda v: v.reshape(1, -1)
    scale = float((NOPE + ROPE) ** -0.5)
    tf = TF_FFN
    tm_proj = _divisor_tile(m, TM_PROJ)

    if time_major_ffn:
        ffn_kw = dict(tm=m, tf=tf, shift=batch, tiles_per_seq=1)
        to_ffn = lambda a: _to_time_major(a, batch, seq)
        from_ffn = lambda a: _to_batch_major(a, batch, seq)
        prep_state = lambda s: s.transpose(1, 0, 2).reshape(1, 2 * batch, D_FF)
        post_state = lambda s: s.reshape(2, batch, D_FF).transpose(1, 0, 2)
    else:
        tm_ffn = _divisor_tile(seq, TM_FFN)
        ffn_kw = dict(tm=tm_ffn, tf=tf, shift=1, tiles_per_seq=seq // tm_ffn)
        to_ffn = from_ffn = lambda a: a
        prep_state = lambda s: jnp.pad(s, ((0, 0), (SUBLANES - 2, 0), (0, 0)))
        post_state = lambda s: s

    xf = x.reshape(m, d)
    conv_new = []
    gla_new = ret_new = ckv_new = kr_new = None
    for layer in range(2):
        g_mix = row(w['norm_mix'][layer])
        if layer == 0:
            w_ab, w_gu = _prep_even(w['w_in_ab'][0], w['w_gate_up'][0])
            cos, sin = _rope_tables(pos, 4 * DK_B)
            rows = _divisor_tile(seq, ROWS_LINATTN)
            small = (w_gu, row(w['b_gate'][0]), row(w['g_gla'][0]), row(w['g_ret'][0]), cos, sin,
                     _log_gamma_row())
            if gla0 is None and rows >= 2 * CHUNK:
                mix, gla_new, ret_new = _proj_linattn(xf, g_mix, w_ab, batch, seq, rows, *small)
            else:
                slab = _norm_matmul(xf, g_mix, w_ab, tm_proj, 4 * HEAD_W)
                mix, gla_new, ret_new = _linattn(
                    slab, batch, seq, rows, *small,
                    None if gla0 is None else gla0[0], None if ret0 is None else ret0[0])
            w_o = w['w_out_ab'][0].astype(BF16)
        else:
            w_in, w_uq_p, w_ukv, w_ukt, w_uvh = _prep_odd(w['w_in_c'][0], w['w_uq'][0], w['w_uk'][0],
                                                          w['w_uv'][0])
            cos, sin = _rope_tables(pos, LANES)
            g_q, g_kv = row(w['g_q'][0]), row(w['g_kv'][0])
            if ckv_past is None:
                q, k, v, ckv_new, kr_new = _mla_proj_prompt(
                    xf, g_mix, w_in, g_q, g_kv, w_uq_p, w_ukv, cos, sin, seq, tm_proj,
                    scale * float(np.log2(np.e)))
                mix = _flash_prompt(q, k, v, batch, seq, _divisor_tile(seq, TQ_FLASH))
            else:
                assert ckv_past.shape[2] % CHUNK == 0 and seq <= CHUNK
                cos_m, sin_m = jnp.tile(cos, (batch, 1)), jnp.tile(sin, (batch, 1))
                q_lat, q_r, ckv_new, kr_new = _mla_proj_sample(
                    xf, g_mix, w_in, g_q, g_kv, w_uq_p, w_ukt, cos_m, sin_m, scale)
                pad = ((0, 0), (0, LANES - seq), (0, 0))
                new_ckv = jnp.pad(ckv_new.reshape(batch, seq, KV_LORA), pad)
                new_kr = jnp.pad(kr_new.reshape(batch, seq, ROPE), pad)
                mix = _latent_attn_sample(q_lat, q_r, ckv_past[0],
                                          jnp.swapaxes(kr_past[0], 1, 2), new_ckv, new_kr,
                                          w_uvh, seq, _divisor_tile(ckv_past.shape[2], TK_LATENT))
            w_o = w['w_out_c'][0].astype(BF16)

        if conv0 is None:
            state = jnp.zeros((batch, SUBLANES, D_FF), F32) if not time_major_ffn else \
                jnp.zeros((1, 2 * batch, D_FF), F32)
        else:
            state = prep_state(conv0[layer])
        y, conv_rows = _ffn(to_ffn(xf), to_ffn(mix), w_o, row(w['norm_ffn'][layer]),
                            w['w_ffn_in'].astype(BF16), w['w_dwconv'][layer],
                            row(w['b_dwconv'][layer]),
                            (0.5 * w['w_ffn_out'][layer]).astype(BF16), state,
                            row(w['norm_final']), layer=layer, final_norm=(layer == 1), **ffn_kw)
        xf = from_ffn(y)
        tps = ffn_kw['tiles_per_seq']
        conv_new.append(post_state(conv_rows[tps - 1::tps]))

    return (xf.reshape(batch, seq, d), gla_new[None], ret_new[None],
            ckv_new.reshape(1, batch, seq, KV_LORA), kr_new.reshape(1, batch, seq, ROPE),
            jnp.stack(conv_new))


def kernel(x_prompt, x_sample, state_gla, state_ret, cache_ckv, cache_krope, state_conv, norm_mix, norm_ffn, norm_final, w_in_ab, w_gate_up, b_gate, g_gla, g_ret, w_out_ab, w_in_c, g_q, g_kv, w_uq, w_uk, w_uv, w_out_c, w_ffn_in, w_dwconv, b_dwconv, w_ffn_out):
    w = {'norm_mix': norm_mix, 'norm_ffn': norm_ffn, 'norm_final': norm_final,
         'w_in_ab': w_in_ab, 'w_gate_up': w_gate_up, 'b_gate': b_gate, 'g_gla': g_gla,
         'g_ret': g_ret, 'w_out_ab': w_out_ab, 'w_in_c': w_in_c, 'g_q': g_q, 'g_kv': g_kv,
         'w_uq': w_uq, 'w_uk': w_uk, 'w_uv': w_uv, 'w_out_c': w_out_c, 'w_ffn_in': w_ffn_in,
         'w_dwconv': w_dwconv, 'b_dwconv': b_dwconv, 'w_ffn_out': w_ffn_out}
    past_len = cache_ckv.shape[2]
    y_p, gla_p, ret_p, ckv_p, kr_p, conv_p = _trunk(
        x_prompt, 0, None, None, None, None, None, w, time_major_ffn=False)
    y_s, gla_s, ret_s, ckv_s, kr_s, conv_s = _trunk(
        x_sample, past_len, state_gla, state_ret, cache_ckv, cache_krope, state_conv, w,
        time_major_ffn=True)
    return (y_p, y_s, gla_p, gla_s, ret_p, ret_s, ckv_p, ckv_s, kr_p, kr_s, conv_p, conv_s)
```

```python
import functools

import numpy as np
import jax
import jax.numpy as jnp
from jax import lax
from jax.experimental import pallas as pl
from jax.experimental.pallas import tpu as pltpu

F32 = jnp.float32
BF16 = jnp.bfloat16

D_MODEL = 1024
CHUNK = 64
EPS = 1e-6
ROPE_THETA = 10000.0
H_A, DK_A, DV_A = 4, 64, 128
GATE_RANK = 16
GATE_TAU = 16.0
H_B, DK_B, DV_B = 4, 64, 128
H_C = 8
Q_LORA, KV_LORA, NOPE, ROPE, V_C = 384, 512, 128, 64, 128
D_FF = 2816
CONV_W = 3

LANES = 128
SUBLANES = 8
VMEM_LIMIT = 56 * 1024 * 1024
AB_COLS = 3200
HEAD_W = 2 * LANES
TM_PROJ = 1024
TM_FFN = 1024
TF_FFN = D_FF
SUB_FFN = 512
ROWS_LINATTN = 256
TQ_FLASH = 256
TK_LATENT = 4096


def _params(n_axes):
    return pltpu.CompilerParams(dimension_semantics=("arbitrary",) * n_axes,
                                vmem_limit_bytes=VMEM_LIMIT)


def _rms(x, g):
    return x * lax.rsqrt(jnp.mean(x * x, axis=-1, keepdims=True) + EPS) * g


def _rope_first_half(shape):
    lane = lax.broadcasted_iota(jnp.int32, shape, 1)
    return (lane % ROPE) < ROPE // 2


def _rope(x, cos, sin_signed, first=None):
    w = x.shape[1]
    half = ROPE // 2
    if first is None:
        first = _rope_first_half(x.shape)
    swapped = jnp.where(first, pltpu.roll(x, w - half, 1), pltpu.roll(x, half, 1))
    return x * cos + swapped * sin_signed


def _nt_dot(a, b):
    return lax.dot_general(a, b, (((1,), (1,)), ((), ())), preferred_element_type=F32)


def _norm_matmul_kernel(x_ref, g_ref, w_ref, o_ref, *, col_chunk):
    h = _rms(x_ref[...], g_ref[...]).astype(BF16)
    n = w_ref.shape[1]
    for c0 in range(0, n, col_chunk):
        c1 = min(c0 + col_chunk, n)
        o_ref[:, c0:c1] = jnp.dot(h, w_ref[:, c0:c1], preferred_element_type=F32).astype(o_ref.dtype)


def _norm_matmul(x, g, w, tm, col_chunk):
    m, d = x.shape
    n = w.shape[1]
    return pl.pallas_call(
        functools.partial(_norm_matmul_kernel, col_chunk=col_chunk),
        grid=(m // tm,),
        in_specs=[pl.BlockSpec((tm, d), lambda i: (i, 0)),
                  pl.BlockSpec((1, d), lambda i: (0, 0)),
                  pl.BlockSpec((d, n), lambda i: (0, 0))],
        out_specs=pl.BlockSpec((tm, n), lambda i: (i, 0)),
        out_shape=jax.ShapeDtypeStruct((m, n), BF16),
        compiler_params=_params(1),
        name="norm_inproj_ab",
    )(x, g, w)


def _decay_attention_streams(streams, mask_k, mask_v, n_chunks, mid, last, between):
    c = CHUNK
    assert 2 * c == LANES
    hk, hv = 4 * DK_A, 4 * DV_A
    rt = lax.broadcasted_iota(jnp.int32, (c, 4 * c), 0)
    ct = lax.broadcasted_iota(jnp.int32, (c, 4 * c), 1)
    causal = (ct % c) <= rt
    lane = lax.broadcasted_iota(jnp.int32, (DK_A, LANES), 1)
    zero_v = jnp.zeros((LANES - c, hv), BF16)
    zero_blk = jnp.zeros((DK_A, DV_A), BF16)
    log2e = np.float32(np.log2(np.e))

    items = []
    for q, k, v, bcum, s_ref in streams:
        bcum2 = bcum * log2e
        for n in range(n_chunks):
            sl = slice(n * c, (n + 1) * c)
            qn, kn, vn, bn = q[sl], k[sl], v[sl], bcum2[sl]
            b_ref = bn[mid:mid + 1]
            b_last = bn[last:last + 1]
            it = dict(s_ref=s_ref, vn=vn)
            it['q_rel'] = (qn * jnp.exp2(bn - b_ref)).astype(BF16)
            k_rel = (kn * jnp.exp2(b_ref - bn)).astype(BF16)
            it['k_bd'] = jnp.concatenate([k_rel] * 4, axis=0) * mask_k
            it['q_dec'] = (qn * jnp.exp2(bn)).astype(BF16)
            k_dec = kn * jnp.exp2(b_last - bn)
            decay_rows = jnp.broadcast_to(jnp.exp2(b_last), (LANES - c, hk))
            it['kt'] = jnp.concatenate([k_dec, decay_rows], axis=0).T
            items.append(it)
    between()
    for it in items:
        it['scores'] = jnp.where(causal, _nt_dot(it['q_rel'], it['k_bd']), 0.0).astype(BF16)
    between()
    for it in items:
        v_bd = jnp.concatenate([it['vn']] * 4, axis=0) * mask_v
        it['o'] = jnp.dot(it['scores'], v_bd, preferred_element_type=F32)
        kt_b = it['kt'].astype(BF16)
        v_pad = jnp.concatenate([it['vn'], zero_v], axis=0)
        it['kv'] = [jnp.dot(kt_b[h * DK_A:(h + 1) * DK_A], v_pad[:, h * DV_A:(h + 1) * DV_A],
                            preferred_element_type=F32) for h in range(4)]
    between()
    for it in items:
        s_ref = it['s_ref']
        s_old = [s_ref[h] for h in range(4)]
        it['s_bd'] = jnp.concatenate(
            [jnp.concatenate([zero_blk] * h + [s_old[h].astype(BF16)] + [zero_blk] * (3 - h), axis=1)
             for h in range(4)], axis=0)
        for h in range(4):
            kth = it['kt'][h * DK_A:(h + 1) * DK_A]
            decay = jnp.where(lane < c, pltpu.roll(kth, c, 1), kth)
            s_ref[h] = s_old[h] * decay + it['kv'][h]
    between()
    outs = []
    for si in range(len(streams)):
        chunk_out = [it['o'] + jnp.dot(it['q_dec'], it['s_bd'], preferred_element_type=F32)
                     for it in items[si * n_chunks:(si + 1) * n_chunks]]
        outs.append(chunk_out[0] if n_chunks == 1 else jnp.concatenate(chunk_out, axis=0))
    between()
    return outs


_SLAB = dict(qa=(0, 256), ka=(256, 256), va=(512, 512), ra=(1024, 512),
             qb=(1536, 256), kb=(1792, 256), vb=(2048, 512), gb=(2560, 512), lo=(3072, LANES))


def _linattn_compute(cols, wgu_ref, bg_ref, gg_ref, gr_ref, cos_ref, sin_ref, lgam_ref, tri_ref,
                     mk_ref, mv_ref, mix_ref, sa, sb, rows, between=lambda: None):
    c = CHUNK
    padded = max(rows, c)
    n_chunks = padded // c
    valid = min(rows, c)
    mid, last = (valid - 1) // 2, valid - 1

    def pad_rows(x):
        if padded == rows:
            return x
        return jnp.concatenate([x, jnp.zeros((padded - rows, x.shape[1]), x.dtype)], axis=0)

    tri = tri_ref[...]
    mask_k, mask_v = mk_ref[...], mv_ref[...]

    gate = jnp.dot(cols['lo'], wgu_ref[...], preferred_element_type=F32) + bg_ref[...]
    log_a = (jnp.minimum(gate, 0.0) - jnp.log1p(jnp.exp(-jnp.abs(gate)))) / GATE_TAU
    log_a = pad_rows(log_a)
    hi = log_a.astype(BF16)
    lo = (log_a - hi.astype(F32)).astype(BF16)
    bcum_a = (jnp.dot(tri, hi, preferred_element_type=F32)
              + jnp.dot(tri, lo, preferred_element_type=F32))
    qa = pad_rows(cols['qa'].astype(F32) * DK_A ** -0.5)
    ka = pad_rows(cols['ka'].astype(F32))
    va = pad_rows(cols['va'])
    between()

    cos, sin = cos_ref[...], sin_ref[...]
    first = _rope_first_half(cos.shape)
    qb = pad_rows(_rope(cols['qb'].astype(F32), cos, sin, first))
    kb = pad_rows(_rope(cols['kb'].astype(F32), cos, sin, first) * DK_B ** -0.5)
    vb = pad_rows(cols['vb'])
    pos = lax.broadcasted_iota(jnp.int32, (padded, 4 * DK_B), 0) % c
    steps = jnp.where(pos < valid, pos + 1, valid).astype(F32)
    bcum_b = steps * lgam_ref[...]
    o_a, o_b = _decay_attention_streams(
        [(qa, ka, va, bcum_a, sa), (qb, kb, vb, bcum_b, sb)], mask_k, mask_v, n_chunks, mid, last,
        between)
    o_a, o_b = o_a[:rows], o_b[:rows]

    gg, gr = gg_ref[...], gr_ref[...]
    for h in range(4):
        sl = slice(h * DV_A, (h + 1) * DV_A)
        oh = o_a[:, sl]
        oh = oh * lax.rsqrt(jnp.mean(oh * oh, axis=-1, keepdims=True) + EPS) * gg[:, sl]
        r = cols['ra'][:, sl].astype(F32)
        mix_ref[:, sl] = (oh * (r * jax.nn.sigmoid(r))).astype(mix_ref.dtype)
        ob = o_b[:, sl]
        ob = ob - jnp.mean(ob, axis=-1, keepdims=True)
        ob = ob * lax.rsqrt(jnp.mean(ob * ob, axis=-1, keepdims=True) + EPS) * gr[:, sl]
        gb = cols['gb'][:, sl].astype(F32)
        mix_ref[:, 4 * DV_A + h * DV_B:4 * DV_A + (h + 1) * DV_B] = (
            ob * (gb * jax.nn.sigmoid(gb))).astype(mix_ref.dtype)


def _linattn_kernel(qa_ref, ka_ref, va_ref, ra_ref, qb_ref, kb_ref, vb_ref, gb_ref, lo_ref,
                    wgu_ref, bg_ref, gg_ref, gr_ref, cos_ref, sin_ref, lgam_ref, tri_ref, mk_ref,
                    mv_ref, *rest, rows, has_state):
    if has_state:
        s0a_ref, s0b_ref, mix_ref, sa_out, sb_out, sa, sb = rest
    else:
        mix_ref, sa_out, sb_out, sa, sb = rest
    t = pl.program_id(1)

    @pl.when(t == 0)
    def _init():
        if has_state:
            sa[...] = s0a_ref[0]
            sb[...] = s0b_ref[0]
        else:
            sa[...] = jnp.zeros_like(sa)
            sb[...] = jnp.zeros_like(sb)

    cols = dict(qa=qa_ref[...], ka=ka_ref[...], va=va_ref[...], ra=ra_ref[...], qb=qb_ref[...],
                kb=kb_ref[...], vb=vb_ref[...], gb=gb_ref[...], lo=lo_ref[...])
    _linattn_compute(cols, wgu_ref, bg_ref, gg_ref, gr_ref, cos_ref, sin_ref, lgam_ref, tri_ref,
                     mk_ref, mv_ref, mix_ref, sa, sb, rows)

    @pl.when(t == pl.num_programs(1) - 1)
    def _emit_state():
        sa_out[0] = sa[...]
        sb_out[0] = sb[...]


def _proj_linattn_kernel(x_ref, g_ref, w_ref, wgu_ref, bg_ref, gg_ref, gr_ref, cos_ref, sin_ref,
                         lgam_ref, tri_ref, mk_ref, mv_ref, mix_ref, sa_out, sb_out, slab_even,
                         slab_odd, sa, sb, *, rows, nt, col_chunk):
    s = pl.program_id(0)
    t = jnp.maximum(s - 1, 0) % nt

    @pl.when(s == 0)
    def _zero_slab():
        slab_odd[...] = jnp.zeros_like(slab_odd)

    @pl.when(t == 0)
    def _init():
        sa[...] = jnp.zeros_like(sa)
        sb[...] = jnp.zeros_like(sb)

    def step(write_ref, read_ref):
        cols = {name: read_ref[:, off:off + width] for name, (off, width) in _SLAB.items()}
        h = _rms(x_ref[...], g_ref[...]).astype(BF16)
        n = w_ref.shape[1]
        pieces = list(range(0, n, col_chunk))
        shares = (5, 1, 0, 1, 1, 5)
        total = len(pieces)
        quotas = [total * sum(shares[:k + 1]) // sum(shares) - total * sum(shares[:k]) // sum(shares)
                  for k in range(len(shares))]

        def project(count):
            for _ in range(min(count, len(pieces))):
                c0 = pieces.pop(0)
                c1 = min(c0 + col_chunk, n)
                write_ref[:, c0:c1] = jnp.dot(h, w_ref[:, c0:c1],
                                              preferred_element_type=F32).astype(write_ref.dtype)

        _linattn_compute(cols, wgu_ref, bg_ref, gg_ref, gr_ref, cos_ref, sin_ref, lgam_ref,
                         tri_ref, mk_ref, mv_ref, mix_ref, sa, sb, rows,
                         between=lambda: project(quotas.pop(0) if quotas else 0))
        project(len(pieces))

    @pl.when(s % 2 == 0)
    def _even():
        step(slab_even, slab_odd)

    @pl.when(s % 2 == 1)
    def _odd():
        step(slab_odd, slab_even)

    @pl.when(t == nt - 1)
    def _emit_state():
        sa_out[0] = sa[...]
        sb_out[0] = sb[...]


def _linattn_masks(rows):
    c = CHUNK
    padded = max(rows, c)
    r = np.arange(padded)
    tri = jnp.asarray(((r[:, None] // c == r[None, :] // c) & (r[None, :] <= r[:, None])), BF16)
    r4 = np.arange(4 * c)
    mask_k = jnp.asarray(r4[:, None] // c == np.arange(4 * DK_A)[None, :] // DK_A, BF16)
    mask_v = jnp.asarray(r4[:, None] // c == np.arange(4 * DV_A)[None, :] // DV_A, BF16)
    return tri, mask_k, mask_v


def _proj_linattn(x, g, w_ab, batch, seq, rows, w_gu, b_gate, g_gla, g_ret, cos, sin, lgam):
    m, d = x.shape
    nt = seq // rows
    n_tiles = batch * nt
    tri, mask_k, mask_v = _linattn_masks(rows)
    hk, hv = 4 * DK_A, 4 * DV_A
    prev = lambda s: jnp.maximum(s - 1, 0)

    def const(shape):
        return pl.BlockSpec(shape, lambda s: (0,) * len(shape))

    state_spec = pl.BlockSpec((1, 4, DK_A, DV_A), lambda s: (prev(s) // nt, 0, 0, 0))
    state_shape = jax.ShapeDtypeStruct((batch, 4, DK_A, DV_A), F32)
    return pl.pallas_call(
        functools.partial(_proj_linattn_kernel, rows=rows, nt=nt, col_chunk=HEAD_W),
        grid=(n_tiles + 1,),
        in_specs=[pl.BlockSpec((rows, d), lambda s: (jnp.minimum(s, n_tiles - 1), 0)),
                  const((1, d)), const(w_ab.shape),
                  const((LANES, hk)), const((1, hk)), const((1, hv)), const((1, hv)),
                  pl.BlockSpec((rows, hk), lambda s: (prev(s) % nt, 0)),
                  pl.BlockSpec((rows, hk), lambda s: (prev(s) % nt, 0)),
                  const((1, hk)), const(tri.shape), const(mask_k.shape), const(mask_v.shape)],
        out_specs=[pl.BlockSpec((rows, 2 * hv), lambda s: (prev(s), 0)), state_spec, state_spec],
        out_shape=[jax.ShapeDtypeStruct((m, 2 * hv), BF16), state_shape, state_shape],
        scratch_shapes=[pltpu.VMEM((rows, AB_COLS), BF16), pltpu.VMEM((rows, AB_COLS), BF16),
                        pltpu.VMEM((4, DK_A, DV_A), F32), pltpu.VMEM((4, DK_B, DV_B), F32)],
        compiler_params=_params(1),
        name="inproj_decay_linear_attention",
    )(x, g, w_ab, w_gu, b_gate, g_gla, g_ret, cos, sin, lgam, tri, mask_k, mask_v)


def _linattn(slab, batch, seq, rows, w_gu, b_gate, g_gla, g_ret, cos, sin, lgam, s0a, s0b):
    nt = seq // rows
    has_state = s0a is not None
    tri, mask_k, mask_v = _linattn_masks(rows)

    def col(width, idx):
        return pl.BlockSpec((rows, width), lambda b, t, idx=idx: (b * nt + t, idx))

    def const(shape):
        return pl.BlockSpec(shape, lambda b, t: (0,) * len(shape))

    hk = 4 * DK_A
    hv = 4 * DV_A
    in_specs = [col(hk, 0), col(hk, 1), col(hv, 1), col(hv, 2),
                col(hk, 6), col(hk, 7), col(hv, 4), col(hv, 5),
                col(LANES, 24),
                const((LANES, hk)), const((1, hk)), const((1, hv)), const((1, hv)),
                pl.BlockSpec((rows, hk), lambda b, t: (t, 0)),
                pl.BlockSpec((rows, hk), lambda b, t: (t, 0)),
                const((1, hk)), const(tri.shape), const(mask_k.shape), const(mask_v.shape)]
    args = [slab] * 9 + [w_gu, b_gate, g_gla, g_ret, cos, sin, lgam, tri, mask_k, mask_v]
    state_spec = pl.BlockSpec((1, 4, DK_A, DV_A), lambda b, t: (b, 0, 0, 0))
    if has_state:
        in_specs += [state_spec, state_spec]
        args += [s0a, s0b]
    state_shape = jax.ShapeDtypeStruct((batch, 4, DK_A, DV_A), F32)
    return pl.pallas_call(
        functools.partial(_linattn_kernel, rows=rows, has_state=has_state),
        grid=(batch, nt),
        in_specs=in_specs,
        out_specs=[pl.BlockSpec((rows, 2 * hv), lambda b, t: (b * nt + t, 0)), state_spec, state_spec],
        out_shape=[jax.ShapeDtypeStruct((batch * seq, 2 * hv), BF16), state_shape, state_shape],
        scratch_shapes=[pltpu.VMEM((4, DK_A, DV_A), F32), pltpu.VMEM((4, DK_B, DV_B), F32)],
        compiler_params=_params(2),
        name="decay_linear_attention",
    )(*args)


def _ffn_kernel(x_ref, mix_ref, wo_ref, g_ref, wa_ref, wu_ref, wdw_ref, bdw_ref, wout_ref, st_ref,
                gf_ref, y_ref, cv_ref, h_s, carry_s,
                *, tm, sub, prev_rows, shift, tiles_per_seq, final_norm, single_step):
    i = pl.program_id(0)
    j = pl.program_id(1)
    p = prev_rows
    subs = [slice(r, r + sub) for r in range(0, tm, sub)]

    def mid_residual(rs):
        xm = x_ref[rs] + jnp.dot(mix_ref[rs], wo_ref[...], preferred_element_type=F32)
        return xm, _rms(xm, g_ref[...]).astype(BF16)

    if not single_step:
        @pl.when(j == 0)
        def _start():
            for rs in subs:
                y_ref[rs], h_s[rs] = mid_residual(rs)

    at_start = (i % tiles_per_seq) == 0
    tail = jnp.where(at_start, st_ref[0], carry_s[j])
    last = j == pl.num_programs(1) - 1
    wdw = wdw_ref[...]
    bdw = bdw_ref[...]
    a = None
    for rs in subs:
        if single_step:
            base, h = mid_residual(rs)
        else:
            base, h = y_ref[rs], h_s[rs]
        a = jnp.dot(h, wa_ref[...], preferred_element_type=F32)
        u = jnp.dot(h, wu_ref[...], preferred_element_type=F32)
        ext = jnp.concatenate([tail, a], axis=0)
        conv = bdw + pltpu.roll(ext, 2 * shift, 0)[p:] * wdw[0:1]
        conv = conv + pltpu.roll(ext, shift, 0)[p:] * wdw[1:2]
        conv = conv + a * wdw[2:3]
        act = conv * (1.0 + lax.erf(conv * np.float32(np.sqrt(0.5)))) * u
        out = base + jnp.dot(act.astype(BF16), wout_ref[...], preferred_element_type=F32)
        if final_norm:
            normed = _rms(out, gf_ref[...])
            out = normed if single_step else jnp.where(last, normed, out)
        y_ref[rs] = out
        tail = a[sub - p:]
    carry_s[j] = tail
    cv_ref[0] = a[sub - 2 * shift:]


def _ffn(x, mix, w_o, g_ffn, w_in, w_dw, b_dw, w_out, conv_state, g_final, *, layer, tm, tf, shift,
         tiles_per_seq, final_norm):
    m, d = x.shape
    n_f = D_FF // tf
    n_seq, prev_rows, _ = conv_state.shape
    kern = functools.partial(_ffn_kernel, tm=tm, sub=min(tm, SUB_FFN), prev_rows=prev_rows,
                             shift=shift, tiles_per_seq=tiles_per_seq, final_norm=final_norm,
                             single_step=(n_f == 1))
    h_rows = SUBLANES * 2 if n_f == 1 else tm
    return pl.pallas_call(
        kern,
        grid=(m // tm, n_f),
        in_specs=[pl.BlockSpec((tm, d), lambda i, j: (i, 0)),
                  pl.BlockSpec((tm, d), lambda i, j: (i, 0)),
                  pl.BlockSpec((d, d), lambda i, j: (0, 0)),
                  pl.BlockSpec((1, d), lambda i, j: (0, 0)),
                  pl.BlockSpec((None, d, tf), lambda i, j: (layer, 0, j)),
                  pl.BlockSpec((None, d, tf), lambda i, j: (layer, 0, n_f + j)),
                  pl.BlockSpec((CONV_W, tf), lambda i, j: (0, j)),
                  pl.BlockSpec((1, tf), lambda i, j: (0, j)),
                  pl.BlockSpec((tf, d), lambda i, j: (j, 0)),
                  pl.BlockSpec((1, prev_rows, tf), lambda i, j: (i // tiles_per_seq, 0, j)),
                  pl.BlockSpec((1, d), lambda i, j: (0, 0))],
        out_specs=[pl.BlockSpec((tm, d), lambda i, j: (i, 0)),
                   pl.BlockSpec((1, 2 * shift, tf), lambda i, j: (i, 0, j))],
        out_shape=[jax.ShapeDtypeStruct((m, d), F32),
                   jax.ShapeDtypeStruct((m // tm, 2 * shift, D_FF), F32)],
        scratch_shapes=[pltpu.VMEM((h_rows, d), BF16), pltpu.VMEM((n_f, prev_rows, tf), F32)],
        compiler_params=_params(2),
        name="outproj_convffn",
    )(x, mix, w_o, g_ffn, w_in, w_in, w_dw, b_dw, w_out, conv_state, g_final)


def _mla_latents(x_ref, g_ref, win_ref, gq_ref, gkv_ref, cos_ref, sin_ref, ckv_ref, kr_ref):
    h = _rms(x_ref[...], g_ref[...]).astype(BF16)
    p = jnp.dot(h, win_ref[...], preferred_element_type=F32)
    cq_n = _rms(p[:, 0:Q_LORA], gq_ref[...]).astype(BF16)
    ckv_n = _rms(p[:, Q_LORA:Q_LORA + KV_LORA], gkv_ref[...])
    ckv_ref[...] = ckv_n
    kr = _rope(p[:, Q_LORA + KV_LORA:], cos_ref[...], sin_ref[...])
    kr_ref[...] = kr[:, 0:ROPE]
    return cq_n, ckv_n, kr


def _q_heads(cq_n, wuq_ref, cos, sin, scale):
    n_nope = H_C * NOPE
    nope, rope = [], []
    for c0 in range(0, n_nope, HEAD_W):
        blk = jnp.dot(cq_n, wuq_ref[:, c0:c0 + HEAD_W], preferred_element_type=F32) * scale
        nope += [blk[:, :NOPE], blk[:, NOPE:]]
    low = lax.broadcasted_iota(jnp.int32, cos.shape, 1) < ROPE
    for c0 in range(n_nope, n_nope + H_C * ROPE, HEAD_W):
        blk = jnp.dot(cq_n, wuq_ref[:, c0:c0 + HEAD_W], preferred_element_type=F32) * scale
        for half in range(HEAD_W // LANES):
            pair = _rope(blk[:, half * LANES:(half + 1) * LANES], cos, sin)
            rope.append(jnp.where(low, pair, 0.0))
            rope.append(jnp.where(low, pltpu.roll(pair, ROPE, 1), 0.0))
    return nope, rope


def _mla_proj_prompt_kernel(x_ref, g_ref, win_ref, gq_ref, gkv_ref, wuq_ref, wukv_ref, cos_ref,
                            sin_ref, q_ref, k_ref, v_ref, ckv_ref, kr_ref, *, scale):
    cq_n, ckv_n, kr = _mla_latents(x_ref, g_ref, win_ref, gq_ref, gkv_ref, cos_ref, sin_ref,
                                   ckv_ref, kr_ref)
    q_nope, q_rope = _q_heads(cq_n, wuq_ref, cos_ref[...], sin_ref[...], scale)
    kr_b = kr.astype(BF16)
    ckv_b = ckv_n.astype(BF16)
    for hh in range(H_C):
        q_ref[hh, :, 0:NOPE] = q_nope[hh].astype(BF16)
        q_ref[hh, :, NOPE:HEAD_W] = q_rope[hh].astype(BF16)
        kvh = jnp.dot(ckv_b, wukv_ref[:, hh * HEAD_W:(hh + 1) * HEAD_W],
                      preferred_element_type=F32)
        k_ref[hh, :, 0:NOPE] = kvh[:, 0:NOPE].astype(BF16)
        k_ref[hh, :, NOPE:HEAD_W] = kr_b
        v_ref[hh] = kvh[:, NOPE:HEAD_W].astype(BF16)


def _mla_proj_prompt(x, g, w_in, g_q, g_kv, w_uq, w_ukv, cos, sin, seq, tm, scale):
    m, d = x.shape
    nt = seq // tm
    const2 = lambda shape: pl.BlockSpec(shape, lambda i: (0, 0))
    return pl.pallas_call(
        functools.partial(_mla_proj_prompt_kernel, scale=scale),
        grid=(m // tm,),
        in_specs=[pl.BlockSpec((tm, d), lambda i: (i, 0)), const2((1, d)), const2(w_in.shape),
                  const2((1, Q_LORA)), const2((1, KV_LORA)), const2(w_uq.shape), const2(w_ukv.shape),
                  pl.BlockSpec((tm, LANES), lambda i: (i % nt, 0)),
                  pl.BlockSpec((tm, LANES), lambda i: (i % nt, 0))],
        out_specs=[pl.BlockSpec((H_C, tm, HEAD_W), lambda i: (0, i, 0)),
                   pl.BlockSpec((H_C, tm, HEAD_W), lambda i: (0, i, 0)),
                   pl.BlockSpec((H_C, tm, V_C), lambda i: (0, i, 0)),
                   pl.BlockSpec((tm, KV_LORA), lambda i: (i, 0)),
                   pl.BlockSpec((tm, ROPE), lambda i: (i, 0))],
        out_shape=[jax.ShapeDtypeStruct((H_C, m, HEAD_W), BF16),
                   jax.ShapeDtypeStruct((H_C, m, HEAD_W), BF16),
                   jax.ShapeDtypeStruct((H_C, m, V_C), BF16),
                   jax.ShapeDtypeStruct((m, KV_LORA), F32),
                   jax.ShapeDtypeStruct((m, ROPE), F32)],
        compiler_params=_params(1),
        name="mla_proj_prompt",
    )(x, g, w_in, g_q, g_kv, w_uq, w_ukv, cos, sin)


def _mla_proj_sample_kernel(x_ref, g_ref, win_ref, gq_ref, gkv_ref, wuq_ref, wukt_ref, cos_ref,
                            sin_ref, qlat_ref, qr_ref, ckv_ref, kr_ref, *, scale):
    cq_n, _, _ = _mla_latents(x_ref, g_ref, win_ref, gq_ref, gkv_ref, cos_ref, sin_ref,
                              ckv_ref, kr_ref)
    q_nope, q_rope = _q_heads(cq_n, wuq_ref, cos_ref[...], sin_ref[...], scale)
    for hh in range(H_C):
        q_lat = jnp.dot(q_nope[hh].astype(BF16), wukt_ref[hh], preferred_element_type=F32)
        qlat_ref[hh] = q_lat.astype(BF16)
        qr_ref[hh] = q_rope[hh][:, 0:ROPE].astype(BF16)


def _mla_proj_sample(x, g, w_in, g_q, g_kv, w_uq, w_ukt, cos, sin, scale):
    m, d = x.shape
    const = lambda shape: pl.BlockSpec(shape, lambda i: (0,) * len(shape))
    return pl.pallas_call(
        functools.partial(_mla_proj_sample_kernel, scale=scale),
        grid=(1,),
        in_specs=[const((m, d)), const((1, d)), const(w_in.shape), const((1, Q_LORA)),
                  const((1, KV_LORA)), const(w_uq.shape), const(w_ukt.shape),
                  const((m, LANES)), const((m, LANES))],
        out_specs=[const((H_C, m, KV_LORA)), const((H_C, m, ROPE)), const((m, KV_LORA)),
                   const((m, ROPE))],
        out_shape=[jax.ShapeDtypeStruct((H_C, m, KV_LORA), BF16),
                   jax.ShapeDtypeStruct((H_C, m, ROPE), BF16),
                   jax.ShapeDtypeStruct((m, KV_LORA), F32),
                   jax.ShapeDtypeStruct((m, ROPE), F32)],
        compiler_params=_params(1),
        name="mla_proj_sample",
    )(x, g, w_in, g_q, g_kv, w_uq, w_ukt, cos, sin)


def _flash_kernel(q_ref, k_ref, v_ref, o_ref, m_s, acc_s, *, tile, n_tiles):
    t = tile
    row = lax.broadcasted_iota(jnp.int32, (t, t), 0)
    col = lax.broadcasted_iota(jnp.int32, (t, t), 1)
    visible = (col // CHUNK) <= (row // CHUNK)
    ones = jnp.ones((t, V_C), BF16)
    for j in range(n_tiles):
        ks = slice(j * t, (j + 1) * t)
        kj = k_ref[0, ks, :]
        vj = jnp.concatenate([v_ref[0, ks, :], ones], axis=1)
        for i in range(j, n_tiles):
            qs = slice(i * t, (i + 1) * t)
            s = _nt_dot(q_ref[0, qs, :], kj)
            if i == j:
                s = jnp.where(visible, s, -jnp.inf)
            m_new = jnp.broadcast_to(jnp.max(s, axis=-1, keepdims=True), (t, LANES))
            if j > 0:
                m_prev = m_s[qs]
                m_new = jnp.maximum(m_prev, m_new)
            p = jnp.exp2(s - jnp.concatenate([m_new] * (t // LANES), axis=1))
            pv = jnp.dot(p.astype(BF16), vj, preferred_element_type=F32)
            if j > 0:
                alpha = jnp.exp2(m_prev - m_new)
                pv = pv + jnp.concatenate([alpha, alpha], axis=1) * acc_s[qs]
            if i == j:
                o_ref[qs, :] = (pv[:, :V_C] / pv[:, V_C:]).astype(o_ref.dtype)
            else:
                m_s[qs] = m_new
                acc_s[qs] = pv


def _flash_prompt(q, k, v, batch, seq, tile):
    m = batch * seq
    return pl.pallas_call(
        functools.partial(_flash_kernel, tile=tile, n_tiles=seq // tile),
        grid=(batch, H_C),
        in_specs=[pl.BlockSpec((1, seq, HEAD_W), lambda b, h: (h, b, 0)),
                  pl.BlockSpec((1, seq, HEAD_W), lambda b, h: (h, b, 0)),
                  pl.BlockSpec((1, seq, V_C), lambda b, h: (h, b, 0))],
        out_specs=pl.BlockSpec((seq, V_C), lambda b, h: (b, h)),
        out_shape=jax.ShapeDtypeStruct((m, H_C * V_C), BF16),
        scratch_shapes=[pltpu.VMEM((seq, LANES), F32), pltpu.VMEM((seq, 2 * V_C), F32)],
        compiler_params=_params(2),
        name="flash_prompt",
    )(q, k, v)


def _latent_attn_kernel(qlat_ref, qr_ref, cckv_ref, ckr_ref, nckv_ref, nkr_ref, wuv_ref, o_ref,
                        *, steps, new_rows, tk):
    rows = H_C * steps
    q_lat = qlat_ref[...].reshape(rows, KV_LORA)
    q_r = qr_ref[...].reshape(rows, ROPE)

    def update(state, ckv, kr, valid, kr_transposed):
        ckv_b = ckv.astype(BF16)
        if kr_transposed:
            s_r = jnp.dot(q_r, kr.astype(BF16), preferred_element_type=F32)
        else:
            s_r = _nt_dot(q_r, kr.astype(BF16))
        s = _nt_dot(q_lat, ckv_b) + s_r
        if valid is not None:
            col = lax.broadcasted_iota(jnp.int32, s.shape, 1)
            s = jnp.where(col < valid, s, -jnp.inf)
        m_blk = jnp.max(s, axis=-1, keepdims=True)
        if state is None:
            m_new = m_blk
            p = jnp.exp(s - m_new)
            return (m_new, jnp.sum(p, axis=-1, keepdims=True),
                    jnp.dot(p.astype(BF16), ckv_b, preferred_element_type=F32))
        m_prev, l_prev, acc = state
        m_new = jnp.maximum(m_prev, m_blk)
        p = jnp.exp(s - m_new)
        alpha = jnp.exp(m_prev - m_new)
        return (m_new, alpha * l_prev + jnp.sum(p, axis=-1, keepdims=True),
                alpha * acc + jnp.dot(p.astype(BF16), ckv_b, preferred_element_type=F32))

    state = None
    for j in range(cckv_ref.shape[1] // tk):
        ks = slice(j * tk, (j + 1) * tk)
        state = update(state, cckv_ref[0, ks, :], ckr_ref[0, :, ks], None, True)
    new_ckv = jnp.concatenate(
        [nckv_ref[0], jnp.zeros((LANES - new_rows, KV_LORA), F32)], axis=0)
    new_kr = jnp.concatenate([nkr_ref[0], jnp.zeros((LANES - new_rows, ROPE), F32)], axis=0)
    _, l_fin, acc = update(state, new_ckv, new_kr, new_rows, False)
    o_lat = (acc / l_fin).astype(BF16)
    for hh in range(H_C):
        o_ref[:, hh * V_C:(hh + 1) * V_C] = jnp.dot(
            o_lat[hh * steps:(hh + 1) * steps], wuv_ref[hh],
            preferred_element_type=F32).astype(o_ref.dtype)


def _latent_attn_sample(q_lat, q_r, cache_ckv, cache_kr, new_ckv, new_kr, w_uv, steps, tk):
    batch, past, _ = cache_ckv.shape
    pad_rows = new_ckv.shape[1]
    return pl.pallas_call(
        functools.partial(_latent_attn_kernel, steps=steps, new_rows=steps, tk=tk),
        grid=(batch,),
        in_specs=[pl.BlockSpec((H_C, steps, KV_LORA), lambda b: (0, b, 0)),
                  pl.BlockSpec((H_C, steps, ROPE), lambda b: (0, b, 0)),
                  pl.BlockSpec((1, past, KV_LORA), lambda b: (b, 0, 0)),
                  pl.BlockSpec((1, ROPE, past), lambda b: (b, 0, 0)),
                  pl.BlockSpec((1, pad_rows, KV_LORA), lambda b: (b, 0, 0)),
                  pl.BlockSpec((1, pad_rows, ROPE), lambda b: (b, 0, 0)),
                  pl.BlockSpec((H_C, KV_LORA, V_C), lambda b: (0, 0, 0))],
        out_specs=pl.BlockSpec((steps, H_C * V_C), lambda b: (b, 0)),
        out_shape=jax.ShapeDtypeStruct((batch * steps, H_C * V_C), BF16),
        compiler_params=_params(1),
        name="latent_attn_sample",
    )(q_lat, q_r, cache_ckv, cache_kr, new_ckv, new_kr, w_uv)


def _rope_tables(pos, width):
    half = ROPE // 2
    freqs = ROPE_THETA ** (-jnp.arange(half, dtype=F32) / half)
    ang = pos.astype(F32)[:, None] * freqs[None, :]
    cos, sin = jnp.cos(ang), jnp.sin(ang)
    cos_h = jnp.concatenate([cos, cos], axis=-1)
    sin_h = jnp.concatenate([-sin, sin], axis=-1)
    reps = width // ROPE
    return jnp.tile(cos_h, (1, reps)), jnp.tile(sin_h, (1, reps))


def _prep_even(w_in_ab, w_gate_up):
    sizes = [H_A * DK_A, H_A * DK_A, H_A * DV_A, H_A * DV_A, GATE_RANK,
             H_B * DK_B, H_B * DK_B, H_B * DV_B, H_B * DV_B]
    offs = np.concatenate([[0], np.cumsum(sizes)])
    parts = [w_in_ab[:, offs[n]:offs[n + 1]] for n in range(9)]
    lo = jnp.pad(parts[4], ((0, 0), (0, LANES - GATE_RANK)))
    w = jnp.concatenate(parts[0:4] + parts[5:9] + [lo], axis=1).astype(BF16)
    w_gu = jnp.pad(w_gate_up, ((0, LANES - GATE_RANK), (0, 0))).astype(BF16)
    return w, w_gu


def _prep_odd(w_in_c, w_uq, w_uk, w_uv):
    d = w_in_c.shape[0]
    w_in = jnp.pad(w_in_c, ((0, 0), (0, LANES - ROPE))).astype(BF16)
    uq = w_uq.reshape(Q_LORA, H_C, NOPE + ROPE)
    w_uq_p = jnp.concatenate([uq[:, :, :NOPE].reshape(Q_LORA, H_C * NOPE),
                              uq[:, :, NOPE:].reshape(Q_LORA, H_C * ROPE)],
                             axis=1).astype(BF16)
    w_ukv = jnp.concatenate([w_uk, w_uv], axis=2).reshape(KV_LORA, H_C * HEAD_W).astype(BF16)
    w_ukt = jnp.transpose(w_uk, (1, 2, 0)).astype(BF16)
    w_uvh = jnp.transpose(w_uv, (1, 0, 2)).astype(BF16)
    del d
    return w_in, w_uq_p, w_ukv, w_ukt, w_uvh


def _log_gamma_row():
    lg = np.log1p(-np.exp2(-5.0 - np.arange(H_B, dtype=np.float32))).astype(np.float32)
    return jnp.asarray(np.repeat(lg, DK_B)[None, :])


def _to_time_major(a, batch, steps):
    return a.reshape(batch, steps, -1).transpose(1, 0, 2).reshape(batch * steps, -1)


def _to_batch_major(a, batch, steps):
    return a.reshape(steps, batch, -1).transpose(1, 0, 2).reshape(batch * steps, -1)


def _divisor_tile(n, pref):
    t = min(n, pref)
    while n % t:
        t //= 2
    return t


def _trunk(x, pos0, gla0, ret0, ckv_past, kr_past, conv0, w, time_major_ffn):
    batch, seq, d = x.shape
    m = batch * seq
    pos = pos0 + jnp.arange(seq, dtype=jnp.int32)
    row = lambda v: v.reshape(1, -1)
    scale = float((NOPE + ROPE) ** -0.5)
    tf = TF_FFN
    tm_proj = _divisor_tile(m, TM_PROJ)

    if time_major_ffn:
        ffn_kw = dict(tm=m, tf=tf, shift=batch, tiles_per_seq=1)
        to_ffn = lambda a: _to_time_major(a, batch, seq)
        from_ffn = lambda a: _to_batch_major(a, batch, seq)
        prep_state = lambda s: s.transpose(1, 0, 2).reshape(1, 2 * batch, D_FF)
        post_state = lambda s: s.reshape(2, batch, D_FF).transpose(1, 0, 2)
    else:
        tm_ffn = _divisor_tile(seq, TM_FFN)
        ffn_kw = dict(tm=tm_ffn, tf=tf, shift=1, tiles_per_seq=seq // tm_ffn)
        to_ffn = from_ffn = lambda a: a
        prep_state = lambda s: jnp.pad(s, ((0, 0), (SUBLANES - 2, 0), (0, 0)))
        post_state = lambda s: s

    xf = x.reshape(m, d)
    conv_new = []
    gla_new = ret_new = ckv_new = kr_new = None
    for layer in range(2):
        g_mix = row(w['norm_mix'][layer])
        if layer == 0:
            w_ab, w_gu = _prep_even(w['w_in_ab'][0], w['w_gate_up'][0])
            cos, sin = _rope_tables(pos, 4 * DK_B)
            rows = _divisor_tile(seq, ROWS_LINATTN)
            small = (w_gu, row(w['b_gate'][0]), row(w['g_gla'][0]), row(w['g_ret'][0]), cos, sin,
                     _log_gamma_row())
            if gla0 is None and rows >= 2 * CHUNK:
                mix, gla_new, ret_new = _proj_linattn(xf, g_mix, w_ab, batch, seq, rows, *small)
            else:
                slab = _norm_matmul(xf, g_mix, w_ab, tm_proj, 4 * HEAD_W)
                mix, gla_new, ret_new = _linattn(
                    slab, batch, seq, rows, *small,
                    None if gla0 is None else gla0[0], None if ret0 is None else ret0[0])
            w_o = w['w_out_ab'][0].astype(BF16)
        else:
            w_in, w_uq_p, w_ukv, w_ukt, w_uvh = _prep_odd(w['w_in_c'][0], w['w_uq'][0], w['w_uk'][0],
                                                          w['w_uv'][0])
            cos, sin = _rope_tables(pos, LANES)
            g_q, g_kv = row(w['g_q'][0]), row(w['g_kv'][0])
            if ckv_past is None:
                q, k, v, ckv_new, kr_new = _mla_proj_prompt(
                    xf, g_mix, w_in, g_q, g_kv, w_uq_p, w_ukv, cos, sin, seq, tm_proj,
                    scale * float(np.log2(np.e)))
                mix = _flash_prompt(q, k, v, batch, seq, _divisor_tile(seq, TQ_FLASH))
            else:
                assert ckv_past.shape[2] % CHUNK == 0 and seq <= CHUNK
                cos_m, sin_m = jnp.tile(cos, (batch, 1)), jnp.tile(sin, (batch, 1))
                q_lat, q_r, ckv_new, kr_new = _mla_proj_sample(
                    xf, g_mix, w_in, g_q, g_kv, w_uq_p, w_ukt, cos_m, sin_m, scale)
                new_ckv = ckv_new.reshape(batch, seq, KV_LORA)
                new_kr = kr_new.reshape(batch, seq, ROPE)
                mix = _latent_attn_sample(q_lat, q_r, ckv_past[0],
                                          jnp.swapaxes(kr_past[0], 1, 2), new_ckv, new_kr,
                                          w_uvh, seq, _divisor_tile(ckv_past.shape[2], TK_LATENT))
            w_o = w['w_out_c'][0].astype(BF16)

        if conv0 is None:
            state = jnp.zeros((batch, SUBLANES, D_FF), F32) if not time_major_ffn else \
                jnp.zeros((1, 2 * batch, D_FF), F32)
        else:
            state = prep_state(conv0[layer])
        y, conv_rows = _ffn(to_ffn(xf), to_ffn(mix), w_o, row(w['norm_ffn'][layer]),
                            w['w_ffn_in'].astype(BF16), w['w_dwconv'][layer],
                            row(w['b_dwconv'][layer]),
                            (0.5 * w['w_ffn_out'][layer]).astype(BF16), state,
                            row(w['norm_final']), layer=layer, final_norm=(layer == 1), **ffn_kw)
        xf = from_ffn(y)
        tps = ffn_kw['tiles_per_seq']
        conv_new.append(post_state(conv_rows[tps - 1::tps]))

    return (xf.reshape(batch, seq, d), gla_new[None], ret_new[None],
            ckv_new.reshape(1, batch, seq, KV_LORA), kr_new.reshape(1, batch, seq, ROPE),
            jnp.stack(conv_new))


def kernel(x_prompt, x_sample, state_gla, state_ret, cache_ckv, cache_krope, state_conv, norm_mix, norm_ffn, norm_final, w_in_ab, w_gate_up, b_gate, g_gla, g_ret, w_out_ab, w_in_c, g_q, g_kv, w_uq, w_uk, w_uv, w_out_c, w_ffn_in, w_dwconv, b_dwconv, w_ffn_out):
    w = {'norm_mix': norm_mix, 'norm_ffn': norm_ffn, 'norm_final': norm_final,
         'w_in_ab': w_in_ab, 'w_gate_up': w_gate_up, 'b_gate': b_gate, 'g_gla': g_gla,
         'g_ret': g_ret, 'w_out_ab': w_out_ab, 'w_in_c': w_in_c, 'g_q': g_q, 'g_kv': g_kv,
         'w_uq': w_uq, 'w_uk': w_uk, 'w_uv': w_uv, 'w_out_c': w_out_c, 'w_ffn_in': w_ffn_in,
         'w_dwconv': w_dwconv, 'b_dwconv': b_dwconv, 'w_ffn_out': w_ffn_out}
    past_len = cache_ckv.shape[2]
    y_p, gla_p, ret_p, ckv_p, kr_p, conv_p = _trunk(
        x_prompt, 0, None, None, None, None, None, w, time_major_ffn=False)
    y_s, gla_s, ret_s, ckv_s, kr_s, conv_s = _trunk(
        x_sample, past_len, state_gla, state_ret, cache_ckv, cache_krope, state_conv, w,
        time_major_ffn=True)
    return (y_p, y_s, gla_p, gla_s, ret_p, ret_s, ckv_p, ckv_s, kr_p, kr_s, conv_p, conv_s)
```

```python
import functools

import numpy as np
import jax
import jax.numpy as jnp
from jax import lax
from jax.experimental import pallas as pl
from jax.experimental.pallas import tpu as pltpu

F32 = jnp.float32
BF16 = jnp.bfloat16

CHUNK = 64
EPS = 1e-6
ROPE_THETA = 10000.0
H_A, DK_A, DV_A = 4, 64, 128
GATE_RANK = 16
GATE_TAU = 16.0
H_B, DK_B, DV_B = 4, 64, 128
H_C = 8
Q_LORA, KV_LORA, NOPE, ROPE, V_C = 384, 512, 128, 64, 128
D_FF = 2816
CONV_W = 3

LANES = 128
SUBLANES = 8
VMEM_LIMIT = 56 * 1024 * 1024
AB_COLS = 3200
HEAD_W = 2 * LANES
TM_PROJ = 1024
TM_FFN = 1024
TF_FFN = D_FF
SUB_FFN = 512
ROWS_LINATTN = 256
TQ_FLASH = 256
TK_LATENT = 4096


def _params(n_axes):
    return pltpu.CompilerParams(dimension_semantics=("arbitrary",) * n_axes,
                                vmem_limit_bytes=VMEM_LIMIT)


def _rms(x, g):
    return x * lax.rsqrt(jnp.mean(x * x, axis=-1, keepdims=True) + EPS) * g


def _rope_first_half(shape):
    lane = lax.broadcasted_iota(jnp.int32, shape, 1)
    return (lane % ROPE) < ROPE // 2


def _rope(x, cos, sin_signed, first=None):
    w = x.shape[1]
    half = ROPE // 2
    if first is None:
        first = _rope_first_half(x.shape)
    swapped = jnp.where(first, pltpu.roll(x, w - half, 1), pltpu.roll(x, half, 1))
    return x * cos + swapped * sin_signed


def _nt_dot(a, b):
    return lax.dot_general(a, b, (((1,), (1,)), ((), ())), preferred_element_type=F32)


def _norm_matmul_kernel(x_ref, g_ref, w_ref, o_ref, *, col_chunk):
    h = _rms(x_ref[...], g_ref[...]).astype(BF16)
    n = w_ref.shape[1]
    for c0 in range(0, n, col_chunk):
        c1 = min(c0 + col_chunk, n)
        o_ref[:, c0:c1] = jnp.dot(h, w_ref[:, c0:c1], preferred_element_type=F32).astype(o_ref.dtype)


def _norm_matmul(x, g, w, tm, col_chunk):
    m, d = x.shape
    n = w.shape[1]
    return pl.pallas_call(
        functools.partial(_norm_matmul_kernel, col_chunk=col_chunk),
        grid=(m // tm,),
        in_specs=[pl.BlockSpec((tm, d), lambda i: (i, 0)),
                  pl.BlockSpec((1, d), lambda i: (0, 0)),
                  pl.BlockSpec((d, n), lambda i: (0, 0))],
        out_specs=pl.BlockSpec((tm, n), lambda i: (i, 0)),
        out_shape=jax.ShapeDtypeStruct((m, n), BF16),
        compiler_params=_params(1),
        name="norm_inproj_ab",
    )(x, g, w)


def _decay_attention_streams(streams, mask_k, mask_v, n_chunks, mid, last, between):
    c = CHUNK
    assert 2 * c == LANES
    hk, hv = 4 * DK_A, 4 * DV_A
    rt = lax.broadcasted_iota(jnp.int32, (c, 4 * c), 0)
    ct = lax.broadcasted_iota(jnp.int32, (c, 4 * c), 1)
    causal = (ct % c) <= rt
    lane = lax.broadcasted_iota(jnp.int32, (DK_A, LANES), 1)
    zero_v = jnp.zeros((LANES - c, hv), BF16)
    zero_blk = jnp.zeros((DK_A, DV_A), BF16)
    log2e = np.float32(np.log2(np.e))

    items = []
    for q, k, v, bcum, s_ref in streams:
        bcum2 = bcum * log2e
        for n in range(n_chunks):
            sl = slice(n * c, (n + 1) * c)
            qn, kn, vn, bn = q[sl], k[sl], v[sl], bcum2[sl]
            b_ref = bn[mid:mid + 1]
            b_last = bn[last:last + 1]
            it = dict(s_ref=s_ref, vn=vn)
            it['q_rel'] = (qn * jnp.exp2(bn - b_ref)).astype(BF16)
            k_rel = (kn * jnp.exp2(b_ref - bn)).astype(BF16)
            it['k_bd'] = jnp.concatenate([k_rel] * 4, axis=0) * mask_k
            it['q_dec'] = (qn * jnp.exp2(bn)).astype(BF16)
            k_dec = kn * jnp.exp2(b_last - bn)
            decay_rows = jnp.broadcast_to(jnp.exp2(b_last), (LANES - c, hk))
            it['kt'] = jnp.concatenate([k_dec, decay_rows], axis=0).T
            items.append(it)
    between()
    for it in items:
        it['scores'] = jnp.where(causal, _nt_dot(it['q_rel'], it['k_bd']), 0.0).astype(BF16)
    between()
    for it in items:
        v_bd = jnp.concatenate([it['vn']] * 4, axis=0) * mask_v
        it['o'] = jnp.dot(it['scores'], v_bd, preferred_element_type=F32)
        kt_b = it['kt'].astype(BF16)
        v_pad = jnp.concatenate([it['vn'], zero_v], axis=0)
        it['kv'] = [jnp.dot(kt_b[h * DK_A:(h + 1) * DK_A], v_pad[:, h * DV_A:(h + 1) * DV_A],
                            preferred_element_type=F32) for h in range(4)]
    between()
    for it in items:
        s_ref = it['s_ref']
        s_old = [s_ref[h] for h in range(4)]
        it['s_bd'] = jnp.concatenate(
            [jnp.concatenate([zero_blk] * h + [s_old[h].astype(BF16)] + [zero_blk] * (3 - h), axis=1)
             for h in range(4)], axis=0)
        for h in range(4):
            kth = it['kt'][h * DK_A:(h + 1) * DK_A]
            decay = jnp.where(lane < c, pltpu.roll(kth, c, 1), kth)
            s_ref[h] = s_old[h] * decay + it['kv'][h]
    between()
    outs = []
    for si in range(len(streams)):
        chunk_out = [it['o'] + jnp.dot(it['q_dec'], it['s_bd'], preferred_element_type=F32)
                     for it in items[si * n_chunks:(si + 1) * n_chunks]]
        outs.append(chunk_out[0] if n_chunks == 1 else jnp.concatenate(chunk_out, axis=0))
    between()
    return outs


_SLAB = dict(qa=(0, 256), ka=(256, 256), va=(512, 512), ra=(1024, 512),
             qb=(1536, 256), kb=(1792, 256), vb=(2048, 512), gb=(2560, 512), lo=(3072, LANES))


def _linattn_compute(cols, wgu_ref, bg_ref, gg_ref, gr_ref, cos_ref, sin_ref, lgam_ref, tri_ref,
                     mk_ref, mv_ref, mix_ref, sa, sb, rows, between=lambda: None):
    c = CHUNK
    padded = max(rows, c)
    n_chunks = padded // c
    valid = min(rows, c)
    mid, last = (valid - 1) // 2, valid - 1

    def pad_rows(x):
        if padded == rows:
            return x
        return jnp.concatenate([x, jnp.zeros((padded - rows, x.shape[1]), x.dtype)], axis=0)

    tri = tri_ref[...]
    mask_k, mask_v = mk_ref[...], mv_ref[...]

    gate = jnp.dot(cols['lo'], wgu_ref[...], preferred_element_type=F32) + bg_ref[...]
    log_a = (jnp.minimum(gate, 0.0) - jnp.log1p(jnp.exp(-jnp.abs(gate)))) / GATE_TAU
    log_a = pad_rows(log_a)
    hi = log_a.astype(BF16)
    lo = (log_a - hi.astype(F32)).astype(BF16)
    bcum_a = (jnp.dot(tri, hi, preferred_element_type=F32)
              + jnp.dot(tri, lo, preferred_element_type=F32))
    qa = pad_rows(cols['qa'].astype(F32) * DK_A ** -0.5)
    ka = pad_rows(cols['ka'].astype(F32))
    va = pad_rows(cols['va'])
    between()

    cos, sin = cos_ref[...], sin_ref[...]
    first = _rope_first_half(cos.shape)
    qb = pad_rows(_rope(cols['qb'].astype(F32), cos, sin, first))
    kb = pad_rows(_rope(cols['kb'].astype(F32), cos, sin, first) * DK_B ** -0.5)
    vb = pad_rows(cols['vb'])
    pos = lax.broadcasted_iota(jnp.int32, (padded, 4 * DK_B), 0) % c
    steps = jnp.where(pos < valid, pos + 1, valid).astype(F32)
    bcum_b = steps * lgam_ref[...]
    o_a, o_b = _decay_attention_streams(
        [(qa, ka, va, bcum_a, sa), (qb, kb, vb, bcum_b, sb)], mask_k, mask_v, n_chunks, mid, last,
        between)
    o_a, o_b = o_a[:rows], o_b[:rows]

    gg, gr = gg_ref[...], gr_ref[...]
    for h in range(4):
        sl = slice(h * DV_A, (h + 1) * DV_A)
        oh = o_a[:, sl]
        oh = oh * lax.rsqrt(jnp.mean(oh * oh, axis=-1, keepdims=True) + EPS) * gg[:, sl]
        r = cols['ra'][:, sl].astype(F32)
        mix_ref[:, sl] = (oh * (r * jax.nn.sigmoid(r))).astype(mix_ref.dtype)
        ob = o_b[:, sl]
        ob = ob - jnp.mean(ob, axis=-1, keepdims=True)
        ob = ob * lax.rsqrt(jnp.mean(ob * ob, axis=-1, keepdims=True) + EPS) * gr[:, sl]
        gb = cols['gb'][:, sl].astype(F32)
        mix_ref[:, 4 * DV_A + h * DV_B:4 * DV_A + (h + 1) * DV_B] = (
            ob * (gb * jax.nn.sigmoid(gb))).astype(mix_ref.dtype)


def _linattn_kernel(qa_ref, ka_ref, va_ref, ra_ref, qb_ref, kb_ref, vb_ref, gb_ref, lo_ref,
                    wgu_ref, bg_ref, gg_ref, gr_ref, cos_ref, sin_ref, lgam_ref, tri_ref, mk_ref,
                    mv_ref, *rest, rows, has_state):
    if has_state:
        s0a_ref, s0b_ref, mix_ref, sa_out, sb_out, sa, sb = rest
    else:
        mix_ref, sa_out, sb_out, sa, sb = rest
    t = pl.program_id(1)

    @pl.when(t == 0)
    def _init():
        if has_state:
            sa[...] = s0a_ref[0]
            sb[...] = s0b_ref[0]
        else:
            sa[...] = jnp.zeros_like(sa)
            sb[...] = jnp.zeros_like(sb)

    cols = dict(qa=qa_ref[...], ka=ka_ref[...], va=va_ref[...], ra=ra_ref[...], qb=qb_ref[...],
                kb=kb_ref[...], vb=vb_ref[...], gb=gb_ref[...], lo=lo_ref[...])
    _linattn_compute(cols, wgu_ref, bg_ref, gg_ref, gr_ref, cos_ref, sin_ref, lgam_ref, tri_ref,
                     mk_ref, mv_ref, mix_ref, sa, sb, rows)

    @pl.when(t == pl.num_programs(1) - 1)
    def _emit_state():
        sa_out[0] = sa[...]
        sb_out[0] = sb[...]


def _proj_linattn_kernel(x_ref, g_ref, w_ref, wgu_ref, bg_ref, gg_ref, gr_ref, cos_ref, sin_ref,
                         lgam_ref, tri_ref, mk_ref, mv_ref, mix_ref, sa_out, sb_out, slab_even,
                         slab_odd, sa, sb, *, rows, nt, col_chunk):
    s = pl.program_id(0)
    t = jnp.maximum(s - 1, 0) % nt

    @pl.when(s == 0)
    def _zero_slab():
        slab_odd[...] = jnp.zeros_like(slab_odd)

    @pl.when(t == 0)
    def _init():
        sa[...] = jnp.zeros_like(sa)
        sb[...] = jnp.zeros_like(sb)

    def step(write_ref, read_ref):
        cols = {name: read_ref[:, off:off + width] for name, (off, width) in _SLAB.items()}
        h = _rms(x_ref[...], g_ref[...]).astype(BF16)
        n = w_ref.shape[1]
        pieces = list(range(0, n, col_chunk))
        shares = (5, 1, 0, 1, 1, 5)
        total = len(pieces)
        quotas = [total * sum(shares[:k + 1]) // sum(shares) - total * sum(shares[:k]) // sum(shares)
                  for k in range(len(shares))]

        def project(count):
            for _ in range(min(count, len(pieces))):
                c0 = pieces.pop(0)
                c1 = min(c0 + col_chunk, n)
                write_ref[:, c0:c1] = jnp.dot(h, w_ref[:, c0:c1],
                                              preferred_element_type=F32).astype(write_ref.dtype)

        _linattn_compute(cols, wgu_ref, bg_ref, gg_ref, gr_ref, cos_ref, sin_ref, lgam_ref,
                         tri_ref, mk_ref, mv_ref, mix_ref, sa, sb, rows,
                         between=lambda: project(quotas.pop(0) if quotas else 0))
        project(len(pieces))

    @pl.when(s % 2 == 0)
    def _even():
        step(slab_even, slab_odd)

    @pl.when(s % 2 == 1)
    def _odd():
        step(slab_odd, slab_even)

    @pl.when(t == nt - 1)
    def _emit_state():
        sa_out[0] = sa[...]
        sb_out[0] = sb[...]


def _linattn_masks(rows):
    c = CHUNK
    padded = max(rows, c)
    r = np.arange(padded)
    tri = jnp.asarray(((r[:, None] // c == r[None, :] // c) & (r[None, :] <= r[:, None])), BF16)
    r4 = np.arange(4 * c)
    mask_k = jnp.asarray(r4[:, None] // c == np.arange(4 * DK_A)[None, :] // DK_A, BF16)
    mask_v = jnp.asarray(r4[:, None] // c == np.arange(4 * DV_A)[None, :] // DV_A, BF16)
    return tri, mask_k, mask_v


def _proj_linattn(x, g, w_ab, batch, seq, rows, w_gu, b_gate, g_gla, g_ret, cos, sin, lgam):
    m, d = x.shape
    nt = seq // rows
    n_tiles = batch * nt
    tri, mask_k, mask_v = _linattn_masks(rows)
    hk, hv = 4 * DK_A, 4 * DV_A
    prev = lambda s: jnp.maximum(s - 1, 0)

    def const(shape):
        return pl.BlockSpec(shape, lambda s: (0,) * len(shape))

    state_spec = pl.BlockSpec((1, 4, DK_A, DV_A), lambda s: (prev(s) // nt, 0, 0, 0))
    state_shape = jax.ShapeDtypeStruct((batch, 4, DK_A, DV_A), F32)
    return pl.pallas_call(
        functools.partial(_proj_linattn_kernel, rows=rows, nt=nt, col_chunk=HEAD_W),
        grid=(n_tiles + 1,),
        in_specs=[pl.BlockSpec((rows, d), lambda s: (jnp.minimum(s, n_tiles - 1), 0)),
                  const((1, d)), const(w_ab.shape),
                  const((LANES, hk)), const((1, hk)), const((1, hv)), const((1, hv)),
                  pl.BlockSpec((rows, hk), lambda s: (prev(s) % nt, 0)),
                  pl.BlockSpec((rows, hk), lambda s: (prev(s) % nt, 0)),
                  const((1, hk)), const(tri.shape), const(mask_k.shape), const(mask_v.shape)],
        out_specs=[pl.BlockSpec((rows, 2 * hv), lambda s: (prev(s), 0)), state_spec, state_spec],
        out_shape=[jax.ShapeDtypeStruct((m, 2 * hv), BF16), state_shape, state_shape],
        scratch_shapes=[pltpu.VMEM((rows, AB_COLS), BF16), pltpu.VMEM((rows, AB_COLS), BF16),
                        pltpu.VMEM((4, DK_A, DV_A), F32), pltpu.VMEM((4, DK_B, DV_B), F32)],
        compiler_params=_params(1),
        name="inproj_decay_linear_attention",
    )(x, g, w_ab, w_gu, b_gate, g_gla, g_ret, cos, sin, lgam, tri, mask_k, mask_v)


def _linattn(slab, batch, seq, rows, w_gu, b_gate, g_gla, g_ret, cos, sin, lgam, s0a, s0b):
    nt = seq // rows
    has_state = s0a is not None
    tri, mask_k, mask_v = _linattn_masks(rows)

    def col(width, idx):
        return pl.BlockSpec((rows, width), lambda b, t, idx=idx: (b * nt + t, idx))

    def const(shape):
        return pl.BlockSpec(shape, lambda b, t: (0,) * len(shape))

    hk = 4 * DK_A
    hv = 4 * DV_A
    in_specs = [col(hk, 0), col(hk, 1), col(hv, 1), col(hv, 2),
                col(hk, 6), col(hk, 7), col(hv, 4), col(hv, 5),
                col(LANES, 24),
                const((LANES, hk)), const((1, hk)), const((1, hv)), const((1, hv)),
                pl.BlockSpec((rows, hk), lambda b, t: (t, 0)),
                pl.BlockSpec((rows, hk), lambda b, t: (t, 0)),
                const((1, hk)), const(tri.shape), const(mask_k.shape), const(mask_v.shape)]
    args = [slab] * 9 + [w_gu, b_gate, g_gla, g_ret, cos, sin, lgam, tri, mask_k, mask_v]
    state_spec = pl.BlockSpec((1, 4, DK_A, DV_A), lambda b, t: (b, 0, 0, 0))
    if has_state:
        in_specs += [state_spec, state_spec]
        args += [s0a, s0b]
    state_shape = jax.ShapeDtypeStruct((batch, 4, DK_A, DV_A), F32)
    return pl.pallas_call(
        functools.partial(_linattn_kernel, rows=rows, has_state=has_state),
        grid=(batch, nt),
        in_specs=in_specs,
        out_specs=[pl.BlockSpec((rows, 2 * hv), lambda b, t: (b * nt + t, 0)), state_spec, state_spec],
        out_shape=[jax.ShapeDtypeStruct((batch * seq, 2 * hv), BF16), state_shape, state_shape],
        scratch_shapes=[pltpu.VMEM((4, DK_A, DV_A), F32), pltpu.VMEM((4, DK_B, DV_B), F32)],
        compiler_params=_params(2),
        name="decay_linear_attention",
    )(*args)


def _ffn_kernel(x_ref, mix_ref, wo_ref, g_ref, wa_ref, wu_ref, wdw_ref, bdw_ref, wout_ref, st_ref,
                gf_ref, y_ref, cv_ref, h_s, carry_s,
                *, tm, sub, prev_rows, shift, tiles_per_seq, final_norm, single_step):
    i = pl.program_id(0)
    j = pl.program_id(1)
    p = prev_rows
    subs = [slice(r, r + sub) for r in range(0, tm, sub)]

    def mid_residual(rs):
        xm = x_ref[rs] + jnp.dot(mix_ref[rs], wo_ref[...], preferred_element_type=F32)
        return xm, _rms(xm, g_ref[...]).astype(BF16)

    if not single_step:
        @pl.when(j == 0)
        def _start():
            for rs in subs:
                y_ref[rs], h_s[rs] = mid_residual(rs)

    at_start = (i % tiles_per_seq) == 0
    tail = jnp.where(at_start, st_ref[0], carry_s[j])
    last = j == pl.num_programs(1) - 1
    wdw = wdw_ref[...]
    bdw = bdw_ref[...]
    a = None
    for rs in subs:
        if single_step:
            base, h = mid_residual(rs)
        else:
            base, h = y_ref[rs], h_s[rs]
        a = jnp.dot(h, wa_ref[...], preferred_element_type=F32)
        u = jnp.dot(h, wu_ref[...], preferred_element_type=F32)
        ext = jnp.concatenate([tail, a], axis=0)
        conv = bdw + pltpu.roll(ext, 2 * shift, 0)[p:] * wdw[0:1]
        conv = conv + pltpu.roll(ext, shift, 0)[p:] * wdw[1:2]
        conv = conv + a * wdw[2:3]
        act = conv * (1.0 + lax.erf(conv * np.float32(np.sqrt(0.5)))) * u
        out = base + jnp.dot(act.astype(BF16), wout_ref[...], preferred_element_type=F32)
        if final_norm:
            normed = _rms(out, gf_ref[...])
            out = normed if single_step else jnp.where(last, normed, out)
        y_ref[rs] = out
        tail = a[sub - p:]
    carry_s[j] = tail
    cv_ref[0] = a[sub - 2 * shift:]


def _ffn(x, mix, w_o, g_ffn, w_in, w_dw, b_dw, w_out, conv_state, g_final, *, layer, tm, tf, shift,
         tiles_per_seq, final_norm):
    m, d = x.shape
    n_f = D_FF // tf
    n_seq, prev_rows, _ = conv_state.shape
    kern = functools.partial(_ffn_kernel, tm=tm, sub=min(tm, SUB_FFN), prev_rows=prev_rows,
                             shift=shift, tiles_per_seq=tiles_per_seq, final_norm=final_norm,
                             single_step=(n_f == 1))
    h_rows = SUBLANES * 2 if n_f == 1 else tm
    return pl.pallas_call(
        kern,
        grid=(m // tm, n_f),
        in_specs=[pl.BlockSpec((tm, d), lambda i, j: (i, 0)),
                  pl.BlockSpec((tm, d), lambda i, j: (i, 0)),
                  pl.BlockSpec((d, d), lambda i, j: (0, 0)),
                  pl.BlockSpec((1, d), lambda i, j: (0, 0)),
                  pl.BlockSpec((None, d, tf), lambda i, j: (layer, 0, j)),
                  pl.BlockSpec((None, d, tf), lambda i, j: (layer, 0, n_f + j)),
                  pl.BlockSpec((CONV_W, tf), lambda i, j: (0, j)),
                  pl.BlockSpec((1, tf), lambda i, j: (0, j)),
                  pl.BlockSpec((tf, d), lambda i, j: (j, 0)),
                  pl.BlockSpec((1, prev_rows, tf), lambda i, j: (i // tiles_per_seq, 0, j)),
                  pl.BlockSpec((1, d), lambda i, j: (0, 0))],
        out_specs=[pl.BlockSpec((tm, d), lambda i, j: (i, 0)),
                   pl.BlockSpec((1, 2 * shift, tf), lambda i, j: (i, 0, j))],
        out_shape=[jax.ShapeDtypeStruct((m, d), F32),
                   jax.ShapeDtypeStruct((m // tm, 2 * shift, D_FF), F32)],
        scratch_shapes=[pltpu.VMEM((h_rows, d), BF16), pltpu.VMEM((n_f, prev_rows, tf), F32)],
        compiler_params=_params(2),
        name="outproj_convffn",
    )(x, mix, w_o, g_ffn, w_in, w_in, w_dw, b_dw, w_out, conv_state, g_final)


def _mla_latents(x_ref, g_ref, win_ref, gq_ref, gkv_ref, cos_ref, sin_ref, ckv_ref, kr_ref):
    h = _rms(x_ref[...], g_ref[...]).astype(BF16)
    p = jnp.dot(h, win_ref[...], preferred_element_type=F32)
    cq_n = _rms(p[:, 0:Q_LORA], gq_ref[...]).astype(BF16)
    ckv_n = _rms(p[:, Q_LORA:Q_LORA + KV_LORA], gkv_ref[...])
    ckv_ref[...] = ckv_n
    kr = _rope(p[:, Q_LORA + KV_LORA:], cos_ref[...], sin_ref[...])
    kr_ref[...] = kr[:, 0:ROPE]
    return cq_n, ckv_n, kr


def _q_heads(cq_n, wuq_ref, cos, sin, scale):
    n_nope = H_C * NOPE
    nope, rope = [], []
    for c0 in range(0, n_nope, HEAD_W):
        blk = jnp.dot(cq_n, wuq_ref[:, c0:c0 + HEAD_W], preferred_element_type=F32) * scale
        nope += [blk[:, :NOPE], blk[:, NOPE:]]
    low = lax.broadcasted_iota(jnp.int32, cos.shape, 1) < ROPE
    for c0 in range(n_nope, n_nope + H_C * ROPE, HEAD_W):
        blk = jnp.dot(cq_n, wuq_ref[:, c0:c0 + HEAD_W], preferred_element_type=F32) * scale
        for half in range(HEAD_W // LANES):
            pair = _rope(blk[:, half * LANES:(half + 1) * LANES], cos, sin)
            rope.append(jnp.where(low, pair, 0.0))
            rope.append(jnp.where(low, pltpu.roll(pair, ROPE, 1), 0.0))
    return nope, rope


def _mla_proj_prompt_kernel(x_ref, g_ref, win_ref, gq_ref, gkv_ref, wuq_ref, wukv_ref, cos_ref,
                            sin_ref, q_ref, k_ref, v_ref, ckv_ref, kr_ref, *, scale):
    cq_n, ckv_n, kr = _mla_latents(x_ref, g_ref, win_ref, gq_ref, gkv_ref, cos_ref, sin_ref,
                                   ckv_ref, kr_ref)
    q_nope, q_rope = _q_heads(cq_n, wuq_ref, cos_ref[...], sin_ref[...], scale)
    kr_b = kr.astype(BF16)
    ckv_b = ckv_n.astype(BF16)
    for hh in range(H_C):
        q_ref[hh, :, 0:NOPE] = q_nope[hh].astype(BF16)
        q_ref[hh, :, NOPE:HEAD_W] = q_rope[hh].astype(BF16)
        kvh = jnp.dot(ckv_b, wukv_ref[:, hh * HEAD_W:(hh + 1) * HEAD_W],
                      preferred_element_type=F32)
        k_ref[hh, :, 0:NOPE] = kvh[:, 0:NOPE].astype(BF16)
        k_ref[hh, :, NOPE:HEAD_W] = kr_b
        v_ref[hh] = kvh[:, NOPE:HEAD_W].astype(BF16)


def _mla_proj_prompt(x, g, w_in, g_q, g_kv, w_uq, w_ukv, cos, sin, seq, tm, scale):
    m, d = x.shape
    nt = seq // tm
    const2 = lambda shape: pl.BlockSpec(shape, lambda i: (0, 0))
    return pl.pallas_call(
        functools.partial(_mla_proj_prompt_kernel, scale=scale),
        grid=(m // tm,),
        in_specs=[pl.BlockSpec((tm, d), lambda i: (i, 0)), const2((1, d)), const2(w_in.shape),
                  const2((1, Q_LORA)), const2((1, KV_LORA)), const2(w_uq.shape), const2(w_ukv.shape),
                  pl.BlockSpec((tm, LANES), lambda i: (i % nt, 0)),
                  pl.BlockSpec((tm, LANES), lambda i: (i % nt, 0))],
        out_specs=[pl.BlockSpec((H_C, tm, HEAD_W), lambda i: (0, i, 0)),
                   pl.BlockSpec((H_C, tm, HEAD_W), lambda i: (0, i, 0)),
                   pl.BlockSpec((H_C, tm, V_C), lambda i: (0, i, 0)),
                   pl.BlockSpec((tm, KV_LORA), lambda i: (i, 0)),
                   pl.BlockSpec((tm, ROPE), lambda i: (i, 0))],
        out_shape=[jax.ShapeDtypeStruct((H_C, m, HEAD_W), BF16),
                   jax.ShapeDtypeStruct((H_C, m, HEAD_W), BF16),
                   jax.ShapeDtypeStruct((H_C, m, V_C), BF16),
                   jax.ShapeDtypeStruct((m, KV_LORA), F32),
                   jax.ShapeDtypeStruct((m, ROPE), F32)],
        compiler_params=_params(1),
        name="mla_proj_prompt",
    )(x, g, w_in, g_q, g_kv, w_uq, w_ukv, cos, sin)


def _mla_proj_sample_kernel(x_ref, g_ref, win_ref, gq_ref, gkv_ref, wuq_ref, wukt_ref, cos_ref,
                            sin_ref, qlat_ref, qr_ref, ckv_ref, kr_ref, *, scale):
    cq_n, _, _ = _mla_latents(x_ref, g_ref, win_ref, gq_ref, gkv_ref, cos_ref, sin_ref,
                              ckv_ref, kr_ref)
    q_nope, q_rope = _q_heads(cq_n, wuq_ref, cos_ref[...], sin_ref[...], scale)
    for hh in range(H_C):
        q_lat = jnp.dot(q_nope[hh].astype(BF16), wukt_ref[hh], preferred_element_type=F32)
        qlat_ref[hh] = q_lat.astype(BF16)
        qr_ref[hh] = q_rope[hh][:, 0:ROPE].astype(BF16)


def _mla_proj_sample(x, g, w_in, g_q, g_kv, w_uq, w_ukt, cos, sin, scale):
    m, d = x.shape
    const = lambda shape: pl.BlockSpec(shape, lambda i: (0,) * len(shape))
    return pl.pallas_call(
        functools.partial(_mla_proj_sample_kernel, scale=scale),
        grid=(1,),
        in_specs=[const((m, d)), const((1, d)), const(w_in.shape), const((1, Q_LORA)),
                  const((1, KV_LORA)), const(w_uq.shape), const(w_ukt.shape),
                  const((m, LANES)), const((m, LANES))],
        out_specs=[const((H_C, m, KV_LORA)), const((H_C, m, ROPE)), const((m, KV_LORA)),
                   const((m, ROPE))],
        out_shape=[jax.ShapeDtypeStruct((H_C, m, KV_LORA), BF16),
                   jax.ShapeDtypeStruct((H_C, m, ROPE), BF16),
                   jax.ShapeDtypeStruct((m, KV_LORA), F32),
                   jax.ShapeDtypeStruct((m, ROPE), F32)],
        compiler_params=_params(1),
        name="mla_proj_sample",
    )(x, g, w_in, g_q, g_kv, w_uq, w_ukt, cos, sin)


def _flash_kernel(q_ref, k_ref, v_ref, o_ref, m_s, acc_s, *, tile, n_tiles):
    t = tile
    row = lax.broadcasted_iota(jnp.int32, (t, t), 0)
    col = lax.broadcasted_iota(jnp.int32, (t, t), 1)
    visible = (col // CHUNK) <= (row // CHUNK)
    ones = jnp.ones((t, V_C), BF16)
    for j in range(n_tiles):
        ks = slice(j * t, (j + 1) * t)
        kj = k_ref[0, ks, :]
        vj = jnp.concatenate([v_ref[0, ks, :], ones], axis=1)
        for i in range(j, n_tiles):
            qs = slice(i * t, (i + 1) * t)
            s = _nt_dot(q_ref[0, qs, :], kj)
            if i == j:
                s = jnp.where(visible, s, -jnp.inf)
            m_new = jnp.broadcast_to(jnp.max(s, axis=-1, keepdims=True), (t, LANES))
            if j > 0:
                m_prev = m_s[qs]
                m_new = jnp.maximum(m_prev, m_new)
            p = jnp.exp2(s - jnp.concatenate([m_new] * (t // LANES), axis=1))
            pv = jnp.dot(p.astype(BF16), vj, preferred_element_type=F32)
            if j > 0:
                alpha = jnp.exp2(m_prev - m_new)
                pv = pv + jnp.concatenate([alpha, alpha], axis=1) * acc_s[qs]
            if i == j:
                o_ref[qs, :] = (pv[:, :V_C] / pv[:, V_C:]).astype(o_ref.dtype)
            else:
                m_s[qs] = m_new
                acc_s[qs] = pv


def _flash_prompt(q, k, v, batch, seq, tile):
    m = batch * seq
    return pl.pallas_call(
        functools.partial(_flash_kernel, tile=tile, n_tiles=seq // tile),
        grid=(batch, H_C),
        in_specs=[pl.BlockSpec((1, seq, HEAD_W), lambda b, h: (h, b, 0)),
                  pl.BlockSpec((1, seq, HEAD_W), lambda b, h: (h, b, 0)),
                  pl.BlockSpec((1, seq, V_C), lambda b, h: (h, b, 0))],
        out_specs=pl.BlockSpec((seq, V_C), lambda b, h: (b, h)),
        out_shape=jax.ShapeDtypeStruct((m, H_C * V_C), BF16),
        scratch_shapes=[pltpu.VMEM((seq, LANES), F32), pltpu.VMEM((seq, 2 * V_C), F32)],
        compiler_params=_params(2),
        name="flash_prompt",
    )(q, k, v)


def _latent_attn_kernel(qlat_ref, qr_ref, cckv_ref, ckr_ref, nckv_ref, nkr_ref, wuv_ref, o_ref,
                        *, steps, new_rows, tk):
    rows = H_C * steps
    q_lat = qlat_ref[...].reshape(rows, KV_LORA)
    q_r = qr_ref[...].reshape(rows, ROPE)

    def update(state, ckv, kr, valid, kr_transposed):
        ckv_b = ckv.astype(BF16)
        if kr_transposed:
            s_r = jnp.dot(q_r, kr.astype(BF16), preferred_element_type=F32)
        else:
            s_r = _nt_dot(q_r, kr.astype(BF16))
        s = _nt_dot(q_lat, ckv_b) + s_r
        if valid is not None:
            col = lax.broadcasted_iota(jnp.int32, s.shape, 1)
            s = jnp.where(col < valid, s, -jnp.inf)
        m_blk = jnp.max(s, axis=-1, keepdims=True)
        if state is None:
            m_new = m_blk
            p = jnp.exp(s - m_new)
            return (m_new, jnp.sum(p, axis=-1, keepdims=True),
                    jnp.dot(p.astype(BF16), ckv_b, preferred_element_type=F32))
        m_prev, l_prev, acc = state
        m_new = jnp.maximum(m_prev, m_blk)
        p = jnp.exp(s - m_new)
        alpha = jnp.exp(m_prev - m_new)
        return (m_new, alpha * l_prev + jnp.sum(p, axis=-1, keepdims=True),
                alpha * acc + jnp.dot(p.astype(BF16), ckv_b, preferred_element_type=F32))

    state = None
    for j in range(cckv_ref.shape[1] // tk):
        ks = slice(j * tk, (j + 1) * tk)
        state = update(state, cckv_ref[0, ks, :], ckr_ref[0, :, ks], None, True)
    new_ckv = jnp.concatenate(
        [nckv_ref[0], jnp.zeros((LANES - new_rows, KV_LORA), F32)], axis=0)
    new_kr = jnp.concatenate([nkr_ref[0], jnp.zeros((LANES - new_rows, ROPE), F32)], axis=0)
    _, l_fin, acc = update(state, new_ckv, new_kr, new_rows, False)
    o_lat = (acc / l_fin).astype(BF16)
    for hh in range(H_C):
        o_ref[:, hh * V_C:(hh + 1) * V_C] = jnp.dot(
            o_lat[hh * steps:(hh + 1) * steps], wuv_ref[hh],
            preferred_element_type=F32).astype(o_ref.dtype)


def _latent_attn_sample(q_lat, q_r, cache_ckv, cache_kr, new_ckv, new_kr, w_uv, steps, tk):
    batch, past, _ = cache_ckv.shape
    pad_rows = new_ckv.shape[1]
    return pl.pallas_call(
        functools.partial(_latent_attn_kernel, steps=steps, new_rows=steps, tk=tk),
        grid=(batch,),
        in_specs=[pl.BlockSpec((H_C, steps, KV_LORA), lambda b: (0, b, 0)),
                  pl.BlockSpec((H_C, steps, ROPE), lambda b: (0, b, 0)),
                  pl.BlockSpec((1, past, KV_LORA), lambda b: (b, 0, 0)),
                  pl.BlockSpec((1, ROPE, past), lambda b: (b, 0, 0)),
                  pl.BlockSpec((1, pad_rows, KV_LORA), lambda b: (b, 0, 0)),
                  pl.BlockSpec((1, pad_rows, ROPE), lambda b: (b, 0, 0)),
                  pl.BlockSpec((H_C, KV_LORA, V_C), lambda b: (0, 0, 0))],
        out_specs=pl.BlockSpec((steps, H_C * V_C), lambda b: (b, 0)),
        out_shape=jax.ShapeDtypeStruct((batch * steps, H_C * V_C), BF16),
        compiler_params=_params(1),
        name="latent_attn_sample",
    )(q_lat, q_r, cache_ckv, cache_kr, new_ckv, new_kr, w_uv)


def _rope_tables(pos, width):
    half = ROPE // 2
    freqs = ROPE_THETA ** (-jnp.arange(half, dtype=F32) / half)
    ang = pos.astype(F32)[:, None] * freqs[None, :]
    cos, sin = jnp.cos(ang), jnp.sin(ang)
    cos_h = jnp.concatenate([cos, cos], axis=-1)
    sin_h = jnp.concatenate([-sin, sin], axis=-1)
    reps = width // ROPE
    return jnp.tile(cos_h, (1, reps)), jnp.tile(sin_h, (1, reps))


def _prep_even(w_in_ab, w_gate_up):
    sizes = [H_A * DK_A, H_A * DK_A, H_A * DV_A, H_A * DV_A, GATE_RANK,
             H_B * DK_B, H_B * DK_B, H_B * DV_B, H_B * DV_B]
    offs = np.concatenate([[0], np.cumsum(sizes)])
    parts = [w_in_ab[:, offs[n]:offs[n + 1]] for n in range(9)]
    lo = jnp.pad(parts[4], ((0, 0), (0, LANES - GATE_RANK)))
    w = jnp.concatenate(parts[0:4] + parts[5:9] + [lo], axis=1).astype(BF16)
    w_gu = jnp.pad(w_gate_up, ((0, LANES - GATE_RANK), (0, 0))).astype(BF16)
    return w, w_gu


def _prep_odd(w_in_c, w_uq, w_uk, w_uv):
    w_in = jnp.pad(w_in_c, ((0, 0), (0, LANES - ROPE))).astype(BF16)
    uq = w_uq.reshape(Q_LORA, H_C, NOPE + ROPE)
    w_uq_p = jnp.concatenate([uq[:, :, :NOPE].reshape(Q_LORA, H_C * NOPE),
                              uq[:, :, NOPE:].reshape(Q_LORA, H_C * ROPE)],
                             axis=1).astype(BF16)
    w_ukv = jnp.concatenate([w_uk, w_uv], axis=2).reshape(KV_LORA, H_C * HEAD_W).astype(BF16)
    w_ukt = jnp.transpose(w_uk, (1, 2, 0)).astype(BF16)
    w_uvh = jnp.transpose(w_uv, (1, 0, 2)).astype(BF16)
    return w_in, w_uq_p, w_ukv, w_ukt, w_uvh


def _log_gamma_row():
    lg = np.log1p(-np.exp2(-5.0 - np.arange(H_B, dtype=np.float32))).astype(np.float32)
    return jnp.asarray(np.repeat(lg, DK_B)[None, :])


def _to_time_major(a, batch, steps):
    return a.reshape(batch, steps, -1).transpose(1, 0, 2).reshape(batch * steps, -1)


def _to_batch_major(a, batch, steps):
    return a.reshape(steps, batch, -1).transpose(1, 0, 2).reshape(batch * steps, -1)


def _divisor_tile(n, pref):
    t = min(n, pref)
    while n % t:
        t //= 2
    return t


def _trunk(x, pos0, gla0, ret0, ckv_past, kr_past, conv0, w, time_major_ffn):
    batch, seq, d = x.shape
    m = batch * seq
    pos = pos0 + jnp.arange(seq, dtype=jnp.int32)
    row = lambda v: v.reshape(1, -1)
    scale = float((NOPE + ROPE) ** -0.5)
    tf = TF_FFN
    tm_proj = _divisor_tile(m, TM_PROJ)

    if time_major_ffn:
        ffn_kw = dict(tm=m, tf=tf, shift=batch, tiles_per_seq=1)
        to_ffn = lambda a: _to_time_major(a, batch, seq)
        from_ffn = lambda a: _to_batch_major(a, batch, seq)
        prep_state = lambda s: s.transpose(1, 0, 2).reshape(1, 2 * batch, D_FF)
        post_state = lambda s: s.reshape(2, batch, D_FF).transpose(1, 0, 2)
    else:
        tm_ffn = _divisor_tile(seq, TM_FFN)
        ffn_kw = dict(tm=tm_ffn, tf=tf, shift=1, tiles_per_seq=seq // tm_ffn)
        to_ffn = from_ffn = lambda a: a
        prep_state = lambda s: jnp.pad(s, ((0, 0), (SUBLANES - 2, 0), (0, 0)))
        post_state = lambda s: s

    xf = x.reshape(m, d)
    conv_new = []
    gla_new = ret_new = ckv_new = kr_new = None
    for layer in range(2):
        g_mix = row(w['norm_mix'][layer])
        if layer == 0:
            w_ab, w_gu = _prep_even(w['w_in_ab'][0], w['w_gate_up'][0])
            cos, sin = _rope_tables(pos, 4 * DK_B)
            rows = _divisor_tile(seq, ROWS_LINATTN)
            small = (w_gu, row(w['b_gate'][0]), row(w['g_gla'][0]), row(w['g_ret'][0]), cos, sin,
                     _log_gamma_row())
            if gla0 is None and rows >= 2 * CHUNK:
                mix, gla_new, ret_new = _proj_linattn(xf, g_mix, w_ab, batch, seq, rows, *small)
            else:
                slab = _norm_matmul(xf, g_mix, w_ab, tm_proj, 4 * HEAD_W)
                mix, gla_new, ret_new = _linattn(
                    slab, batch, seq, rows, *small,
                    None if gla0 is None else gla0[0], None if ret0 is None else ret0[0])
            w_o = w['w_out_ab'][0].astype(BF16)
        else:
            w_in, w_uq_p, w_ukv, w_ukt, w_uvh = _prep_odd(w['w_in_c'][0], w['w_uq'][0], w['w_uk'][0],
                                                          w['w_uv'][0])
            cos, sin = _rope_tables(pos, LANES)
            g_q, g_kv = row(w['g_q'][0]), row(w['g_kv'][0])
            if ckv_past is None:
                q, k, v, ckv_new, kr_new = _mla_proj_prompt(
                    xf, g_mix, w_in, g_q, g_kv, w_uq_p, w_ukv, cos, sin, seq, tm_proj,
                    scale * float(np.log2(np.e)))
                mix = _flash_prompt(q, k, v, batch, seq, _divisor_tile(seq, TQ_FLASH))
            else:
                assert ckv_past.shape[2] % CHUNK == 0 and seq <= CHUNK
                cos_m, sin_m = jnp.tile(cos, (batch, 1)), jnp.tile(sin, (batch, 1))
                q_lat, q_r, ckv_new, kr_new = _mla_proj_sample(
                    xf, g_mix, w_in, g_q, g_kv, w_uq_p, w_ukt, cos_m, sin_m, scale)
                new_ckv = ckv_new.reshape(batch, seq, KV_LORA)
                new_kr = kr_new.reshape(batch, seq, ROPE)
                mix = _latent_attn_sample(q_lat, q_r, ckv_past[0],
                                          jnp.swapaxes(kr_past[0], 1, 2), new_ckv, new_kr,
                                          w_uvh, seq, _divisor_tile(ckv_past.shape[2], TK_LATENT))
            w_o = w['w_out_c'][0].astype(BF16)

        if conv0 is None:
            state = jnp.zeros((batch, SUBLANES, D_FF), F32) if not time_major_ffn else \
                jnp.zeros((1, 2 * batch, D_FF), F32)
        else:
            state = prep_state(conv0[layer])
        y, conv_rows = _ffn(to_ffn(xf), to_ffn(mix), w_o, row(w['norm_ffn'][layer]),
                            w['w_ffn_in'].astype(BF16), w['w_dwconv'][layer],
                            row(w['b_dwconv'][layer]),
                            (0.5 * w['w_ffn_out'][layer]).astype(BF16), state,
                            row(w['norm_final']), layer=layer, final_norm=(layer == 1), **ffn_kw)
        xf = from_ffn(y)
        tps = ffn_kw['tiles_per_seq']
        conv_new.append(post_state(conv_rows[tps - 1::tps]))

    return (xf.reshape(batch, seq, d), gla_new[None], ret_new[None],
            ckv_new.reshape(1, batch, seq, KV_LORA), kr_new.reshape(1, batch, seq, ROPE),
            jnp.stack(conv_new))


def kernel(x_prompt, x_sample, state_gla, state_ret, cache_ckv, cache_krope, state_conv, norm_mix, norm_ffn, norm_final, w_in_ab, w_gate_up, b_gate, g_gla, g_ret, w_out_ab, w_in_c, g_q, g_kv, w_uq, w_uk, w_uv, w_out_c, w_ffn_in, w_dwconv, b_dwconv, w_ffn_out):
    w = {'norm_mix': norm_mix, 'norm_ffn': norm_ffn, 'norm_final': norm_final,
         'w_in_ab': w_in_ab, 'w_gate_up': w_gate_up, 'b_gate': b_gate, 'g_gla': g_gla,
         'g_ret': g_ret, 'w_out_ab': w_out_ab, 'w_in_c': w_in_c, 'g_q': g_q, 'g_kv': g_kv,
         'w_uq': w_uq, 'w_uk': w_uk, 'w_uv': w_uv, 'w_out_c': w_out_c, 'w_ffn_in': w_ffn_in,
         'w_dwconv': w_dwconv, 'b_dwconv': b_dwconv, 'w_ffn_out': w_ffn_out}
    past_len = cache_ckv.shape[2]
    y_p, gla_p, ret_p, ckv_p, kr_p, conv_p = _trunk(
        x_prompt, 0, None, None, None, None, None, w, time_major_ffn=False)
    y_s, gla_s, ret_s, ckv_s, kr_s, conv_s = _trunk(
        x_sample, past_len, state_gla, state_ret, cache_ckv, cache_krope, state_conv, w,
        time_major_ffn=True)
    return (y_p, y_s, gla_p, gla_s, ret_p, ret_s, ckv_p, ckv_s, kr_p, kr_s, conv_p, conv_s)
```
